```python
import functools
import jax
import jax.numpy as jnp
from jax import lax
import numpy as np

D_MODEL = 1024
BATCH = 8
SEQ = 2048
DEPTH = 4

GRID_W = 64
CTX_LEN = 256
HEAD_DIM = 128
RET_HEADS = 4
MLSTM_HEADS = 4
NA_HEADS = 4
RET_WIDTH = RET_HEADS * HEAD_DIM
MLSTM_WIDTH = MLSTM_HEADS * HEAD_DIM
NA_WIDTH = NA_HEADS * HEAD_DIM
N_BRANCHES = 3
GATE_COLS = N_BRANCHES * D_MODEL
IN_COLS = 4 * RET_WIDTH + 4 * MLSTM_WIDTH + 4 * MLSTM_HEADS + 3 * NA_WIDTH + GATE_COLS
CHUNK = 128
CONV_WIDTH = 5
NA_WIN_ROWS = 8
NA_WIN_COLS = 16
ROPE_BASE = 10000.0
N_GROUPS = 4
EXPERTS_PER_GROUP = 8
N_EXPERTS = N_GROUPS * EXPERTS_PER_GROUP
TOP_K = 2
EXPERT_FF = 512
EXPERT_BLOCK = 256
EPS = 1e-6
NEG_INF = -1e30

kernel_name = 'hybrid_ret_mlstm_natten_hmoe_dit'


def rms_norm(x, g):
    x32 = x.astype(jnp.float32)
    y = x32 * lax.rsqrt(jnp.mean(x32 * x32, axis=-1, keepdims=True) + EPS)
    return (y * g.astype(jnp.float32)).astype(x.dtype)


def modulate(h, shift, scale):
    return h * (1 + scale) + shift


def head_group_norm(y, g):
    b, s, h, dh = y.shape
    y32 = y.astype(jnp.float32)
    mu = jnp.mean(y32, axis=-1, keepdims=True)
    var = jnp.mean(jnp.square(y32 - mu), axis=-1, keepdims=True)
    y32 = (y32 - mu) * lax.rsqrt(var + EPS)
    return (y32.reshape(b, s, h * dh) * g.astype(jnp.float32)).astype(y.dtype)


def split_heads(a, n_heads):
    return a.reshape(a.shape[0], a.shape[1], n_heads, HEAD_DIM)


def to_scan(a):
    return jnp.swapaxes(a, 1, 2).astype(jnp.float32)


def from_scan(a, dtype):
    return jnp.swapaxes(a, 1, 2).astype(dtype)


def split_projection(p):
    sizes = [RET_WIDTH] * 4 + [MLSTM_WIDTH] * 4 + [4 * MLSTM_HEADS] + [NA_WIDTH] * 3 + [GATE_COLS]
    return jnp.split(p, np.cumsum(sizes)[:-1].tolist(), axis=-1)


def axial_rope_tables(n_tokens):
    t = jnp.arange(n_tokens)
    rows = (t // GRID_W).astype(jnp.float32)
    cols = (t % GRID_W).astype(jnp.float32)
    n_freq = HEAD_DIM // 4
    inv_freq = ROPE_BASE ** (-jnp.arange(n_freq, dtype=jnp.float32) / n_freq)
    ang = jnp.stack([rows[:, None] * inv_freq, cols[:, None] * inv_freq], axis=1)
    return jnp.cos(ang)[:, None, :, None, :], jnp.sin(ang)[:, None, :, None, :]


def apply_axial_rope(x, cos, sin):
    b, s, h, dh = x.shape
    xr = x.astype(jnp.float32).reshape(b, s, h, 2, 2, dh // 4)
    rot = jnp.stack([-xr[..., 1, :], xr[..., 0, :]], axis=-2)
    return (xr * cos + rot * sin).reshape(b, s, h, dh).astype(x.dtype)


def retention_scan(q, k, v, log_gamma, state0, with_out):
    b, h, s, dh = q.shape
    n_chunks = s // CHUNK
    idx = jnp.arange(CHUNK, dtype=jnp.float32)
    diff = idx[:, None] - idx[None, :]
    tri = diff >= 0
    intra = jnp.where(tri, jnp.exp(jnp.where(tri, diff, 0.0)[None] * log_gamma[:, None, None]), 0.0)
    q_decay = jnp.exp((idx + 1.0)[None] * log_gamma[:, None])[:, :, None]
    k_decay = jnp.exp((CHUNK - 1.0 - idx)[None] * log_gamma[:, None])[:, :, None]
    chunk_decay = jnp.exp(CHUNK * log_gamma)[:, None, None]

    def chunks(a):
        return jnp.moveaxis(a.reshape(b, h, n_chunks, CHUNK, dh), 2, 0)

    def step(s_prev, inp):
        qb, kb, vb = inp
        s_new = s_prev * chunk_decay + jnp.einsum('bhcd,bhce->bhde', kb * k_decay, vb)
        if not with_out:
            return s_new, None
        scores = jnp.einsum('bhid,bhjd->bhij', qb, kb) * intra
        o = (jnp.einsum('bhij,bhje->bhie', scores, vb)
             + jnp.einsum('bhid,bhde->bhie', qb * q_decay, s_prev))
        return s_new, o

    s_fin, o = lax.scan(step, state0, (chunks(q), chunks(k), chunks(v)))
    if not with_out:
        return None, s_fin
    return jnp.moveaxis(o, 0, 2).reshape(b, h, s, dh), s_fin


def mlstm_scan(q, k, v, i_pre, log_f, state0, with_out):
    b, h, s, dh = q.shape
    n_chunks = s // CHUNK
    tri = jnp.tril(jnp.ones((CHUNK, CHUNK), dtype=bool))

    def chunks(a):
        return jnp.moveaxis(a.reshape(b, h, n_chunks, CHUNK, *a.shape[3:]), 2, 0)

    def step(carry, inp):
        c_prev, n_prev, m_prev = carry
        qb, kb, vb, ib, fb = inp
        cum_f = jnp.cumsum(fb, axis=-1)
        total_f = cum_f[..., -1]
        log_kw = total_f[..., None] - cum_f + ib
        m_new = jnp.maximum(total_f + m_prev, jnp.max(log_kw, axis=-1))
        kw = jnp.exp(log_kw - m_new[..., None])
        pw = jnp.exp(total_f + m_prev - m_new)
        c_new = pw[..., None, None] * c_prev + jnp.einsum('bhjd,bhje->bhde', kb * kw[..., None], vb)
        n_new = pw[..., None] * n_prev + jnp.einsum('bhj,bhjd->bhd', kw, kb)
        if not with_out:
            return (c_new, n_new, m_new), None
        log_w = jnp.where(tri, cum_f[..., :, None] - cum_f[..., None, :] + ib[..., None, :], -jnp.inf)
        log_p = cum_f + m_prev[..., None]
        m_t = jnp.maximum(log_p, jnp.max(log_w, axis=-1))
        w = jnp.exp(log_w - m_t[..., None])
        p = jnp.exp(log_p - m_t)
        qk = jnp.einsum('bhid,bhjd->bhij', qb, kb) * w
        num = jnp.einsum('bhij,bhje->bhie', qk, vb) + p[..., None] * jnp.einsum('bhid,bhde->bhie', qb, c_prev)
        den = jnp.sum(qk, axis=-1) + p * jnp.einsum('bhid,bhd->bhi', qb, n_prev)
        h_t = num / jnp.maximum(jnp.abs(den), jnp.exp(-m_t))[..., None]
        return (c_new, n_new, m_new), h_t

    state, hs = lax.scan(step, state0, (chunks(q), chunks(k), chunks(v), chunks(i_pre), chunks(log_f)))
    if not with_out:
        return None, state
    return jnp.moveaxis(hs, 0, 2).reshape(b, h, s, dh), state


def bidirectional_scan(scan_f, scan_b, lat_f, lat_b, ctx_f, ctx_b, state0, ctx_out):
    flip = lambda arrs: tuple(jnp.flip(a, axis=2) for a in arrs)
    o_ctx_f, st_f = scan_f(*ctx_f, state0=state0, with_out=ctx_out)
    o_ctx_b, st_b = scan_b(*flip(ctx_b), state0=state0, with_out=ctx_out)
    o_lat_f, _ = scan_f(*lat_f, state0=st_f, with_out=True)
    o_lat_b, _ = scan_b(*flip(lat_b), state0=st_b, with_out=True)
    y_lat = o_lat_f + jnp.flip(o_lat_b, axis=2)
    if not ctx_out:
        return y_lat, None
    return y_lat, o_ctx_f + jnp.flip(o_ctx_b, axis=2)


def short_conv(u, w, bias):
    y = lax.conv_general_dilated(
        u, w[:, None, :].astype(u.dtype), window_strides=(1,),
        padding=[(CONV_WIDTH // 2, CONV_WIDTH // 2)],
        dimension_numbers=('NWC', 'WIO', 'NWC'), feature_group_count=u.shape[-1])
    return jax.nn.silu(y + bias)


def neighbourhood_attention(q, k, v, k_ctx, v_ctx, rpb):
    b, s, h, dh = q.shape
    rows = s // GRID_W
    win_r = min(NA_WIN_ROWS, rows)
    scale = dh ** -0.5
    qg = q.reshape(b, rows, GRID_W, h, dh)
    kg = k.reshape(b, rows, GRID_W, h, dh)
    vg = v.reshape(b, rows, GRID_W, h, dh)
    r = jnp.arange(rows)
    row_start = jnp.clip(r - win_r // 2, 0, rows - win_r)
    key_rows = row_start[:, None] + jnp.arange(win_r)[None, :]
    kb = kg[:, key_rows]
    vb = vg[:, key_rows]
    col = jnp.arange(GRID_W)
    col_start = jnp.clip(col - NA_WIN_COLS // 2, 0, GRID_W - NA_WIN_COLS)
    col_ok = (col[None, :] >= col_start[:, None]) & (col[None, :] < col_start[:, None] + NA_WIN_COLS)
    row_idx = (key_rows - r[:, None] + NA_WIN_ROWS - 1)[:, None, :, None]
    col_idx = jnp.clip(col[None, :] - col[:, None] + NA_WIN_COLS - 1, 0, 2 * NA_WIN_COLS - 2)[None, :, None, :]
    bias = rpb[:, row_idx, col_idx].astype(jnp.float32)
    s_loc = jnp.einsum('brqhd,brwkhd->bhrqwk', qg, kb).astype(jnp.float32) * scale + bias
    s_loc = jnp.where(col_ok[:, None, :], s_loc, NEG_INF)
    s_ctx = jnp.einsum('brqhd,bchd->bhrqc', qg, k_ctx).astype(jnp.float32) * scale
    n_loc = win_r * GRID_W
    probs = jax.nn.softmax(jnp.concatenate([s_loc.reshape(b, h, rows, GRID_W, n_loc), s_ctx], axis=-1), axis=-1)
    probs = probs.astype(q.dtype)
    p_loc = probs[..., :n_loc].reshape(b, h, rows, GRID_W, win_r, GRID_W)
    p_ctx = probs[..., n_loc:]
    o = jnp.einsum('bhrqwk,brwkhd->brqhd', p_loc, vb) + jnp.einsum('bhrqc,bchd->brqhd', p_ctx, v_ctx)
    return o.reshape(b, s, h, dh)


def context_attention(q, k, v):
    s = jnp.einsum('bqhd,bkhd->bhqk', q, k).astype(jnp.float32) * (q.shape[-1] ** -0.5)
    p = jax.nn.softmax(s, axis=-1).astype(q.dtype)
    return jnp.einsum('bhqk,bkhd->bqhd', p, v)


def merge_branches(ys, gate_pre, w_branch, w_out):
    g = jax.nn.sigmoid(gate_pre).reshape(*gate_pre.shape[:-1], N_BRANCHES, gate_pre.shape[-1] // N_BRANCHES)
    proj = jnp.einsum('...nc,ncd->...nd', jnp.stack(ys, axis=-2), w_branch)
    return jnp.sum(g * proj, axis=-2) @ w_out


def token_mixer(h_lat, h_ctx, w_in, ret_decay, ret_norm_g, conv_w, conv_b, mlstm_gate_b,
                mlstm_norm_g, na_rpb, w_branch, w_out, ctx_out):
    b, s, _ = h_lat.shape
    n_ctx = h_ctx.shape[1]
    dt = h_lat.dtype
    lat = split_projection(h_lat @ w_in)
    ctx = split_projection(h_ctx @ w_in)
    qk_scale = HEAD_DIM ** -0.5

    cos, sin = axial_rope_tables(s)

    def retention_inputs(p, rotary):
        q, k, v = (split_heads(a, RET_HEADS) for a in p[0:3])
        if rotary:
            q, k = apply_axial_rope(q, cos, sin), apply_axial_rope(k, cos, sin)
        return to_scan(q), to_scan(k) * qk_scale, to_scan(v)

    log_gamma = jax.nn.log_sigmoid(ret_decay.astype(jnp.float32))
    ret_lat = retention_inputs(lat, True)
    ret_ctx = retention_inputs(ctx, False)
    r_lat, r_ctx = bidirectional_scan(
        functools.partial(retention_scan, log_gamma=log_gamma[0]),
        functools.partial(retention_scan, log_gamma=log_gamma[1]),
        ret_lat, ret_lat, ret_ctx, ret_ctx,
        jnp.zeros((b, RET_HEADS, HEAD_DIM, HEAD_DIM), jnp.float32), ctx_out)

    def retention_out(r, p):
        return head_group_norm(from_scan(r, dt), ret_norm_g) * jax.nn.silu(p[3])

    def mlstm_inputs(p):
        qk = short_conv(jnp.concatenate([p[4], p[5]], axis=-1), conv_w, conv_b)
        q, k = jnp.split(qk, 2, axis=-1)
        q, k, v = (to_scan(split_heads(a, MLSTM_HEADS)) for a in (q, k, p[6]))
        k = k * qk_scale
        g = p[8].astype(jnp.float32).reshape(p[8].shape[0], p[8].shape[1], 4, MLSTM_HEADS)
        g = jnp.moveaxis(g + mlstm_gate_b.astype(jnp.float32), 1, -1)
        fwd = (q, k, v, g[:, 0], jax.nn.log_sigmoid(g[:, 1]))
        bwd = (q, k, v, g[:, 2], jax.nn.log_sigmoid(g[:, 3]))
        return fwd, bwd

    m_lat_f, m_lat_b = mlstm_inputs(lat)
    m_ctx_f, m_ctx_b = mlstm_inputs(ctx)
    m0 = (jnp.zeros((b, MLSTM_HEADS, HEAD_DIM, HEAD_DIM), jnp.float32),
          jnp.zeros((b, MLSTM_HEADS, HEAD_DIM), jnp.float32),
          jnp.zeros((b, MLSTM_HEADS), jnp.float32))
    m_lat, m_ctx = bidirectional_scan(mlstm_scan, mlstm_scan, m_lat_f, m_lat_b, m_ctx_f, m_ctx_b, m0, ctx_out)

    def mlstm_out(hm, p):
        o_gate = jax.nn.sigmoid(split_heads(p[7], MLSTM_HEADS))
        return head_group_norm(from_scan(hm, dt) * o_gate, mlstm_norm_g)

    nq_l, nk_l, nv_l = (split_heads(a, NA_HEADS) for a in lat[9:12])
    nk_c, nv_c = (split_heads(a, NA_HEADS) for a in ctx[10:12])
    na_lat = neighbourhood_attention(nq_l, nk_l, nv_l, nk_c, nv_c, na_rpb).reshape(b, s, NA_WIDTH)

    y_lat = merge_branches((retention_out(r_lat, lat), mlstm_out(m_lat, lat), na_lat), lat[12], w_branch, w_out)
    if not ctx_out:
        return y_lat, None
    nq_c = split_heads(ctx[9], NA_HEADS)
    na_ctx = context_attention(nq_c, nk_c, nv_c).reshape(b, n_ctx, NA_WIDTH)
    y_ctx = merge_branches((retention_out(r_ctx, ctx), mlstm_out(m_ctx, ctx), na_ctx), ctx[12], w_branch, w_out)
    return y_lat, y_ctx


def hierarchical_moe(h, w_group, w_router, w_gate, w_up, w_down):
    n_tok, d = h.shape
    rows = jnp.arange(n_tok)
    group_logits = (h @ w_group).astype(jnp.float32)
    group = jnp.argmax(group_logits, axis=-1)
    group_w = jax.nn.softmax(group_logits, axis=-1)[rows, group]
    expert_logits = (h @ w_router).astype(jnp.float32).reshape(n_tok, N_GROUPS, EXPERTS_PER_GROUP)
    top_logit, top_idx = lax.top_k(expert_logits[rows, group], TOP_K)
    weight = group_w[:, None] * jax.nn.softmax(top_logit, axis=-1)
    expert = group[:, None] * EXPERTS_PER_GROUP + top_idx

    n_assign = n_tok * TOP_K
    e_flat = expert.reshape(n_assign)
    order = jnp.argsort(e_flat)
    e_sorted = e_flat[order]
    tok_sorted = order // TOP_K
    w_sorted = weight.reshape(n_assign)[order]
    counts = jnp.zeros((N_EXPERTS,), jnp.int32).at[e_flat].add(1)
    padded = (counts + EXPERT_BLOCK - 1) // EXPERT_BLOCK * EXPERT_BLOCK
    pad_end = jnp.cumsum(padded)
    pad_start = pad_end - padded
    start = jnp.cumsum(counts) - counts
    dest = pad_start[e_sorted] + jnp.arange(n_assign) - start[e_sorted]
    n_blocks = (n_assign + N_EXPERTS * (EXPERT_BLOCK - 1) + EXPERT_BLOCK - 1) // EXPERT_BLOCK
    xs = jnp.zeros((n_blocks * EXPERT_BLOCK, d), h.dtype).at[dest].set(h[tok_sorted])
    block_expert = jnp.minimum(
        jnp.sum(jnp.arange(n_blocks)[:, None] * EXPERT_BLOCK >= pad_end[None, :], axis=1), N_EXPERTS - 1)

    def expert_ffn(args):
        xb, e = args
        return (jax.nn.silu(xb @ w_gate[e]) * (xb @ w_up[e])) @ w_down[e]

    ys = lax.map(expert_ffn, (xs.reshape(n_blocks, EXPERT_BLOCK, d), block_expert)).reshape(-1, d)
    return jnp.zeros((n_tok, d), h.dtype).at[tok_sorted].add(ys[dest] * w_sorted[:, None].astype(h.dtype))


def setup_inputs(seed: int = 0) -> dict:
    key = jax.random.key(seed)
    ks = jax.random.split(key, 24)
    f32 = jnp.float32
    L, D = DEPTH, D_MODEL

    def normal(k, shape, std):
        return jax.random.normal(k, shape, f32) * std

    heads = jnp.arange(RET_HEADS, dtype=f32)
    ret_logit = jnp.log(2.0 ** (5.0 + heads) - 1.0)
    f_bias = jnp.linspace(3.0, 6.0, MLSTM_HEADS, dtype=f32)
    zeros_h = jnp.zeros((MLSTM_HEADS,), f32)
    gate_bias = jnp.stack([zeros_h, f_bias, zeros_h, f_bias])
    return {
        'x': normal(ks[0], (BATCH, SEQ, D), 1.0),
        'c': normal(ks[1], (BATCH, D), 1.0),
        'ctx': normal(ks[2], (BATCH, CTX_LEN, D), 1.0),
        'c_ctx': normal(ks[3], (D,), 1.0),
        'w_mod': normal(ks[4], (L, D, 6 * D), 0.5 * D ** -0.5),
        'b_mod': normal(ks[5], (L, 6 * D), 0.02),
        'norm1_g': 1.0 + normal(ks[6], (L, D), 0.02),
        'norm2_g': 1.0 + normal(ks[7], (L, D), 0.02),
        'w_in': normal(ks[8], (L, D, IN_COLS), D ** -0.5),
        'ret_decay': ret_logit + normal(ks[9], (L, 2, RET_HEADS), 0.1),
        'ret_norm_g': 1.0 + normal(ks[10], (L, RET_WIDTH), 0.02),
        'conv_w': normal(ks[11], (L, CONV_WIDTH, 2 * MLSTM_WIDTH), CONV_WIDTH ** -0.5),
        'conv_b': normal(ks[12], (L, 2 * MLSTM_WIDTH), 0.02),
        'mlstm_gate_b': gate_bias + normal(ks[13], (L, 4, MLSTM_HEADS), 0.1),
        'mlstm_norm_g': 1.0 + normal(ks[14], (L, MLSTM_WIDTH), 0.02),
        'na_rpb': normal(ks[15], (L, NA_HEADS, 2 * NA_WIN_ROWS - 1, 2 * NA_WIN_COLS - 1), 0.1),
        'w_branch': normal(ks[16], (L, N_BRANCHES, RET_WIDTH, D), RET_WIDTH ** -0.5),
        'w_out': normal(ks[17], (L, D, D), D ** -0.5),
        'w_group': normal(ks[18], (L, D, N_GROUPS), D ** -0.5),
        'w_router': normal(ks[19], (L, D, N_EXPERTS), D ** -0.5),
        'w_expert_gate': normal(ks[20], (L, N_EXPERTS, D, EXPERT_FF), D ** -0.5),
        'w_expert_up': normal(ks[21], (L, N_EXPERTS, D, EXPERT_FF), D ** -0.5),
        'w_expert_down': normal(ks[22], (L, N_EXPERTS, EXPERT_FF, D), EXPERT_FF ** -0.5),
        'final_norm_g': 1.0 + normal(ks[23], (D,), 0.02),
    }


def reference(x, c, ctx, c_ctx, w_mod, b_mod, norm1_g, norm2_g, w_in, ret_decay, ret_norm_g,
              conv_w, conv_b, mlstm_gate_b, mlstm_norm_g, na_rpb, w_branch, w_out,
              w_group, w_router, w_expert_gate, w_expert_up, w_expert_down, final_norm_g):
    b, s, d = x.shape
    n_ctx = ctx.shape[1]
    cond_lat = jax.nn.silu(c)
    cond_ctx = jax.nn.silu(c_ctx)
    xc = ctx
    for layer in range(DEPTH):
        ctx_needed = layer < DEPTH - 1
        sh1, sc1, g1, sh2, sc2, g2 = jnp.split(cond_lat @ w_mod[layer] + b_mod[layer], 6, axis=-1)
        csh1, csc1, cg1, csh2, csc2, cg2 = jnp.split(cond_ctx @ w_mod[layer] + b_mod[layer], 6, axis=-1)
        h_lat = modulate(rms_norm(x, norm1_g[layer]), sh1[:, None], sc1[:, None])
        h_ctx = modulate(rms_norm(xc, norm1_g[layer]), csh1, csc1)
        y_lat, y_ctx = token_mixer(h_lat, h_ctx, w_in[layer], ret_decay[layer], ret_norm_g[layer],
                                   conv_w[layer], conv_b[layer], mlstm_gate_b[layer], mlstm_norm_g[layer],
                                   na_rpb[layer], w_branch[layer], w_out[layer], ctx_needed)
        x = x + g1[:, None] * y_lat
        h2_lat = modulate(rms_norm(x, norm2_g[layer]), sh2[:, None], sc2[:, None]).reshape(b * s, d)
        moe_w = (w_group[layer], w_router[layer], w_expert_gate[layer], w_expert_up[layer], w_expert_down[layer])
        if ctx_needed:
            xc = xc + cg1 * y_ctx
            h2_ctx = modulate(rms_norm(xc, norm2_g[layer]), csh2, csc2).reshape(b * n_ctx, d)
            f = hierarchical_moe(jnp.concatenate([h2_lat, h2_ctx], axis=0), *moe_w)
            x = x + g2[:, None] * f[: b * s].reshape(b, s, d)
            xc = xc + cg2 * f[b * s:].reshape(b, n_ctx, d)
        else:
            f = hierarchical_moe(h2_lat, *moe_w)
            x = x + g2[:, None] * f.reshape(b, s, d)
    return rms_norm(x, final_norm_g)
```

```python
import functools

import numpy as np
import jax
import jax.numpy as jnp
from jax import lax
from jax.experimental import pallas as pl
from jax.experimental.pallas import tpu as pltpu

F32 = jnp.float32
BF16 = jnp.bfloat16

D_MODEL = 1024
SEQ = 2048
CTX_LEN = 256
TOK = CTX_LEN + SEQ
GRID_W = 64
GRID_ROWS = SEQ // GRID_W
HEAD_DIM = 128
N_HEADS = 4
WIDTH = N_HEADS * HEAD_DIM
CHUNK = 128
N_CHUNKS = TOK // CHUNK
CTX_CHUNKS = CTX_LEN // CHUNK
CONV_WIDTH = 5
NA_WIN_ROWS = 8
NA_WIN_COLS = 16
NA_Q_ROWS = 4
NA_Q = NA_Q_ROWS * GRID_W
NA_K_ROWS = NA_Q_ROWS + NA_WIN_ROWS
NA_K = NA_K_ROWS * GRID_W
NA_STEPS = SEQ // NA_Q
ROPE_BASE = 10000.0
N_GROUPS = 4
EXPERTS_PER_GROUP = 8
N_EXPERTS = N_GROUPS * EXPERTS_PER_GROUP
EXPERT_FF = 512
EXPERT_BLOCK = 256
EPS = 1e-6
NEG_INF = -1e30
QK_SCALE = HEAD_DIM ** -0.5

TM = 256
ROUTE_TM = 512
LANES = 128
VMEM_LIMIT = 56 * 1024 * 1024


def _dot(a, b):
    return jnp.dot(a, b, preferred_element_type=F32)


def _dot_nt(a, b):
    return lax.dot_general(a, b, (((1,), (1,)), ((), ())), preferred_element_type=F32)


def _bdot(a, b):
    return _dot(a.astype(BF16), b.astype(BF16))


def _bdot_nt(a, b):
    return _dot_nt(a.astype(BF16), b.astype(BF16))


def _bdot_tn(a, b):
    return _dot(a.T.astype(BF16), b.astype(BF16))


def _log_sigmoid(x):
    return jnp.minimum(x, 0.0) - jnp.log1p(jnp.exp(-jnp.abs(x)))


def _silu(x):
    return x * jax.nn.sigmoid(x)


def _params(sem, vmem=None):
    return pltpu.CompilerParams(dimension_semantics=sem, vmem_limit_bytes=vmem)


def _mod_row(n_batch):
    tiles = TOK // TM

    def index(i):
        return (jnp.where(i % tiles == 0, n_batch, i // tiles), 0, 0)

    return index


def _mod_kernel(c_ref, w_ref, b_ref, o_ref):
    cond = _silu(c_ref[...])
    o_ref[0] = _bdot(cond, w_ref[0]) + b_ref[0]


def _modulation(cond, w_mod, b_mod):
    depth, d, cols = w_mod.shape
    rows = cond.shape[0]
    tn = 1536
    return pl.pallas_call(
        _mod_kernel,
        grid=(depth, cols // tn),
        in_specs=[
            pl.BlockSpec((rows, d), lambda l, j: (0, 0)),
            pl.BlockSpec((1, d, tn), lambda l, j: (l, 0, j)),
            pl.BlockSpec((1, 1, tn), lambda l, j: (l, 0, j)),
        ],
        out_specs=pl.BlockSpec((1, rows, tn), lambda l, j: (l, 0, j)),
        out_shape=jax.ShapeDtypeStruct((depth, rows, cols), F32),
        compiler_params=_params(("parallel", "parallel")),
        name="modulation",
    )(cond, w_mod, b_mod.reshape(depth, 1, cols))


def _rms_modulate(x, g, shift, scale):
    y = x * lax.rsqrt(jnp.mean(x * x, axis=-1, keepdims=True) + EPS) * g
    return y * (1.0 + scale) + shift


def _inproj_kernel(x_ref, mod_ref, g_ref, w_rm_ref, w_gate_ref, w_na_ref, w_bg_ref,
                   ret_ref, ml_ref, gate_ref, na_ref, bg_ref):
    h = _rms_modulate(x_ref[...], g_ref[...], mod_ref[0, 0:1, :], mod_ref[0, 1:2, :]).astype(BF16)
    ret_ref[...] = _dot(h, w_rm_ref[:, : 4 * WIDTH])
    ml_ref[...] = _dot(h, w_rm_ref[:, 4 * WIDTH:])
    gate_ref[...] = _dot(h, w_gate_ref[...])
    na_ref[...] = _dot(h, w_na_ref[...])
    bg_ref[...] = _dot(h, w_bg_ref[...])


def _in_projection(x_all, mod, norm_g, w_rm, w_gate, w_na, w_bg, n_batch):
    n = x_all.shape[0]
    const = lambda i: (0, 0)
    row = lambda i: (i, 0)

    def wspec(w):
        return pl.BlockSpec(w.shape, const, pipeline_mode=pl.Buffered(1))

    widths = (4 * WIDTH, 4 * WIDTH, LANES, 3 * WIDTH, 3 * D_MODEL)
    return pl.pallas_call(
        _inproj_kernel,
        grid=(n // TM,),
        in_specs=[
            pl.BlockSpec((TM, D_MODEL), row),
            pl.BlockSpec((1, 6, D_MODEL), _mod_row(n_batch)),
            pl.BlockSpec((1, D_MODEL), const),
            wspec(w_rm), wspec(w_gate), wspec(w_na), wspec(w_bg),
        ],
        out_specs=[pl.BlockSpec((TM, w), row) for w in widths],
        out_shape=[jax.ShapeDtypeStruct((n, w), F32) for w in widths],
        compiler_params=_params(("parallel",), VMEM_LIMIT),
        name="in_projection",
    )(x_all, mod, norm_g.reshape(1, D_MODEL), w_rm, w_gate, w_na, w_bg)


def _chunk_order(t):
    fwd = t
    bwd = jnp.where(t < CTX_CHUNKS, CTX_CHUNKS - 1 - t, N_CHUNKS + CTX_CHUNKS - 1 - t)
    return fwd, bwd


def _chunk_slice(c):
    return pl.ds(pl.multiple_of(c * CHUNK, CHUNK), CHUNK)


def _head_norm(y, gain):
    mu = jnp.mean(y, axis=-1, keepdims=True)
    yc = y - mu
    var = jnp.mean(yc * yc, axis=-1, keepdims=True)
    return yc * lax.rsqrt(var + EPS) * gain


def _select_lane(x, lane, idx):
    return jnp.sum(jnp.where(lane == idx, x, 0.0), axis=-1, keepdims=True)


def _retention_kernel(dec_ref, q_ref, k_ref, v_ref, g_ref, cos_ref, sin_ref, gn_ref, o_ref,
                      qs_ref, ks_ref, st_ref):
    head = pl.program_id(1)

    lane = lax.broadcasted_iota(jnp.int32, (SEQ, HEAD_DIM), 1)
    first_half = (lane % (HEAD_DIM // 2)) < (HEAD_DIM // 4)
    cos = cos_ref[...]
    sin = sin_ref[...]

    def rope(x):
        rot = jnp.where(first_half, pltpu.roll(x, HEAD_DIM - HEAD_DIM // 4, 1), pltpu.roll(x, HEAD_DIM // 4, 1))
        return x * cos + rot * sin

    qs_ref[0:CTX_LEN, :] = q_ref[0:CTX_LEN, :]
    ks_ref[0:CTX_LEN, :] = k_ref[0:CTX_LEN, :] * QK_SCALE
    qs_ref[CTX_LEN:, :] = rope(q_ref[CTX_LEN:, :])
    ks_ref[CTX_LEN:, :] = rope(k_ref[CTX_LEN:, :]) * QK_SCALE

    dec = _log_sigmoid(dec_ref[...])
    hl = lax.broadcasted_iota(jnp.int32, dec.shape, 1)
    lg = jnp.sum(jnp.where(hl == head, dec, 0.0), axis=-1, keepdims=True)
    lg_f, lg_b = lg[0:1, :], lg[1:2, :]

    ii = lax.broadcasted_iota(jnp.int32, (CHUNK, CHUNK), 0).astype(F32)
    jj = lax.broadcasted_iota(jnp.int32, (CHUNK, CHUNK), 1).astype(F32)
    col = lax.broadcasted_iota(jnp.int32, (CHUNK, 1), 0).astype(F32)

    def decay_mat(dist, lg_dir):
        ok = dist >= 0
        return jnp.where(ok, jnp.exp(jnp.where(ok, dist, 0.0) * lg_dir), 0.0)

    intra = (decay_mat(ii - jj, lg_f), decay_mat(jj - ii, lg_b))
    q_decay = (jnp.exp((col + 1.0) * lg_f), jnp.exp((CHUNK - col) * lg_b))
    k_decay = (jnp.exp((CHUNK - 1.0 - col) * lg_f), jnp.exp(col * lg_b))
    chunk_decay = (jnp.exp(CHUNK * lg_f), jnp.exp(CHUNK * lg_b))

    st_ref[...] = jnp.zeros_like(st_ref)
    o_ref[...] = jnp.zeros_like(o_ref)

    def step(t, carry):
        for d, c in enumerate(_chunk_order(t)):
            rows = _chunk_slice(c)
            q = qs_ref[rows, :]
            k = ks_ref[rows, :]
            v = v_ref[rows, :]
            s_prev = st_ref[d]
            scores = _bdot_nt(q, k) * intra[d]
            o_ref[rows, :] += _bdot(scores, v) + _bdot(q * q_decay[d], s_prev)
            st_ref[d] = s_prev * chunk_decay[d] + _bdot_tn(k * k_decay[d], v)
        return carry

    lax.fori_loop(0, N_CHUNKS, step, 0)

    o_ref[...] = _head_norm(o_ref[...], gn_ref[...]) * _silu(g_ref[...])


def _retention(ret, ret_decay, norm_g, cos, sin, n_batch):
    n = ret.shape[0]

    def head_block(offset):
        return pl.BlockSpec((TOK, HEAD_DIM), lambda b, h: (b, offset + h))

    return pl.pallas_call(
        _retention_kernel,
        grid=(n_batch, N_HEADS),
        in_specs=[
            pl.BlockSpec(ret_decay.shape, lambda b, h: (0, 0)),
            head_block(0), head_block(N_HEADS), head_block(2 * N_HEADS), head_block(3 * N_HEADS),
            pl.BlockSpec((SEQ, HEAD_DIM), lambda b, h: (0, 0)),
            pl.BlockSpec((SEQ, HEAD_DIM), lambda b, h: (0, 0)),
            pl.BlockSpec((1, HEAD_DIM), lambda b, h: (0, h)),
        ],
        out_specs=pl.BlockSpec((TOK, HEAD_DIM), lambda b, h: (b, h)),
        out_shape=jax.ShapeDtypeStruct((n, WIDTH), F32),
        scratch_shapes=[
            pltpu.VMEM((TOK, HEAD_DIM), F32),
            pltpu.VMEM((TOK, HEAD_DIM), F32),
            pltpu.VMEM((2, HEAD_DIM, HEAD_DIM), F32),
        ],
        compiler_params=_params(("parallel", "parallel")),
        name="retention",
    )(ret_decay, ret, ret, ret, ret, cos, sin, norm_g.reshape(1, WIDTH))


CONV_PAD = 8


def _mlstm_kernel(q_ref, k_ref, v_ref, og_ref, gate_ref, gb_ref, cwq_ref, cwk_ref, cbq_ref, cbk_ref,
                  gn_ref, o_ref, qs_ref, ks_ref, pad_ref, gx_ref, gxt_ref, cst_ref):
    head = pl.program_id(1)

    trow = lax.broadcasted_iota(jnp.int32, (TOK, 1), 0)
    pad_ref[0:CONV_PAD, :] = jnp.zeros((CONV_PAD, HEAD_DIM), F32)
    pad_ref[CONV_PAD + TOK:, :] = jnp.zeros((CONV_PAD, HEAD_DIM), F32)

    def conv(u_ref, w_ref, b_ref):
        pad_ref[CONV_PAD:CONV_PAD + TOK, :] = u_ref[...]
        acc = jnp.zeros((TOK, HEAD_DIM), F32)
        for j in range(CONV_WIDTH):
            shift = j - CONV_WIDTH // 2
            tap = pad_ref[CONV_PAD + shift:CONV_PAD + shift + TOK, :]
            same_segment = (trow < CTX_LEN) == (trow + shift < CTX_LEN)
            acc = acc + jnp.where(same_segment, tap, 0.0) * w_ref[j:j + 1, :]
        return _silu(acc + b_ref[...])

    qs_ref[...] = conv(q_ref, cwq_ref, cbq_ref)
    ks_ref[...] = conv(k_ref, cwk_ref, cbk_ref) * QK_SCALE

    lane = lax.broadcasted_iota(jnp.int32, (TOK, LANES), 1)
    g = gate_ref[...] + gb_ref[...]
    is_forget = ((lane // N_HEADS) % 2) == 1
    gx_ref[...] = jnp.where(is_forget, _log_sigmoid(g), g)
    for c in range(N_CHUNKS):
        gxt_ref[c] = gx_ref[c * CHUNK:(c + 1) * CHUNK, :].T

    ii = lax.broadcasted_iota(jnp.int32, (CHUNK, CHUNK), 0)
    jj = lax.broadcasted_iota(jnp.int32, (CHUNK, CHUNK), 1)
    causal = (jj <= ii, jj >= ii)
    clane = lax.broadcasted_iota(jnp.int32, (CHUNK, LANES), 1)

    cst_ref[...] = jnp.zeros_like(cst_ref)
    o_ref[...] = jnp.zeros_like(o_ref)

    def step(t, carry):
        new_carry = []
        for d, c in enumerate(_chunk_order(t)):
            n_prev, m_prev = carry[2 * d], carry[2 * d + 1]
            rows = _chunk_slice(c)
            q = qs_ref[rows, :]
            k = ks_ref[rows, :]
            v = v_ref[rows, :]
            gc = gx_ref[rows, :]
            i_col = _select_lane(gc, clane, 2 * d * N_HEADS + head)
            f_col = _select_lane(gc, clane, (2 * d + 1) * N_HEADS + head)
            i_row = gxt_ref[c, pl.ds(2 * d * N_HEADS + head, 1), :]
            f_row = gxt_ref[c, pl.ds((2 * d + 1) * N_HEADS + head, 1), :]
            vis = causal[d]
            cum_col = jnp.sum(jnp.where(vis, f_row, 0.0), axis=1, keepdims=True)
            cum_row = jnp.sum(jnp.where(causal[1 - d], f_col, 0.0), axis=0, keepdims=True)
            total = jnp.sum(f_row, axis=1, keepdims=True)
            c_prev = cst_ref[d]

            log_kw = total - cum_col + i_col
            m_new = jnp.maximum(total + m_prev, jnp.max(log_kw, axis=0, keepdims=True))
            kw = jnp.exp(log_kw - m_new)
            pw = jnp.exp(total + m_prev - m_new)
            cst_ref[d] = pw * c_prev + _bdot_tn(k * kw, v)
            n_new = pw * n_prev + jnp.sum(kw * k, axis=0, keepdims=True)

            log_w = jnp.where(vis, cum_col - cum_row + i_row, -jnp.inf)
            log_p = cum_col + m_prev
            m_t = jnp.maximum(log_p, jnp.max(log_w, axis=1, keepdims=True))
            w = jnp.exp(log_w - m_t)
            p = jnp.exp(log_p - m_t)
            qk = _bdot_nt(q, k) * w
            num = _bdot(qk, v) + p * _bdot(q, c_prev)
            den = jnp.sum(qk, axis=1, keepdims=True) + p * jnp.sum(q * n_prev, axis=1, keepdims=True)
            o_ref[rows, :] += num / jnp.maximum(jnp.abs(den), jnp.exp(-m_t))
            new_carry += [n_new, m_new]
        return tuple(new_carry)

    zero_n = jnp.zeros((1, HEAD_DIM), F32)
    zero_m = jnp.zeros((1, 1), F32)
    lax.fori_loop(0, N_CHUNKS, step, (zero_n, zero_m, zero_n, zero_m))

    o_ref[...] = _head_norm(o_ref[...] * jax.nn.sigmoid(og_ref[...]), gn_ref[...])


def _mlstm(ml, gates, gate_b, conv_w, conv_b, norm_g, n_batch):
    n = ml.shape[0]

    def head_block(offset):
        return pl.BlockSpec((TOK, HEAD_DIM), lambda b, h: (b, offset + h))

    def head_cols(rows, offset):
        return pl.BlockSpec((rows, HEAD_DIM), lambda b, h: (0, offset + h))

    return pl.pallas_call(
        _mlstm_kernel,
        grid=(n_batch, N_HEADS),
        in_specs=[
            head_block(0), head_block(N_HEADS), head_block(2 * N_HEADS), head_block(3 * N_HEADS),
            pl.BlockSpec((TOK, LANES), lambda b, h: (b, 0)),
            pl.BlockSpec((1, LANES), lambda b, h: (0, 0)),
            head_cols(CONV_WIDTH, 0), head_cols(CONV_WIDTH, N_HEADS),
            head_cols(1, 0), head_cols(1, N_HEADS),
            head_cols(1, 0),
        ],
        out_specs=pl.BlockSpec((TOK, HEAD_DIM), lambda b, h: (b, h)),
        out_shape=jax.ShapeDtypeStruct((n, WIDTH), F32),
        scratch_shapes=[
            pltpu.VMEM((TOK, HEAD_DIM), F32),
            pltpu.VMEM((TOK, HEAD_DIM), F32),
            pltpu.VMEM((TOK + 2 * CONV_PAD, HEAD_DIM), F32),
            pltpu.VMEM((TOK, LANES), F32),
            pltpu.VMEM((N_CHUNKS, LANES, CHUNK), F32),
            pltpu.VMEM((2, HEAD_DIM, HEAD_DIM), F32),
        ],
        compiler_params=_params(("parallel", "parallel")),
        name="mlstm",
    )(ml, ml, ml, ml, gates, gate_b, conv_w, conv_w, conv_b, conv_b, norm_g.reshape(1, WIDTH))


def _na_kernel(q_ref, k_ref, v_ref, bias_ref, o_ref):
    step = pl.program_id(2)
    q = q_ref[...].astype(BF16)
    k_ctx = k_ref[0:CTX_LEN, :].astype(BF16)
    v_ctx = v_ref[0:CTX_LEN, :].astype(BF16)
    s_ctx = _dot_nt(q, k_ctx) * QK_SCALE

    @pl.when(step == 0)
    def _():
        m = jnp.max(s_ctx, axis=-1, keepdims=True)
        p = jnp.exp(s_ctx - m)
        o_ref[...] = _dot(p.astype(BF16), v_ctx) / jnp.sum(p, axis=-1, keepdims=True)

    @pl.when(step > 0)
    def _():
        start = CTX_LEN + NA_Q * jnp.clip(step - 2, 0, NA_STEPS - NA_K_ROWS // NA_Q_ROWS)
        rows = pl.ds(pl.multiple_of(start, NA_Q), NA_K)
        s_loc = _dot_nt(q, k_ref[rows, :].astype(BF16)) * QK_SCALE + bias_ref[0, 0]
        m = jnp.maximum(jnp.max(s_loc, axis=-1, keepdims=True), jnp.max(s_ctx, axis=-1, keepdims=True))
        p_loc = jnp.exp(s_loc - m)
        p_ctx = jnp.exp(s_ctx - m)
        denom = jnp.sum(p_loc, axis=-1, keepdims=True) + jnp.sum(p_ctx, axis=-1, keepdims=True)
        o = _dot(p_loc.astype(BF16), v_ref[rows, :].astype(BF16)) + _dot(p_ctx.astype(BF16), v_ctx)
        o_ref[...] = o / denom


def _na_bias_tables(rpb):
    a = np.arange(NA_Q_ROWS)[:, None, None, None]
    qc = np.arange(GRID_W)[None, :, None, None]
    t = np.arange(NA_K_ROWS)[None, None, :, None]
    kc = np.arange(GRID_W)[None, None, None, :]
    col_start = np.clip(qc - NA_WIN_COLS // 2, 0, GRID_W - NA_WIN_COLS)
    col_ok = (kc >= col_start) & (kc < col_start + NA_WIN_COLS)
    col_idx = np.clip(kc - qc + NA_WIN_COLS - 1, 0, 2 * NA_WIN_COLS - 2)
    shape = (NA_Q_ROWS, GRID_W, NA_K_ROWS, GRID_W)
    last_start = NA_K_ROWS - NA_WIN_ROWS
    cases = [
        (t < NA_WIN_ROWS, t - a + NA_WIN_ROWS - 1),
        ((t >= a) & (t < a + NA_WIN_ROWS), t - a + NA_WIN_ROWS // 2 - 1),
        (t >= last_start + 0 * a, t - a + NA_Q_ROWS - NA_K_ROWS + NA_WIN_ROWS - 1),
    ]
    tables = []
    for row_ok, row_idx in cases:
        ok = np.broadcast_to(row_ok & col_ok, shape)
        ridx = np.broadcast_to(np.clip(row_idx, 0, 2 * NA_WIN_ROWS - 2), shape)
        cidx = np.broadcast_to(col_idx, shape)
        bias = rpb[:, ridx, cidx]
        tables.append(jnp.where(ok[None], bias, NEG_INF).reshape(rpb.shape[0], NA_Q, NA_K))
    return jnp.stack(tables, axis=1).astype(F32)


def _neighbourhood_attention(na, bias, n_batch):
    n = na.shape[0]
    steps = 1 + NA_STEPS
    tiles = TOK // NA_Q

    def table(b, h, j):
        return (h, jnp.where(j <= 1, 0, jnp.where(j == NA_STEPS, 2, 1)), 0, 0)

    return pl.pallas_call(
        _na_kernel,
        grid=(n_batch, N_HEADS, steps),
        in_specs=[
            pl.BlockSpec((NA_Q, HEAD_DIM), lambda b, h, j: (b * tiles + j, h)),
            pl.BlockSpec((TOK, HEAD_DIM), lambda b, h, j: (b, N_HEADS + h)),
            pl.BlockSpec((TOK, HEAD_DIM), lambda b, h, j: (b, 2 * N_HEADS + h)),
            pl.BlockSpec((1, 1, NA_Q, NA_K), table),
        ],
        out_specs=pl.BlockSpec((NA_Q, HEAD_DIM), lambda b, h, j: (b * tiles + j, h)),
        out_shape=jax.ShapeDtypeStruct((n, WIDTH), F32),
        compiler_params=_params(("parallel", "parallel", "arbitrary")),
        name="neighbourhood_attention",
    )(na, na, na, bias)


def _merge_kernel(r_ref, m_ref, a_ref, bg_ref, x_ref, mod_ref, wb_ref, wo_ref, g2_ref, wr_hi_ref, wr_lo_ref,
                  x_out_ref, h2_ref, logit_ref):
    gate = jax.nn.sigmoid(bg_ref[...])
    mix = (gate[:, 0:D_MODEL] * _dot(r_ref[...].astype(BF16), wb_ref[0])
           + gate[:, D_MODEL:2 * D_MODEL] * _dot(m_ref[...].astype(BF16), wb_ref[1])
           + gate[:, 2 * D_MODEL:] * _dot(a_ref[...].astype(BF16), wb_ref[2]))
    y = _dot(mix.astype(BF16), wo_ref[...])
    x_new = x_ref[...] + mod_ref[0, 2:3, :] * y
    x_out_ref[...] = x_new
    h2 = _rms_modulate(x_new, g2_ref[...], mod_ref[0, 3:4, :], mod_ref[0, 4:5, :])
    h2_ref[...] = h2
    hi = h2.astype(BF16)
    lo = (h2 - hi.astype(F32)).astype(BF16)
    logit_ref[...] = _dot(hi, wr_hi_ref[...]) + (_dot(lo, wr_hi_ref[...]) + _dot(hi, wr_lo_ref[...]))


def _merge(r, m, a, bg, x_all, mod, w_branch, w_out, norm2_g, wr_hi, wr_lo, n_batch):
    n = x_all.shape[0]
    row = lambda i: (i, 0)
    const = lambda i: (0, 0)
    return pl.pallas_call(
        _merge_kernel,
        grid=(n // TM,),
        in_specs=[
            pl.BlockSpec((TM, WIDTH), row), pl.BlockSpec((TM, WIDTH), row), pl.BlockSpec((TM, WIDTH), row),
            pl.BlockSpec((TM, 3 * D_MODEL), row),
            pl.BlockSpec((TM, D_MODEL), row),
            pl.BlockSpec((1, 6, D_MODEL), _mod_row(n_batch)),
            pl.BlockSpec(w_branch.shape, lambda i: (0, 0, 0)),
            pl.BlockSpec(w_out.shape, const),
            pl.BlockSpec((1, D_MODEL), const),
            pl.BlockSpec(wr_hi.shape, const),
            pl.BlockSpec(wr_lo.shape, const),
        ],
        out_specs=[pl.BlockSpec((TM, D_MODEL), row), pl.BlockSpec((TM, D_MODEL), row), pl.BlockSpec((TM, LANES), row)],
        out_shape=[jax.ShapeDtypeStruct((n, D_MODEL), F32), jax.ShapeDtypeStruct((n, D_MODEL), F32),
                   jax.ShapeDtypeStruct((n, LANES), F32)],
        compiler_params=_params(("parallel",), VMEM_LIMIT),
        name="merge",
    )(r, m, a, bg, x_all, mod, w_branch, w_out, norm2_g.reshape(1, D_MODEL), wr_hi, wr_lo)


ROUTE_E1, ROUTE_E2, ROUTE_RANK1, ROUTE_RANK2, ROUTE_W1, ROUTE_W2 = range(6)


def _route_kernel(logit_ref, route_ref, count_ref, cnt_ref):
    @pl.when(pl.program_id(0) == 0)
    def _():
        cnt_ref[...] = jnp.zeros_like(cnt_ref)

    lg = logit_ref[...]
    lane = lax.broadcasted_iota(jnp.int32, lg.shape, 1)
    lane_f = lane.astype(F32)

    def first_argmax(vals):
        top = jnp.max(vals, axis=-1, keepdims=True)
        idx = jnp.min(jnp.where(vals == top, lane_f, float(LANES)), axis=-1, keepdims=True)
        return top, idx

    group_logits = jnp.where(lane < N_GROUPS, lg, -jnp.inf)
    g_top, g_idx = first_argmax(group_logits)
    group_w = 1.0 / jnp.sum(jnp.exp(group_logits - g_top), axis=-1, keepdims=True)

    first = N_GROUPS + EXPERTS_PER_GROUP * g_idx
    in_group = (lane_f >= first) & (lane_f < first + EXPERTS_PER_GROUP)
    expert_logits = jnp.where(in_group, lg, -jnp.inf)
    v1, i1 = first_argmax(expert_logits)
    v2, i2 = first_argmax(jnp.where(lane_f == i1, -jnp.inf, expert_logits))
    t = jnp.exp(v2 - v1)
    w1 = group_w / (1.0 + t)
    w2 = group_w * t / (1.0 + t)

    oh1 = (lane_f == i1).astype(F32)
    oh2 = (lane_f == i2).astype(F32)
    both = oh1 + oh2
    rows = lg.shape[0]
    earlier = (lax.broadcasted_iota(jnp.int32, (rows, rows), 1)
               < lax.broadcasted_iota(jnp.int32, (rows, rows), 0)).astype(BF16)
    before = _dot(earlier, both.astype(BF16)) + cnt_ref[...]
    rank1 = jnp.sum(oh1 * before, axis=-1, keepdims=True)
    rank2 = jnp.sum(oh2 * before, axis=-1, keepdims=True)
    cnt_ref[...] += jnp.sum(both, axis=0, keepdims=True)
    count_ref[...] = jnp.broadcast_to(cnt_ref[...], count_ref.shape)

    out = jnp.zeros_like(lg)
    for slot, val in ((ROUTE_E1, i1 - N_GROUPS), (ROUTE_E2, i2 - N_GROUPS), (ROUTE_RANK1, rank1),
                      (ROUTE_RANK2, rank2), (ROUTE_W1, w1), (ROUTE_W2, w2)):
        out = jnp.where(lane == slot, val, out)
    route_ref[...] = out


def _route(logits):
    n = logits.shape[0]
    return pl.pallas_call(
        _route_kernel,
        grid=(n // ROUTE_TM,),
        in_specs=[pl.BlockSpec((ROUTE_TM, LANES), lambda i: (i, 0))],
        out_specs=[pl.BlockSpec((ROUTE_TM, LANES), lambda i: (i, 0)), pl.BlockSpec((8, LANES), lambda i: (0, 0))],
        out_shape=[jax.ShapeDtypeStruct((n, LANES), F32), jax.ShapeDtypeStruct((8, LANES), F32)],
        scratch_shapes=[pltpu.VMEM((1, LANES), F32)],
        compiler_params=_params(("arbitrary",)),
        name="route",
    )(logits)


def _dispatch_plan(route, counts):
    n = route.shape[0]
    counts = counts[0, N_GROUPS:N_GROUPS + N_EXPERTS].astype(jnp.int32)
    padded = (counts + EXPERT_BLOCK - 1) // EXPERT_BLOCK * EXPERT_BLOCK
    pad_end = jnp.cumsum(padded)
    pad_start = pad_end - padded
    e = route[:, ROUTE_E1:ROUTE_E2 + 1].astype(jnp.int32)
    rank = route[:, ROUTE_RANK1:ROUTE_RANK2 + 1].astype(jnp.int32)
    dest = pad_start[e] + rank
    n_blocks = (2 * n + N_EXPERTS * (EXPERT_BLOCK - 1) + EXPERT_BLOCK - 1) // EXPERT_BLOCK
    block_expert = jnp.minimum(
        jnp.sum(jnp.arange(n_blocks)[:, None] * EXPERT_BLOCK >= pad_end[None, :], axis=1), N_EXPERTS - 1)
    used_blocks = (pad_end[-1] // EXPERT_BLOCK).reshape(1)
    dest_tiles = dest.reshape(n // TM, TM, 2).transpose(0, 2, 1).reshape(n // TM, 1, 2 * TM)
    return dest_tiles, block_expert.astype(jnp.int32), used_blocks.astype(jnp.int32), n_blocks


def _row_copy(src_ref, src_row, dst_ref, dst_row, sem):
    return pltpu.make_async_copy(src_ref.at[pl.ds(src_row, 1)], dst_ref.at[pl.ds(dst_row, 1)], sem)


def _dispatch_kernel(dest_ref, h_ref, xs_in_ref, xs_ref, sem):
    del xs_in_ref

    def start(t, carry):
        _row_copy(h_ref, t, xs_ref, dest_ref[0, 0, t], sem).start()
        _row_copy(h_ref, t, xs_ref, dest_ref[0, 0, TM + t], sem).start()
        return carry

    lax.fori_loop(0, TM, start, 0)

    def wait(t, carry):
        _row_copy(h_ref, 0, xs_ref, 0, sem).wait()
        return carry

    lax.fori_loop(0, 2 * TM, wait, 0)


def _dispatch(h2, dest_tiles, n_rows):
    n = h2.shape[0]
    xs0 = jnp.zeros((n_rows, D_MODEL), F32)
    return pl.pallas_call(
        _dispatch_kernel,
        grid=(n // TM,),
        in_specs=[
            pl.BlockSpec((1, 1, 2 * TM), lambda i: (i, 0, 0), memory_space=pltpu.SMEM),
            pl.BlockSpec((TM, D_MODEL), lambda i: (i, 0)),
            pl.BlockSpec(memory_space=pl.ANY),
        ],
        out_specs=pl.BlockSpec(memory_space=pl.ANY),
        out_shape=jax.ShapeDtypeStruct((n_rows, D_MODEL), F32),
        scratch_shapes=[pltpu.SemaphoreType.DMA(())],
        input_output_aliases={2: 0},
        compiler_params=_params(("arbitrary",)),
        name="dispatch",
    )(dest_tiles, h2, xs0)


def _expert_kernel(be_ref, used_ref, x_ref, wg_ref, wu_ref, wd_ref, y_ref, wg_s, wu_s, wd_s):
    i = pl.program_id(0)
    new_expert = jnp.logical_or(i == 0, be_ref[i] != be_ref[jnp.maximum(i - 1, 0)])

    @pl.when(new_expert)
    def _():
        wg_s[...] = wg_ref[0].astype(BF16)
        wu_s[...] = wu_ref[0].astype(BF16)
        wd_s[...] = wd_ref[0].astype(BF16)

    @pl.when(i < used_ref[0])
    def _():
        x = x_ref[...].astype(BF16)
        hidden = _silu(_dot(x, wg_s[...])) * _dot(x, wu_s[...])
        y_ref[...] = _dot(hidden.astype(BF16), wd_s[...])

    @pl.when(i >= used_ref[0])
    def _():
        y_ref[...] = jnp.zeros_like(y_ref)


def _experts(xs, block_expert, used_blocks, w_gate, w_up, w_down):
    n_rows = xs.shape[0]
    n_blocks = n_rows // EXPERT_BLOCK
    grid_spec = pltpu.PrefetchScalarGridSpec(
        num_scalar_prefetch=2,
        grid=(n_blocks,),
        in_specs=[
            pl.BlockSpec((EXPERT_BLOCK, D_MODEL), lambda i, be, used: (i, 0)),
            pl.BlockSpec((1, D_MODEL, EXPERT_FF), lambda i, be, used: (be[i], 0, 0)),
            pl.BlockSpec((1, D_MODEL, EXPERT_FF), lambda i, be, used: (be[i], 0, 0)),
            pl.BlockSpec((1, EXPERT_FF, D_MODEL), lambda i, be, used: (be[i], 0, 0)),
        ],
        out_specs=pl.BlockSpec((EXPERT_BLOCK, D_MODEL), lambda i, be, used: (i, 0)),
        scratch_shapes=[
            pltpu.VMEM((D_MODEL, EXPERT_FF), BF16),
            pltpu.VMEM((D_MODEL, EXPERT_FF), BF16),
            pltpu.VMEM((EXPERT_FF, D_MODEL), BF16),
        ],
    )
    return pl.pallas_call(
        _expert_kernel,
        grid_spec=grid_spec,
        out_shape=jax.ShapeDtypeStruct((n_rows, D_MODEL), F32),
        compiler_params=_params(("arbitrary",), VMEM_LIMIT),
        name="experts",
    )(block_expert, used_blocks, xs, w_gate, w_up, w_down)


def _combine_kernel(dest_ref, ys_ref, x_ref, route_ref, mod_ref, fg_ref, o_ref, y1_ref, y2_ref, sem, *, final):
    def start(t, carry):
        _row_copy(ys_ref, dest_ref[0, 0, t], y1_ref, t, sem).start()
        _row_copy(ys_ref, dest_ref[0, 0, TM + t], y2_ref, t, sem).start()
        return carry

    lax.fori_loop(0, TM, start, 0)

    def wait(t, carry):
        _row_copy(ys_ref, 0, y1_ref, 0, sem).wait()
        return carry

    lax.fori_loop(0, 2 * TM, wait, 0)

    route = route_ref[...]
    w1 = route[:, ROUTE_W1:ROUTE_W1 + 1]
    w2 = route[:, ROUTE_W2:ROUTE_W2 + 1]
    x_new = x_ref[...] + mod_ref[0, 5:6, :] * (y1_ref[...] * w1 + y2_ref[...] * w2)
    if final:
        x_new = x_new * lax.rsqrt(jnp.mean(x_new * x_new, axis=-1, keepdims=True) + EPS) * fg_ref[...]
    o_ref[...] = x_new


def _combine(ys, dest_tiles, x_all, route, mod, final_g, n_batch, final):
    n = x_all.shape[0]
    row = lambda i: (i, 0)
    return pl.pallas_call(
        functools.partial(_combine_kernel, final=final),
        grid=(n // TM,),
        in_specs=[
            pl.BlockSpec((1, 1, 2 * TM), lambda i: (i, 0, 0), memory_space=pltpu.SMEM),
            pl.BlockSpec(memory_space=pl.ANY),
            pl.BlockSpec((TM, D_MODEL), row),
            pl.BlockSpec((TM, LANES), row),
            pl.BlockSpec((1, 6, D_MODEL), _mod_row(n_batch)),
            pl.BlockSpec((1, D_MODEL), lambda i: (0, 0)),
        ],
        out_specs=pl.BlockSpec((TM, D_MODEL), row),
        out_shape=jax.ShapeDtypeStruct((n, D_MODEL), F32),
        scratch_shapes=[
            pltpu.VMEM((TM, D_MODEL), F32),
            pltpu.VMEM((TM, D_MODEL), F32),
            pltpu.SemaphoreType.DMA(()),
        ],
        compiler_params=_params(("arbitrary",)),
        name="combine",
    )(dest_tiles, ys, x_all, route, mod, final_g.reshape(1, D_MODEL))


def _rope_tables():
    t = jnp.arange(SEQ)
    rows = (t // GRID_W).astype(F32)
    cols = (t % GRID_W).astype(F32)
    n_freq = HEAD_DIM // 4
    inv_freq = ROPE_BASE ** (-jnp.arange(n_freq, dtype=F32) / n_freq)
    ang_r = rows[:, None] * inv_freq
    ang_c = cols[:, None] * inv_freq
    cos = jnp.concatenate([jnp.cos(ang_r)] * 2 + [jnp.cos(ang_c)] * 2, axis=1)
    sin = jnp.concatenate([-jnp.sin(ang_r), jnp.sin(ang_r), -jnp.sin(ang_c), jnp.sin(ang_c)], axis=1)
    return cos, sin


def kernel(x, c, ctx, c_ctx, w_mod, b_mod, norm1_g, norm2_g, w_in, ret_decay, ret_norm_g, conv_w, conv_b,
           mlstm_gate_b, mlstm_norm_g, na_rpb, w_branch, w_out, w_group, w_router, w_expert_gate,
           w_expert_up, w_expert_down, final_norm_g):
    n_batch, seq, d = x.shape
    depth = w_mod.shape[0]
    assert (seq, d, ctx.shape[1]) == (SEQ, D_MODEL, CTX_LEN)
    n = n_batch * TOK

    cond_rows = -(-(n_batch + 1) // 8) * 8
    cond = jnp.zeros((cond_rows, d), F32).at[:n_batch].set(c).at[n_batch].set(c_ctx)
    mod_all = _modulation(cond, w_mod, b_mod).reshape(depth, cond_rows, 6, d)

    cos, sin = _rope_tables()
    x_all = jnp.concatenate([ctx, x], axis=1).reshape(n, d)

    g0 = 8 * WIDTH
    g1 = g0 + 4 * N_HEADS
    n1 = g1 + 3 * WIDTH

    for layer in range(depth):
        w = w_in[layer].astype(BF16)
        w_rm, w_na, w_bg = w[:, :g0], w[:, g1:n1], w[:, n1:]
        w_gate = jnp.pad(w[:, g0:g1], ((0, 0), (0, LANES - 4 * N_HEADS)))
        gate_b = jnp.pad(mlstm_gate_b[layer].reshape(1, 4 * N_HEADS), ((0, 0), (0, LANES - 4 * N_HEADS)))
        w_route = jnp.concatenate([w_group[layer], w_router[layer]], axis=1)
        w_route = jnp.pad(w_route, ((0, 0), (0, LANES - w_route.shape[1])))
        wr_hi = w_route.astype(BF16)
        wr_lo = (w_route - wr_hi.astype(F32)).astype(BF16)
        mod = mod_all[layer]

        ret, ml, gates, na, bg = _in_projection(x_all, mod, norm1_g[layer], w_rm, w_gate, w_na, w_bg, n_batch)
        r_out = _retention(ret, ret_decay[layer], ret_norm_g[layer], cos, sin, n_batch)
        m_out = _mlstm(ml, gates, gate_b, conv_w[layer], conv_b[layer].reshape(1, 2 * WIDTH),
                       mlstm_norm_g[layer], n_batch)
        a_out = _neighbourhood_attention(na, _na_bias_tables(na_rpb[layer]), n_batch)
        x_all, h2, logits = _merge(r_out, m_out, a_out, bg, x_all, mod, w_branch[layer].astype(BF16),
                                   w_out[layer].astype(BF16), norm2_g[layer], wr_hi, wr_lo, n_batch)
        route, counts = _route(logits)
        dest_tiles, block_expert, used_blocks, n_blocks = _dispatch_plan(route, counts)
        xs = _dispatch(h2, dest_tiles, n_blocks * EXPERT_BLOCK)
        ys = _experts(xs, block_expert, used_blocks, w_expert_gate[layer], w_expert_up[layer], w_expert_down[layer])
        x_all = _combine(ys, dest_tiles, x_all, route, mod, final_norm_g, n_batch, final=layer == depth - 1)

    return x_all.reshape(n_batch, TOK, d)[:, CTX_LEN:]
```

```python
import functools

import numpy as np
import jax
import jax.numpy as jnp
from jax import lax
from jax.experimental import pallas as pl
from jax.experimental.pallas import tpu as pltpu

F32 = jnp.float32
BF16 = jnp.bfloat16

D_MODEL = 1024
SEQ = 2048
CTX_LEN = 256
TOK = CTX_LEN + SEQ
GRID_W = 64
GRID_ROWS = SEQ // GRID_W
HEAD_DIM = 128
N_HEADS = 4
WIDTH = N_HEADS * HEAD_DIM
CHUNK = 128
N_CHUNKS = TOK // CHUNK
CTX_CHUNKS = CTX_LEN // CHUNK
CONV_WIDTH = 5
NA_WIN_ROWS = 8
NA_WIN_COLS = 16
NA_Q_ROWS = 4
NA_Q = NA_Q_ROWS * GRID_W
NA_K_ROWS = NA_Q_ROWS + NA_WIN_ROWS
NA_K = NA_K_ROWS * GRID_W
NA_STEPS = SEQ // NA_Q
ROPE_BASE = 10000.0
N_GROUPS = 4
EXPERTS_PER_GROUP = 8
N_EXPERTS = N_GROUPS * EXPERTS_PER_GROUP
EXPERT_FF = 512
EXPERT_BLOCK = 256
EPS = 1e-6
NEG_INF = -1e30
QK_SCALE = HEAD_DIM ** -0.5

TM = 256
ROUTE_TM = 512
LANES = 128
VMEM_LIMIT = 56 * 1024 * 1024


def _dot(a, b):
    return jnp.dot(a, b, preferred_element_type=F32)


def _dot_nt(a, b):
    return lax.dot_general(a, b, (((1,), (1,)), ((), ())), preferred_element_type=F32)


def _bdot(a, b):
    return _dot(a.astype(BF16), b.astype(BF16))


def _bdot_nt(a, b):
    return _dot_nt(a.astype(BF16), b.astype(BF16))


def _bdot_tn(a, b):
    return _dot(a.T.astype(BF16), b.astype(BF16))


def _log_sigmoid(x):
    return jnp.minimum(x, 0.0) - jnp.log1p(jnp.exp(-jnp.abs(x)))


def _silu(x):
    return x * jax.nn.sigmoid(x)


def _params(sem, vmem=None):
    return pltpu.CompilerParams(dimension_semantics=sem, vmem_limit_bytes=vmem)


def _mod_row(n_batch):
    tiles = TOK // TM

    def index(i):
        return (jnp.where(i % tiles == 0, n_batch, i // tiles), 0, 0)

    return index


def _mod_kernel(c_ref, w_ref, b_ref, o_ref):
    cond = _silu(c_ref[...])
    o_ref[0] = _bdot(cond, w_ref[0]) + b_ref[0]


def _modulation(cond, w_mod, b_mod):
    depth, d, cols = w_mod.shape
    rows = cond.shape[0]
    tn = 1536
    return pl.pallas_call(
        _mod_kernel,
        grid=(depth, cols // tn),
        in_specs=[
            pl.BlockSpec((rows, d), lambda l, j: (0, 0)),
            pl.BlockSpec((1, d, tn), lambda l, j: (l, 0, j)),
            pl.BlockSpec((1, 1, tn), lambda l, j: (l, 0, j)),
        ],
        out_specs=pl.BlockSpec((1, rows, tn), lambda l, j: (l, 0, j)),
        out_shape=jax.ShapeDtypeStruct((depth, rows, cols), F32),
        compiler_params=_params(("parallel", "parallel")),
        name="modulation",
    )(cond, w_mod, b_mod.reshape(depth, 1, cols))


def _rms_modulate(x, g, shift, scale):
    y = x * lax.rsqrt(jnp.mean(x * x, axis=-1, keepdims=True) + EPS) * g
    return y * (1.0 + scale) + shift


IN_WIDTHS = (4 * WIDTH, 4 * WIDTH, LANES, 3 * WIDTH, 3 * D_MODEL)
IN_OFFSETS = tuple(int(v) for v in np.cumsum((0,) + IN_WIDTHS))


def _pack_in_weights(w_in):
    g1 = 2 * 4 * WIDTH + 4 * N_HEADS
    lane_pad = jnp.zeros(w_in.shape[:-1] + (LANES - 4 * N_HEADS,), w_in.dtype)
    return jnp.concatenate([w_in[..., :g1], lane_pad, w_in[..., g1:]], axis=-1).astype(BF16)


def _inproj_kernel(x_ref, mod_ref, g_ref, w_ref, *out_refs):
    h = _rms_modulate(x_ref[...], g_ref[0], mod_ref[0, 0:1, :], mod_ref[0, 1:2, :]).astype(BF16)
    for o_ref, lo, hi in zip(out_refs, IN_OFFSETS[:-1], IN_OFFSETS[1:]):
        o_ref[...] = _dot(h, w_ref[0, :, lo:hi])


def _in_projection(x_all, mod, norm_g, w_packed, layer, n_batch):
    n = x_all.shape[0]
    row = lambda i: (i, 0)
    return pl.pallas_call(
        _inproj_kernel,
        grid=(n // TM,),
        in_specs=[
            pl.BlockSpec((TM, D_MODEL), row),
            pl.BlockSpec((1, 6, D_MODEL), _mod_row(n_batch)),
            pl.BlockSpec((1, 1, D_MODEL), lambda i: (layer, 0, 0)),
            pl.BlockSpec((1,) + w_packed.shape[1:], lambda i: (layer, 0, 0), pipeline_mode=pl.Buffered(1)),
        ],
        out_specs=[pl.BlockSpec((TM, w), row) for w in IN_WIDTHS],
        out_shape=[jax.ShapeDtypeStruct((n, w), F32) for w in IN_WIDTHS],
        compiler_params=_params(("parallel",), VMEM_LIMIT),
        name="in_projection",
    )(x_all, mod, norm_g.reshape(-1, 1, D_MODEL), w_packed)


def _chunk_order(t):
    fwd = t
    bwd = jnp.where(t < CTX_CHUNKS, CTX_CHUNKS - 1 - t, N_CHUNKS + CTX_CHUNKS - 1 - t)
    return fwd, bwd


def _chunk_slice(c):
    return pl.ds(pl.multiple_of(c * CHUNK, CHUNK), CHUNK)


def _head_norm(y, gain):
    mu = jnp.mean(y, axis=-1, keepdims=True)
    yc = y - mu
    var = jnp.mean(yc * yc, axis=-1, keepdims=True)
    return yc * lax.rsqrt(var + EPS) * gain


def _select_lane(x, lane, idx):
    return jnp.sum(jnp.where(lane == idx, x, 0.0), axis=-1, keepdims=True)


def _retention_kernel(dec_ref, q_ref, k_ref, v_ref, g_ref, cos_ref, sin_ref, gn_ref, o_ref,
                      qs_ref, ks_ref, st_ref):
    head = pl.program_id(1)

    lane = lax.broadcasted_iota(jnp.int32, (SEQ, HEAD_DIM), 1)
    first_half = (lane % (HEAD_DIM // 2)) < (HEAD_DIM // 4)
    cos = cos_ref[...]
    sin = sin_ref[...]

    def rope(x):
        rot = jnp.where(first_half, pltpu.roll(x, HEAD_DIM - HEAD_DIM // 4, 1), pltpu.roll(x, HEAD_DIM // 4, 1))
        return x * cos + rot * sin

    qs_ref[0:CTX_LEN, :] = q_ref[0:CTX_LEN, :]
    ks_ref[0:CTX_LEN, :] = k_ref[0:CTX_LEN, :] * QK_SCALE
    qs_ref[CTX_LEN:, :] = rope(q_ref[CTX_LEN:, :])
    ks_ref[CTX_LEN:, :] = rope(k_ref[CTX_LEN:, :]) * QK_SCALE

    dec = _log_sigmoid(dec_ref[...])
    hl = lax.broadcasted_iota(jnp.int32, dec.shape, 1)
    lg = jnp.sum(jnp.where(hl == head, dec, 0.0), axis=-1, keepdims=True)
    lg_f, lg_b = lg[0:1, :], lg[1:2, :]

    ii = lax.broadcasted_iota(jnp.int32, (CHUNK, CHUNK), 0).astype(F32)
    jj = lax.broadcasted_iota(jnp.int32, (CHUNK, CHUNK), 1).astype(F32)
    col = lax.broadcasted_iota(jnp.int32, (CHUNK, 1), 0).astype(F32)

    def decay_mat(dist, lg_dir):
        ok = dist >= 0
        return jnp.where(ok, jnp.exp(jnp.where(ok, dist, 0.0) * lg_dir), 0.0)

    intra = (decay_mat(ii - jj, lg_f), decay_mat(jj - ii, lg_b))
    q_decay = (jnp.exp((col + 1.0) * lg_f), jnp.exp((CHUNK - col) * lg_b))
    k_decay = (jnp.exp((CHUNK - 1.0 - col) * lg_f), jnp.exp(col * lg_b))
    chunk_decay = (jnp.exp(CHUNK * lg_f), jnp.exp(CHUNK * lg_b))

    st_ref[...] = jnp.zeros_like(st_ref)
    o_ref[...] = jnp.zeros_like(o_ref)

    def step(t, carry):
        for d, c in enumerate(_chunk_order(t)):
            rows = _chunk_slice(c)
            q = qs_ref[rows, :]
            k = ks_ref[rows, :]
            v = v_ref[rows, :]
            s_prev = st_ref[d]
            scores = _bdot_nt(q, k) * intra[d]
            o_ref[rows, :] += _bdot(scores, v) + _bdot(q * q_decay[d], s_prev)
            st_ref[d] = s_prev * chunk_decay[d] + _bdot_tn(k * k_decay[d], v)
        return carry

    lax.fori_loop(0, N_CHUNKS, step, 0)

    o_ref[...] = _head_norm(o_ref[...], gn_ref[...]) * _silu(g_ref[...])


def _retention(ret, ret_decay, norm_g, cos, sin, n_batch):
    n = ret.shape[0]

    def head_block(offset):
        return pl.BlockSpec((TOK, HEAD_DIM), lambda b, h: (b, offset + h))

    return pl.pallas_call(
        _retention_kernel,
        grid=(n_batch, N_HEADS),
        in_specs=[
            pl.BlockSpec(ret_decay.shape, lambda b, h: (0, 0)),
            head_block(0), head_block(N_HEADS), head_block(2 * N_HEADS), head_block(3 * N_HEADS),
            pl.BlockSpec((SEQ, HEAD_DIM), lambda b, h: (0, 0)),
            pl.BlockSpec((SEQ, HEAD_DIM), lambda b, h: (0, 0)),
            pl.BlockSpec((1, HEAD_DIM), lambda b, h: (0, h)),
        ],
        out_specs=pl.BlockSpec((TOK, HEAD_DIM), lambda b, h: (b, h)),
        out_shape=jax.ShapeDtypeStruct((n, WIDTH), F32),
        scratch_shapes=[
            pltpu.VMEM((TOK, HEAD_DIM), F32),
            pltpu.VMEM((TOK, HEAD_DIM), F32),
            pltpu.VMEM((2, HEAD_DIM, HEAD_DIM), F32),
        ],
        compiler_params=_params(("parallel", "parallel")),
        name="retention",
    )(ret_decay, ret, ret, ret, ret, cos, sin, norm_g.reshape(1, WIDTH))


CONV_PAD = 8


def _mlstm_kernel(q_ref, k_ref, v_ref, og_ref, gate_ref, gb_ref, cwq_ref, cwk_ref, cbq_ref, cbk_ref,
                  gn_ref, o_ref, qs_ref, ks_ref, pad_ref, gx_ref, gxt_ref, cst_ref):
    head = pl.program_id(1)

    trow = lax.broadcasted_iota(jnp.int32, (TOK, 1), 0)
    pad_ref[0:CONV_PAD, :] = jnp.zeros((CONV_PAD, HEAD_DIM), F32)
    pad_ref[CONV_PAD + TOK:, :] = jnp.zeros((CONV_PAD, HEAD_DIM), F32)

    def conv(u_ref, w_ref, b_ref):
        pad_ref[CONV_PAD:CONV_PAD + TOK, :] = u_ref[...]
        acc = jnp.zeros((TOK, HEAD_DIM), F32)
        for j in range(CONV_WIDTH):
            shift = j - CONV_WIDTH // 2
            tap = pad_ref[CONV_PAD + shift:CONV_PAD + shift + TOK, :]
            same_segment = (trow < CTX_LEN) == (trow + shift < CTX_LEN)
            acc = acc + jnp.where(same_segment, tap, 0.0) * w_ref[j:j + 1, :]
        return _silu(acc + b_ref[...])

    qs_ref[...] = conv(q_ref, cwq_ref, cbq_ref)
    ks_ref[...] = conv(k_ref, cwk_ref, cbk_ref) * QK_SCALE

    lane = lax.broadcasted_iota(jnp.int32, (TOK, LANES), 1)
    g = gate_ref[...] + gb_ref[...]
    is_forget = ((lane // N_HEADS) % 2) == 1
    gx_ref[...] = jnp.where(is_forget, _log_sigmoid(g), g)
    for c in range(N_CHUNKS):
        gxt_ref[c] = gx_ref[c * CHUNK:(c + 1) * CHUNK, :].T

    ii = lax.broadcasted_iota(jnp.int32, (CHUNK, CHUNK), 0)
    jj = lax.broadcasted_iota(jnp.int32, (CHUNK, CHUNK), 1)
    causal = (jj <= ii, jj >= ii)
    clane = lax.broadcasted_iota(jnp.int32, (CHUNK, LANES), 1)

    cst_ref[...] = jnp.zeros_like(cst_ref)
    o_ref[...] = jnp.zeros_like(o_ref)

    def step(t, carry):
        new_carry = []
        for d, c in enumerate(_chunk_order(t)):
            n_prev, m_prev = carry[2 * d], carry[2 * d + 1]
            rows = _chunk_slice(c)
            q = qs_ref[rows, :]
            k = ks_ref[rows, :]
            v = v_ref[rows, :]
            gc = gx_ref[rows, :]
            i_col = _select_lane(gc, clane, 2 * d * N_HEADS + head)
            f_col = _select_lane(gc, clane, (2 * d + 1) * N_HEADS + head)
            i_row = gxt_ref[c, pl.ds(2 * d * N_HEADS + head, 1), :]
            f_row = gxt_ref[c, pl.ds((2 * d + 1) * N_HEADS + head, 1), :]
            vis = causal[d]
            cum_col = jnp.sum(jnp.where(vis, f_row, 0.0), axis=1, keepdims=True)
            cum_row = jnp.sum(jnp.where(causal[1 - d], f_col, 0.0), axis=0, keepdims=True)
            total = jnp.sum(f_row, axis=1, keepdims=True)
            c_prev = cst_ref[d]

            log_kw = total - cum_col + i_col
            m_new = jnp.maximum(total + m_prev, jnp.max(log_kw, axis=0, keepdims=True))
            kw = jnp.exp(log_kw - m_new)
            pw = jnp.exp(total + m_prev - m_new)
            cst_ref[d] = pw * c_prev + _bdot_tn(k * kw, v)
            n_new = pw * n_prev + jnp.sum(kw * k, axis=0, keepdims=True)

            log_w = jnp.where(vis, cum_col - cum_row + i_row, -jnp.inf)
            log_p = cum_col + m_prev
            m_t = jnp.maximum(log_p, jnp.max(log_w, axis=1, keepdims=True))
            w = jnp.exp(log_w - m_t)
            p = jnp.exp(log_p - m_t)
            qk = _bdot_nt(q, k) * w
            num = _bdot(qk, v) + p * _bdot(q, c_prev)
            den = jnp.sum(qk, axis=1, keepdims=True) + p * jnp.sum(q * n_prev, axis=1, keepdims=True)
            o_ref[rows, :] += num / jnp.maximum(jnp.abs(den), jnp.exp(-m_t))
            new_carry += [n_new, m_new]
        return tuple(new_carry)

    zero_n = jnp.zeros((1, HEAD_DIM), F32)
    zero_m = jnp.zeros((1, 1), F32)
    lax.fori_loop(0, N_CHUNKS, step, (zero_n, zero_m, zero_n, zero_m))

    o_ref[...] = _head_norm(o_ref[...] * jax.nn.sigmoid(og_ref[...]), gn_ref[...])


def _mlstm(ml, gates, gate_b, conv_w, conv_b, norm_g, n_batch):
    n = ml.shape[0]

    def head_block(offset):
        return pl.BlockSpec((TOK, HEAD_DIM), lambda b, h: (b, offset + h))

    def head_cols(rows, offset):
        return pl.BlockSpec((rows, HEAD_DIM), lambda b, h: (0, offset + h))

    return pl.pallas_call(
        _mlstm_kernel,
        grid=(n_batch, N_HEADS),
        in_specs=[
            head_block(0), head_block(N_HEADS), head_block(2 * N_HEADS), head_block(3 * N_HEADS),
            pl.BlockSpec((TOK, LANES), lambda b, h: (b, 0)),
            pl.BlockSpec((1, LANES), lambda b, h: (0, 0)),
            head_cols(CONV_WIDTH, 0), head_cols(CONV_WIDTH, N_HEADS),
            head_cols(1, 0), head_cols(1, N_HEADS),
            head_cols(1, 0),
        ],
        out_specs=pl.BlockSpec((TOK, HEAD_DIM), lambda b, h: (b, h)),
        out_shape=jax.ShapeDtypeStruct((n, WIDTH), F32),
        scratch_shapes=[
            pltpu.VMEM((TOK, HEAD_DIM), F32),
            pltpu.VMEM((TOK, HEAD_DIM), F32),
            pltpu.VMEM((TOK + 2 * CONV_PAD, HEAD_DIM), F32),
            pltpu.VMEM((TOK, LANES), F32),
            pltpu.VMEM((N_CHUNKS, LANES, CHUNK), F32),
            pltpu.VMEM((2, HEAD_DIM, HEAD_DIM), F32),
        ],
        compiler_params=_params(("parallel", "parallel")),
        name="mlstm",
    )(ml, ml, ml, ml, gates, gate_b, conv_w, conv_w, conv_b, conv_b, norm_g.reshape(1, WIDTH))


def _na_kernel(q_ref, k_ref, v_ref, bias_ref, o_ref):
    step = pl.program_id(2)
    q = q_ref[...].astype(BF16)
    k_ctx = k_ref[0:CTX_LEN, :].astype(BF16)
    v_ctx = v_ref[0:CTX_LEN, :].astype(BF16)
    s_ctx = _dot_nt(q, k_ctx) * QK_SCALE

    @pl.when(step == 0)
    def _():
        m = jnp.max(s_ctx, axis=-1, keepdims=True)
        p = jnp.exp(s_ctx - m)
        o_ref[...] = _dot(p.astype(BF16), v_ctx) / jnp.sum(p, axis=-1, keepdims=True)

    @pl.when(step > 0)
    def _():
        start = CTX_LEN + NA_Q * jnp.clip(step - 2, 0, NA_STEPS - NA_K_ROWS // NA_Q_ROWS)
        rows = pl.ds(pl.multiple_of(start, NA_Q), NA_K)
        s_loc = _dot_nt(q, k_ref[rows, :].astype(BF16)) * QK_SCALE + bias_ref[0, 0, 0]
        m = jnp.maximum(jnp.max(s_loc, axis=-1, keepdims=True), jnp.max(s_ctx, axis=-1, keepdims=True))
        p_loc = jnp.exp(s_loc - m)
        p_ctx = jnp.exp(s_ctx - m)
        denom = jnp.sum(p_loc, axis=-1, keepdims=True) + jnp.sum(p_ctx, axis=-1, keepdims=True)
        o = _dot(p_loc.astype(BF16), v_ref[rows, :].astype(BF16)) + _dot(p_ctx.astype(BF16), v_ctx)
        o_ref[...] = o / denom


def _na_bias_tables(rpb):
    lead = rpb.shape[:-2]
    n_row_off, n_col_off = 2 * NA_WIN_ROWS - 1, 2 * NA_WIN_COLS - 1
    qc = np.arange(GRID_W)[:, None]
    kc = np.arange(GRID_W)[None, :]
    col_start = np.clip(qc - NA_WIN_COLS // 2, 0, GRID_W - NA_WIN_COLS)
    col_ok = (kc >= col_start) & (kc < col_start + NA_WIN_COLS)
    col_idx = np.clip(kc - qc + NA_WIN_COLS - 1, 0, n_col_off - 1)
    onehot = ((col_idx[None] == np.arange(n_col_off)[:, None, None]) & col_ok[None]).astype(np.float32)
    slabs = jnp.einsum('...rc,cqk->...rqk', rpb.astype(F32), onehot, precision=lax.Precision.HIGHEST)
    slabs = jnp.where(col_ok, slabs, NEG_INF)
    pad = [(0, 0)] * len(lead) + [(NA_K_ROWS, NA_K_ROWS), (0, 0), (0, 0)]
    slabs = jnp.pad(slabs, pad, constant_values=NEG_INF)

    t = np.arange(NA_K_ROWS)
    last_start = NA_K_ROWS - NA_WIN_ROWS
    tables = []
    for case in range(3):
        per_query_row = []
        for a in range(NA_Q_ROWS):
            row_ok, first = [
                (t < NA_WIN_ROWS, NA_WIN_ROWS - 1 - a),
                ((t >= a) & (t < a + NA_WIN_ROWS), NA_WIN_ROWS // 2 - 1 - a),
                (t >= last_start, NA_Q_ROWS - NA_K_ROWS + NA_WIN_ROWS - 1 - a),
            ][case]
            rows = slabs[..., NA_K_ROWS + first:NA_K_ROWS + first + NA_K_ROWS, :, :]
            rows = jnp.where(row_ok[:, None, None], rows, NEG_INF)
            per_query_row.append(jnp.swapaxes(rows, -3, -2))
        tables.append(jnp.stack(per_query_row, axis=-4).reshape(*lead, NA_Q, NA_K))
    return jnp.stack(tables, axis=-3)


def _neighbourhood_attention(na, bias, layer, n_batch):
    n = na.shape[0]
    steps = 1 + NA_STEPS
    tiles = TOK // NA_Q

    def table(b, h, j):
        return (layer, h, jnp.where(j <= 1, 0, jnp.where(j == NA_STEPS, 2, 1)), 0, 0)

    return pl.pallas_call(
        _na_kernel,
        grid=(n_batch, N_HEADS, steps),
        in_specs=[
            pl.BlockSpec((NA_Q, HEAD_DIM), lambda b, h, j: (b * tiles + j, h)),
            pl.BlockSpec((TOK, HEAD_DIM), lambda b, h, j: (b, N_HEADS + h)),
            pl.BlockSpec((TOK, HEAD_DIM), lambda b, h, j: (b, 2 * N_HEADS + h)),
            pl.BlockSpec((1, 1, 1, NA_Q, NA_K), table),
        ],
        out_specs=pl.BlockSpec((NA_Q, HEAD_DIM), lambda b, h, j: (b * tiles + j, h)),
        out_shape=jax.ShapeDtypeStruct((n, WIDTH), F32),
        compiler_params=_params(("parallel", "parallel", "arbitrary")),
        name="neighbourhood_attention",
    )(na, na, na, bias)


def _merge_kernel(r_ref, m_ref, a_ref, bg_ref, x_ref, mod_ref, wb_ref, wo_ref, g2_ref, wr_hi_ref, wr_lo_ref,
                  x_out_ref, h2_ref, logit_ref):
    gate = jax.nn.sigmoid(bg_ref[...])
    mix = (gate[:, 0:D_MODEL] * _dot(r_ref[...].astype(BF16), wb_ref[0, 0])
           + gate[:, D_MODEL:2 * D_MODEL] * _dot(m_ref[...].astype(BF16), wb_ref[0, 1])
           + gate[:, 2 * D_MODEL:] * _dot(a_ref[...].astype(BF16), wb_ref[0, 2]))
    y = _dot(mix.astype(BF16), wo_ref[0])
    x_new = x_ref[...] + mod_ref[0, 2:3, :] * y
    x_out_ref[...] = x_new
    h2 = _rms_modulate(x_new, g2_ref[0], mod_ref[0, 3:4, :], mod_ref[0, 4:5, :])
    h2_ref[...] = h2
    hi = h2.astype(BF16)
    lo = (h2 - hi.astype(F32)).astype(BF16)
    logit_ref[...] = _dot(hi, wr_hi_ref[0]) + (_dot(lo, wr_hi_ref[0]) + _dot(hi, wr_lo_ref[0]))


def _merge(r, m, a, bg, x_all, mod, w_branch, w_out, norm2_g, wr_hi, wr_lo, layer, n_batch):
    n = x_all.shape[0]
    row = lambda i: (i, 0)

    def layer_block(w):
        return pl.BlockSpec((1,) + w.shape[1:], lambda i: (layer,) + (0,) * (w.ndim - 1))

    return pl.pallas_call(
        _merge_kernel,
        grid=(n // TM,),
        in_specs=[
            pl.BlockSpec((TM, WIDTH), row), pl.BlockSpec((TM, WIDTH), row), pl.BlockSpec((TM, WIDTH), row),
            pl.BlockSpec((TM, 3 * D_MODEL), row),
            pl.BlockSpec((TM, D_MODEL), row),
            pl.BlockSpec((1, 6, D_MODEL), _mod_row(n_batch)),
            layer_block(w_branch), layer_block(w_out),
            pl.BlockSpec((1, 1, D_MODEL), lambda i: (layer, 0, 0)),
            layer_block(wr_hi), layer_block(wr_lo),
        ],
        out_specs=[pl.BlockSpec((TM, D_MODEL), row), pl.BlockSpec((TM, D_MODEL), row), pl.BlockSpec((TM, LANES), row)],
        out_shape=[jax.ShapeDtypeStruct((n, D_MODEL), F32), jax.ShapeDtypeStruct((n, D_MODEL), F32),
                   jax.ShapeDtypeStruct((n, LANES), F32)],
        compiler_params=_params(("parallel",), VMEM_LIMIT),
        name="merge",
    )(r, m, a, bg, x_all, mod, w_branch, w_out, norm2_g.reshape(-1, 1, D_MODEL), wr_hi, wr_lo)


ROUTE_E1, ROUTE_E2, ROUTE_RANK1, ROUTE_RANK2, ROUTE_W1, ROUTE_W2 = range(6)


def _route_kernel(logit_ref, route_ref, count_ref, cnt_ref):
    @pl.when(pl.program_id(0) == 0)
    def _():
        cnt_ref[...] = jnp.zeros_like(cnt_ref)

    lg = logit_ref[...]
    lane = lax.broadcasted_iota(jnp.int32, lg.shape, 1)
    lane_f = lane.astype(F32)

    def first_argmax(vals):
        top = jnp.max(vals, axis=-1, keepdims=True)
        idx = jnp.min(jnp.where(vals == top, lane_f, float(LANES)), axis=-1, keepdims=True)
        return top, idx

    group_logits = jnp.where(lane < N_GROUPS, lg, -jnp.inf)
    g_top, g_idx = first_argmax(group_logits)
    group_w = 1.0 / jnp.sum(jnp.exp(group_logits - g_top), axis=-1, keepdims=True)

    first = N_GROUPS + EXPERTS_PER_GROUP * g_idx
    in_group = (lane_f >= first) & (lane_f < first + EXPERTS_PER_GROUP)
    expert_logits = jnp.where(in_group, lg, -jnp.inf)
    v1, i1 = first_argmax(expert_logits)
    v2, i2 = first_argmax(jnp.where(lane_f == i1, -jnp.inf, expert_logits))
    t = jnp.exp(v2 - v1)
    w1 = group_w / (1.0 + t)
    w2 = group_w * t / (1.0 + t)

    oh1 = (lane_f == i1).astype(F32)
    oh2 = (lane_f == i2).astype(F32)
    both = oh1 + oh2
    rows = lg.shape[0]
    earlier = (lax.broadcasted_iota(jnp.int32, (rows, rows), 1)
               < lax.broadcasted_iota(jnp.int32, (rows, rows), 0)).astype(BF16)
    before = _dot(earlier, both.astype(BF16)) + cnt_ref[...]
    rank1 = jnp.sum(oh1 * before, axis=-1, keepdims=True)
    rank2 = jnp.sum(oh2 * before, axis=-1, keepdims=True)
    cnt_ref[...] += jnp.sum(both, axis=0, keepdims=True)
    count_ref[...] = jnp.broadcast_to(cnt_ref[...], count_ref.shape)

    out = jnp.zeros_like(lg)
    for slot, val in ((ROUTE_E1, i1 - N_GROUPS), (ROUTE_E2, i2 - N_GROUPS), (ROUTE_RANK1, rank1),
                      (ROUTE_RANK2, rank2), (ROUTE_W1, w1), (ROUTE_W2, w2)):
        out = jnp.where(lane == slot, val, out)
    route_ref[...] = out


def _route(logits):
    n = logits.shape[0]
    return pl.pallas_call(
        _route_kernel,
        grid=(n // ROUTE_TM,),
        in_specs=[pl.BlockSpec((ROUTE_TM, LANES), lambda i: (i, 0))],
        out_specs=[pl.BlockSpec((ROUTE_TM, LANES), lambda i: (i, 0)), pl.BlockSpec((8, LANES), lambda i: (0, 0))],
        out_shape=[jax.ShapeDtypeStruct((n, LANES), F32), jax.ShapeDtypeStruct((8, LANES), F32)],
        scratch_shapes=[pltpu.VMEM((1, LANES), F32)],
        compiler_params=_params(("arbitrary",)),
        name="route",
    )(logits)


def _dispatch_plan(route, counts):
    n = route.shape[0]
    counts = counts[0, N_GROUPS:N_GROUPS + N_EXPERTS].astype(jnp.int32)
    padded = (counts + EXPERT_BLOCK - 1) // EXPERT_BLOCK * EXPERT_BLOCK
    pad_end = jnp.cumsum(padded)
    pad_start = pad_end - padded
    e = route[:, ROUTE_E1:ROUTE_E2 + 1].astype(jnp.int32)
    rank = route[:, ROUTE_RANK1:ROUTE_RANK2 + 1].astype(jnp.int32)
    dest = pad_start[e] + rank
    n_blocks = (2 * n + N_EXPERTS * (EXPERT_BLOCK - 1) + EXPERT_BLOCK - 1) // EXPERT_BLOCK
    block_expert = jnp.minimum(
        jnp.sum(jnp.arange(n_blocks)[:, None] * EXPERT_BLOCK >= pad_end[None, :], axis=1), N_EXPERTS - 1)
    used_blocks = (pad_end[-1] // EXPERT_BLOCK).reshape(1)
    dest_tiles = dest.reshape(n // TM, TM, 2).transpose(0, 2, 1).reshape(n // TM, 1, 2 * TM)
    return dest_tiles, block_expert.astype(jnp.int32), used_blocks.astype(jnp.int32), n_blocks


def _row_copy(src_ref, src_row, dst_ref, dst_row, sem):
    return pltpu.make_async_copy(src_ref.at[pl.ds(src_row, 1)], dst_ref.at[pl.ds(dst_row, 1)], sem)


def _dispatch_kernel(dest_ref, h_ref, xs_in_ref, xs_ref, sem):
    del xs_in_ref

    def start(t, carry):
        _row_copy(h_ref, t, xs_ref, dest_ref[0, 0, t], sem).start()
        _row_copy(h_ref, t, xs_ref, dest_ref[0, 0, TM + t], sem).start()
        return carry

    lax.fori_loop(0, TM, start, 0)

    def wait(t, carry):
        _row_copy(h_ref, 0, xs_ref, 0, sem).wait()
        return carry

    lax.fori_loop(0, 2 * TM, wait, 0)


def _dispatch(h2, dest_tiles, n_rows):
    n = h2.shape[0]
    xs0 = jnp.zeros((n_rows, D_MODEL), F32)
    return pl.pallas_call(
        _dispatch_kernel,
        grid=(n // TM,),
        in_specs=[
            pl.BlockSpec((1, 1, 2 * TM), lambda i: (i, 0, 0), memory_space=pltpu.SMEM),
            pl.BlockSpec((TM, D_MODEL), lambda i: (i, 0)),
            pl.BlockSpec(memory_space=pl.ANY),
        ],
        out_specs=pl.BlockSpec(memory_space=pl.ANY),
        out_shape=jax.ShapeDtypeStruct((n_rows, D_MODEL), F32),
        scratch_shapes=[pltpu.SemaphoreType.DMA(())],
        input_output_aliases={2: 0},
        compiler_params=_params(("arbitrary",)),
        name="dispatch",
    )(dest_tiles, h2, xs0)


def _expert_kernel(be_ref, used_ref, x_ref, wg_ref, wu_ref, wd_ref, y_ref, wg_s, wu_s, wd_s):
    i = pl.program_id(0)
    new_expert = jnp.logical_or(i == 0, be_ref[i] != be_ref[jnp.maximum(i - 1, 0)])

    @pl.when(new_expert)
    def _():
        wg_s[...] = wg_ref[0, 0].astype(BF16)
        wu_s[...] = wu_ref[0, 0].astype(BF16)
        wd_s[...] = wd_ref[0, 0].astype(BF16)

    @pl.when(i < used_ref[0])
    def _():
        x = x_ref[...].astype(BF16)
        hidden = _silu(_dot(x, wg_s[...])) * _dot(x, wu_s[...])
        y_ref[...] = _dot(hidden.astype(BF16), wd_s[...])

    @pl.when(i >= used_ref[0])
    def _():
        y_ref[...] = jnp.zeros_like(y_ref)


def _experts(xs, block_expert, used_blocks, w_gate, w_up, w_down, layer):
    n_rows = xs.shape[0]
    n_blocks = n_rows // EXPERT_BLOCK
    grid_spec = pltpu.PrefetchScalarGridSpec(
        num_scalar_prefetch=2,
        grid=(n_blocks,),
        in_specs=[
            pl.BlockSpec((EXPERT_BLOCK, D_MODEL), lambda i, be, used: (i, 0)),
            pl.BlockSpec((1, 1, D_MODEL, EXPERT_FF), lambda i, be, used: (layer, be[i], 0, 0)),
            pl.BlockSpec((1, 1, D_MODEL, EXPERT_FF), lambda i, be, used: (layer, be[i], 0, 0)),
            pl.BlockSpec((1, 1, EXPERT_FF, D_MODEL), lambda i, be, used: (layer, be[i], 0, 0)),
        ],
        out_specs=pl.BlockSpec((EXPERT_BLOCK, D_MODEL), lambda i, be, used: (i, 0)),
        scratch_shapes=[
            pltpu.VMEM((D_MODEL, EXPERT_FF), BF16),
            pltpu.VMEM((D_MODEL, EXPERT_FF), BF16),
            pltpu.VMEM((EXPERT_FF, D_MODEL), BF16),
        ],
    )
    return pl.pallas_call(
        _expert_kernel,
        grid_spec=grid_spec,
        out_shape=jax.ShapeDtypeStruct((n_rows, D_MODEL), F32),
        compiler_params=_params(("arbitrary",), VMEM_LIMIT),
        name="experts",
    )(block_expert, used_blocks, xs, w_gate, w_up, w_down)


def _combine_kernel(dest_ref, ys_ref, x_ref, route_ref, mod_ref, fg_ref, o_ref, y1_ref, y2_ref, sem, *, final):
    def start(t, carry):
        _row_copy(ys_ref, dest_ref[0, 0, t], y1_ref, t, sem).start()
        _row_copy(ys_ref, dest_ref[0, 0, TM + t], y2_ref, t, sem).start()
        return carry

    lax.fori_loop(0, TM, start, 0)

    def wait(t, carry):
        _row_copy(ys_ref, 0, y1_ref, 0, sem).wait()
        return carry

    lax.fori_loop(0, 2 * TM, wait, 0)

    route = route_ref[...]
    w1 = route[:, ROUTE_W1:ROUTE_W1 + 1]
    w2 = route[:, ROUTE_W2:ROUTE_W2 + 1]
    x_new = x_ref[...] + mod_ref[0, 5:6, :] * (y1_ref[...] * w1 + y2_ref[...] * w2)
    if final:
        x_new = x_new * lax.rsqrt(jnp.mean(x_new * x_new, axis=-1, keepdims=True) + EPS) * fg_ref[...]
    o_ref[...] = x_new


def _combine(ys, dest_tiles, x_all, route, mod, final_g, n_batch, final):
    n = x_all.shape[0]
    row = lambda i: (i, 0)
    return pl.pallas_call(
        functools.partial(_combine_kernel, final=final),
        grid=(n // TM,),
        in_specs=[
            pl.BlockSpec((1, 1, 2 * TM), lambda i: (i, 0, 0), memory_space=pltpu.SMEM),
            pl.BlockSpec(memory_space=pl.ANY),
            pl.BlockSpec((TM, D_MODEL), row),
            pl.BlockSpec((TM, LANES), row),
            pl.BlockSpec((1, 6, D_MODEL), _mod_row(n_batch)),
            pl.BlockSpec((1, D_MODEL), lambda i: (0, 0)),
        ],
        out_specs=pl.BlockSpec((TM, D_MODEL), row),
        out_shape=jax.ShapeDtypeStruct((n, D_MODEL), F32),
        scratch_shapes=[
            pltpu.VMEM((TM, D_MODEL), F32),
            pltpu.VMEM((TM, D_MODEL), F32),
            pltpu.SemaphoreType.DMA(()),
        ],
        compiler_params=_params(("arbitrary",)),
        name="combine",
    )(dest_tiles, ys, x_all, route, mod, final_g.reshape(1, D_MODEL))


def _rope_tables():
    t = jnp.arange(SEQ)
    rows = (t // GRID_W).astype(F32)
    cols = (t % GRID_W).astype(F32)
    n_freq = HEAD_DIM // 4
    inv_freq = ROPE_BASE ** (-jnp.arange(n_freq, dtype=F32) / n_freq)
    ang_r = rows[:, None] * inv_freq
    ang_c = cols[:, None] * inv_freq
    cos = jnp.concatenate([jnp.cos(ang_r)] * 2 + [jnp.cos(ang_c)] * 2, axis=1)
    sin = jnp.concatenate([-jnp.sin(ang_r), jnp.sin(ang_r), -jnp.sin(ang_c), jnp.sin(ang_c)], axis=1)
    return cos, sin


def kernel(x, c, ctx, c_ctx, w_mod, b_mod, norm1_g, norm2_g, w_in, ret_decay, ret_norm_g, conv_w, conv_b,
           mlstm_gate_b, mlstm_norm_g, na_rpb, w_branch, w_out, w_group, w_router, w_expert_gate,
           w_expert_up, w_expert_down, final_norm_g):
    n_batch, seq, d = x.shape
    depth = w_mod.shape[0]
    assert (seq, d, ctx.shape[1]) == (SEQ, D_MODEL, CTX_LEN)
    n = n_batch * TOK

    cond_rows = -(-(n_batch + 1) // 8) * 8
    cond = jnp.zeros((cond_rows, d), F32).at[:n_batch].set(c).at[n_batch].set(c_ctx)
    mod_all = _modulation(cond, w_mod, b_mod).reshape(depth, cond_rows, 6, d)

    cos, sin = _rope_tables()
    x_all = jnp.concatenate([ctx, x], axis=1).reshape(n, d)

    w_packed = _pack_in_weights(w_in)
    w_branch_b = w_branch.astype(BF16)
    w_out_b = w_out.astype(BF16)
    w_route = jnp.concatenate([w_group, w_router], axis=-1)
    w_route = jnp.pad(w_route, ((0, 0), (0, 0), (0, LANES - w_route.shape[-1])))
    wr_hi = w_route.astype(BF16)
    wr_lo = (w_route - wr_hi.astype(F32)).astype(BF16)
    gate_b = jnp.pad(mlstm_gate_b.reshape(depth, 1, 4 * N_HEADS), ((0, 0), (0, 0), (0, LANES - 4 * N_HEADS)))
    na_bias = _na_bias_tables(na_rpb)

    for layer in range(depth):
        mod = mod_all[layer]
        ret, ml, gates, na, bg = _in_projection(x_all, mod, norm1_g, w_packed, layer, n_batch)
        r_out = _retention(ret, ret_decay[layer], ret_norm_g[layer], cos, sin, n_batch)
        m_out = _mlstm(ml, gates, gate_b[layer], conv_w[layer], conv_b[layer].reshape(1, 2 * WIDTH),
                       mlstm_norm_g[layer], n_batch)
        a_out = _neighbourhood_attention(na, na_bias, layer, n_batch)
        x_all, h2, logits = _merge(r_out, m_out, a_out, bg, x_all, mod, w_branch_b, w_out_b, norm2_g,
                                   wr_hi, wr_lo, layer, n_batch)
        route, counts = _route(logits)
        dest_tiles, block_expert, used_blocks, n_blocks = _dispatch_plan(route, counts)
        xs = _dispatch(h2, dest_tiles, n_blocks * EXPERT_BLOCK)
        ys = _experts(xs, block_expert, used_blocks, w_expert_gate, w_expert_up, w_expert_down, layer)
        x_all = _combine(ys, dest_tiles, x_all, route, mod, final_norm_g, n_batch, final=layer == depth - 1)

    return x_all.reshape(n_batch, TOK, d)[:, CTX_LEN:]
```

```python
import functools

import numpy as np
import jax
import jax.numpy as jnp
from jax import lax
from jax.experimental import pallas as pl
from jax.experimental.pallas import tpu as pltpu

F32 = jnp.float32
BF16 = jnp.bfloat16

D_MODEL = 1024
SEQ = 2048
CTX_LEN = 256
TOK = CTX_LEN + SEQ
GRID_W = 64
GRID_ROWS = SEQ // GRID_W
HEAD_DIM = 128
N_HEADS = 4
WIDTH = N_HEADS * HEAD_DIM
CHUNK = 128
N_CHUNKS = TOK // CHUNK
CTX_CHUNKS = CTX_LEN // CHUNK
CONV_WIDTH = 5
NA_WIN_ROWS = 8
NA_WIN_COLS = 16
NA_Q_ROWS = 4
NA_Q = NA_Q_ROWS * GRID_W
NA_K_ROWS = NA_Q_ROWS + NA_WIN_ROWS
NA_K = NA_K_ROWS * GRID_W
NA_STEPS = SEQ // NA_Q
ROPE_BASE = 10000.0
N_GROUPS = 4
EXPERTS_PER_GROUP = 8
N_EXPERTS = N_GROUPS * EXPERTS_PER_GROUP
EXPERT_FF = 512
EXPERT_BLOCK = 256
EPS = 1e-6
NEG_INF = -1e30
QK_SCALE = HEAD_DIM ** -0.5

TM = 256
ROUTE_TM = 512
LANES = 128
VMEM_LIMIT = 56 * 1024 * 1024


def _dot(a, b):
    return jnp.dot(a, b, preferred_element_type=F32)


def _dot_nt(a, b):
    return lax.dot_general(a, b, (((1,), (1,)), ((), ())), preferred_element_type=F32)


def _bdot(a, b):
    return _dot(a.astype(BF16), b.astype(BF16))


def _bdot_nt(a, b):
    return _dot_nt(a.astype(BF16), b.astype(BF16))


def _bdot_tn(a, b):
    return _dot(a.T.astype(BF16), b.astype(BF16))


def _log_sigmoid(x):
    return jnp.minimum(x, 0.0) - jnp.log1p(jnp.exp(-jnp.abs(x)))


def _silu(x):
    return x * jax.nn.sigmoid(x)


def _params(sem, vmem=None):
    return pltpu.CompilerParams(dimension_semantics=sem, vmem_limit_bytes=vmem)


def _mod_row(n_batch):
    tiles = TOK // TM

    def index(i):
        return (jnp.where(i % tiles == 0, n_batch, i // tiles), 0, 0)

    return index


def _mod_kernel(c_ref, w_ref, b_ref, o_ref):
    cond = _silu(c_ref[...])
    o_ref[0] = _bdot(cond, w_ref[0]) + b_ref[0]


def _modulation(cond, w_mod, b_mod):
    depth, d, cols = w_mod.shape
    rows = cond.shape[0]
    tn = 1536
    return pl.pallas_call(
        _mod_kernel,
        grid=(depth, cols // tn),
        in_specs=[
            pl.BlockSpec((rows, d), lambda l, j: (0, 0)),
            pl.BlockSpec((1, d, tn), lambda l, j: (l, 0, j)),
            pl.BlockSpec((1, 1, tn), lambda l, j: (l, 0, j)),
        ],
        out_specs=pl.BlockSpec((1, rows, tn), lambda l, j: (l, 0, j)),
        out_shape=jax.ShapeDtypeStruct((depth, rows, cols), F32),
        compiler_params=_params(("parallel", "parallel")),
        name="modulation",
    )(cond, w_mod, b_mod.reshape(depth, 1, cols))


def _rms_modulate(x, g, shift, scale):
    y = x * lax.rsqrt(jnp.mean(x * x, axis=-1, keepdims=True) + EPS) * g
    return y * (1.0 + scale) + shift


IN_WIDTHS = (4 * WIDTH, 4 * WIDTH, LANES, 3 * WIDTH, 3 * D_MODEL)
IN_OFFSETS = tuple(int(v) for v in np.cumsum((0,) + IN_WIDTHS))


def _pack_in_weights(w_in):
    g1 = 2 * 4 * WIDTH + 4 * N_HEADS
    lane_pad = jnp.zeros(w_in.shape[:-1] + (LANES - 4 * N_HEADS,), w_in.dtype)
    return jnp.concatenate([w_in[..., :g1], lane_pad, w_in[..., g1:]], axis=-1).astype(BF16)


def _inproj_kernel(x_ref, mod_ref, g_ref, w_ref, *out_refs):
    h = _rms_modulate(x_ref[...], g_ref[0], mod_ref[0, 0:1, :], mod_ref[0, 1:2, :]).astype(BF16)
    for o_ref, lo, hi in zip(out_refs, IN_OFFSETS[:-1], IN_OFFSETS[1:]):
        o_ref[...] = _dot(h, w_ref[0, :, lo:hi])


def _in_projection(x_all, mod, norm_g, w_packed, layer, n_batch):
    n = x_all.shape[0]
    row = lambda i: (i, 0)
    return pl.pallas_call(
        _inproj_kernel,
        grid=(n // TM,),
        in_specs=[
            pl.BlockSpec((TM, D_MODEL), row),
            pl.BlockSpec((1, 6, D_MODEL), _mod_row(n_batch)),
            pl.BlockSpec((1, 1, D_MODEL), lambda i: (layer, 0, 0)),
            pl.BlockSpec((1,) + w_packed.shape[1:], lambda i: (layer, 0, 0), pipeline_mode=pl.Buffered(1)),
        ],
        out_specs=[pl.BlockSpec((TM, w), row) for w in IN_WIDTHS],
        out_shape=[jax.ShapeDtypeStruct((n, w), F32) for w in IN_WIDTHS],
        compiler_params=_params(("parallel",), VMEM_LIMIT),
        name="in_projection",
    )(x_all, mod, norm_g.reshape(-1, 1, D_MODEL), w_packed)


def _chunk_order(t):
    fwd = t
    bwd = jnp.where(t < CTX_CHUNKS, CTX_CHUNKS - 1 - t, N_CHUNKS + CTX_CHUNKS - 1 - t)
    return fwd, bwd


def _chunk_slice(c):
    return pl.ds(pl.multiple_of(c * CHUNK, CHUNK), CHUNK)


def _head_norm(y, gain):
    mu = jnp.mean(y, axis=-1, keepdims=True)
    yc = y - mu
    var = jnp.mean(yc * yc, axis=-1, keepdims=True)
    return yc * lax.rsqrt(var + EPS) * gain


ROW_TILE = D_MODEL // LANES


def _store_token_tiles(ref, x, pitch, offset=0):
    rows = x.shape[0]
    for c in range(ROW_TILE):
        ref[pl.ds(offset + c, rows, stride=pitch), :] = x[:, c * LANES:(c + 1) * LANES]


def _load_token_slab(ref, rows, pitch, c, offset=0):
    return ref[pl.ds(offset + c, rows, stride=pitch), :]


def _select_lane(x, lane, idx):
    return jnp.sum(jnp.where(lane == idx, x, 0.0), axis=-1, keepdims=True)


def _retention_kernel(dec_ref, q_ref, k_ref, v_ref, g_ref, cos_ref, sin_ref, gn_ref, o_ref,
                      qs_ref, ks_ref, st_ref):
    head = pl.program_id(1)

    lane = lax.broadcasted_iota(jnp.int32, (SEQ, HEAD_DIM), 1)
    first_half = (lane % (HEAD_DIM // 2)) < (HEAD_DIM // 4)
    cos = cos_ref[...]
    sin = sin_ref[...]

    def rope(x):
        rot = jnp.where(first_half, pltpu.roll(x, HEAD_DIM - HEAD_DIM // 4, 1), pltpu.roll(x, HEAD_DIM // 4, 1))
        return x * cos + rot * sin

    qs_ref[0:CTX_LEN, :] = q_ref[0:CTX_LEN, :]
    ks_ref[0:CTX_LEN, :] = k_ref[0:CTX_LEN, :] * QK_SCALE
    qs_ref[CTX_LEN:, :] = rope(q_ref[CTX_LEN:, :])
    ks_ref[CTX_LEN:, :] = rope(k_ref[CTX_LEN:, :]) * QK_SCALE

    dec = _log_sigmoid(dec_ref[...])
    hl = lax.broadcasted_iota(jnp.int32, dec.shape, 1)
    lg = jnp.sum(jnp.where(hl == head, dec, 0.0), axis=-1, keepdims=True)
    lg_f, lg_b = lg[0:1, :], lg[1:2, :]

    ii = lax.broadcasted_iota(jnp.int32, (CHUNK, CHUNK), 0).astype(F32)
    jj = lax.broadcasted_iota(jnp.int32, (CHUNK, CHUNK), 1).astype(F32)
    col = lax.broadcasted_iota(jnp.int32, (CHUNK, 1), 0).astype(F32)

    def decay_mat(dist, lg_dir):
        ok = dist >= 0
        return jnp.where(ok, jnp.exp(jnp.where(ok, dist, 0.0) * lg_dir), 0.0)

    intra = (decay_mat(ii - jj, lg_f), decay_mat(jj - ii, lg_b))
    q_decay = (jnp.exp((col + 1.0) * lg_f), jnp.exp((CHUNK - col) * lg_b))
    k_decay = (jnp.exp((CHUNK - 1.0 - col) * lg_f), jnp.exp(col * lg_b))
    chunk_decay = (jnp.exp(CHUNK * lg_f), jnp.exp(CHUNK * lg_b))

    st_ref[...] = jnp.zeros_like(st_ref)
    o_ref[...] = jnp.zeros_like(o_ref)

    def step(t, carry):
        for d, c in enumerate(_chunk_order(t)):
            rows = _chunk_slice(c)
            q = qs_ref[rows, :]
            k = ks_ref[rows, :]
            v = v_ref[rows, :]
            s_prev = st_ref[d]
            scores = _bdot_nt(q, k) * intra[d]
            o_ref[rows, :] += _bdot(scores, v) + _bdot(q * q_decay[d], s_prev)
            st_ref[d] = s_prev * chunk_decay[d] + _bdot_tn(k * k_decay[d], v)
        return carry

    lax.fori_loop(0, N_CHUNKS, step, 0)

    o_ref[...] = _head_norm(o_ref[...], gn_ref[...]) * _silu(g_ref[...])


def _retention(ret, ret_decay, norm_g, cos, sin, n_batch):
    n = ret.shape[0]

    def head_block(offset):
        return pl.BlockSpec((TOK, HEAD_DIM), lambda b, h: (b, offset + h))

    return pl.pallas_call(
        _retention_kernel,
        grid=(n_batch, N_HEADS),
        in_specs=[
            pl.BlockSpec(ret_decay.shape, lambda b, h: (0, 0)),
            head_block(0), head_block(N_HEADS), head_block(2 * N_HEADS), head_block(3 * N_HEADS),
            pl.BlockSpec((SEQ, HEAD_DIM), lambda b, h: (0, 0)),
            pl.BlockSpec((SEQ, HEAD_DIM), lambda b, h: (0, 0)),
            pl.BlockSpec((1, HEAD_DIM), lambda b, h: (0, h)),
        ],
        out_specs=pl.BlockSpec((TOK, HEAD_DIM), lambda b, h: (b, h)),
        out_shape=jax.ShapeDtypeStruct((n, WIDTH), F32),
        scratch_shapes=[
            pltpu.VMEM((TOK, HEAD_DIM), F32),
            pltpu.VMEM((TOK, HEAD_DIM), F32),
            pltpu.VMEM((2, HEAD_DIM, HEAD_DIM), F32),
        ],
        compiler_params=_params(("parallel", "parallel")),
        name="retention",
    )(ret_decay, ret, ret, ret, ret, cos, sin, norm_g.reshape(1, WIDTH))


CONV_PAD = 8


def _mlstm_kernel(q_ref, k_ref, v_ref, og_ref, gate_ref, gb_ref, cwq_ref, cwk_ref, cbq_ref, cbk_ref,
                  gn_ref, o_ref, qs_ref, ks_ref, pad_ref, gx_ref, gxt_ref, cst_ref):
    head = pl.program_id(1)

    trow = lax.broadcasted_iota(jnp.int32, (TOK, 1), 0)
    pad_ref[0:CONV_PAD, :] = jnp.zeros((CONV_PAD, HEAD_DIM), F32)
    pad_ref[CONV_PAD + TOK:, :] = jnp.zeros((CONV_PAD, HEAD_DIM), F32)

    def conv(u_ref, w_ref, b_ref):
        pad_ref[CONV_PAD:CONV_PAD + TOK, :] = u_ref[...]
        acc = jnp.zeros((TOK, HEAD_DIM), F32)
        for j in range(CONV_WIDTH):
            shift = j - CONV_WIDTH // 2
            tap = pad_ref[CONV_PAD + shift:CONV_PAD + shift + TOK, :]
            same_segment = (trow < CTX_LEN) == (trow + shift < CTX_LEN)
            acc = acc + jnp.where(same_segment, tap, 0.0) * w_ref[j:j + 1, :]
        return _silu(acc + b_ref[...])

    qs_ref[...] = conv(q_ref, cwq_ref, cbq_ref)
    ks_ref[...] = conv(k_ref, cwk_ref, cbk_ref) * QK_SCALE

    lane = lax.broadcasted_iota(jnp.int32, (TOK, LANES), 1)
    g = gate_ref[...] + gb_ref[...]
    is_forget = ((lane // N_HEADS) % 2) == 1
    gx_ref[...] = jnp.where(is_forget, _log_sigmoid(g), g)
    for c in range(N_CHUNKS):
        gxt_ref[c] = gx_ref[c * CHUNK:(c + 1) * CHUNK, :].T

    ii = lax.broadcasted_iota(jnp.int32, (CHUNK, CHUNK), 0)
    jj = lax.broadcasted_iota(jnp.int32, (CHUNK, CHUNK), 1)
    causal = (jj <= ii, jj >= ii)
    clane = lax.broadcasted_iota(jnp.int32, (CHUNK, LANES), 1)

    cst_ref[...] = jnp.zeros_like(cst_ref)
    o_ref[...] = jnp.zeros_like(o_ref)

    def step(t, carry):
        new_carry = []
        for d, c in enumerate(_chunk_order(t)):
            n_prev, m_prev = carry[2 * d], carry[2 * d + 1]
            rows = _chunk_slice(c)
            q = qs_ref[rows, :]
            k = ks_ref[rows, :]
            v = v_ref[rows, :]
            gc = gx_ref[rows, :]
            i_col = _select_lane(gc, clane, 2 * d * N_HEADS + head)
            f_col = _select_lane(gc, clane, (2 * d + 1) * N_HEADS + head)
            i_row = gxt_ref[c, pl.ds(2 * d * N_HEADS + head, 1), :]
            f_row = gxt_ref[c, pl.ds((2 * d + 1) * N_HEADS + head, 1), :]
            vis = causal[d]
            cum_col = jnp.sum(jnp.where(vis, f_row, 0.0), axis=1, keepdims=True)
            cum_row = jnp.sum(jnp.where(causal[1 - d], f_col, 0.0), axis=0, keepdims=True)
            total = jnp.sum(f_row, axis=1, keepdims=True)
            c_prev = cst_ref[d]

            log_kw = total - cum_col + i_col
            m_new = jnp.maximum(total + m_prev, jnp.max(log_kw, axis=0, keepdims=True))
            kw = jnp.exp(log_kw - m_new)
            pw = jnp.exp(total + m_prev - m_new)
            cst_ref[d] = pw * c_prev + _bdot_tn(k * kw, v)
            n_new = pw * n_prev + jnp.sum(kw * k, axis=0, keepdims=True)

            log_w = jnp.where(vis, cum_col - cum_row + i_row, -jnp.inf)
            log_p = cum_col + m_prev
            m_t = jnp.maximum(log_p, jnp.max(log_w, axis=1, keepdims=True))
            w = jnp.exp(log_w - m_t)
            p = jnp.exp(log_p - m_t)
            qk = _bdot_nt(q, k) * w
            num = _bdot(qk, v) + p * _bdot(q, c_prev)
            den = jnp.sum(qk, axis=1, keepdims=True) + p * jnp.sum(q * n_prev, axis=1, keepdims=True)
            o_ref[rows, :] += num / jnp.maximum(jnp.abs(den), jnp.exp(-m_t))
            new_carry += [n_new, m_new]
        return tuple(new_carry)

    zero_n = jnp.zeros((1, HEAD_DIM), F32)
    zero_m = jnp.zeros((1, 1), F32)
    lax.fori_loop(0, N_CHUNKS, step, (zero_n, zero_m, zero_n, zero_m))

    o_ref[...] = _head_norm(o_ref[...] * jax.nn.sigmoid(og_ref[...]), gn_ref[...])


def _mlstm(ml, gates, gate_b, conv_w, conv_b, norm_g, n_batch):
    n = ml.shape[0]

    def head_block(offset):
        return pl.BlockSpec((TOK, HEAD_DIM), lambda b, h: (b, offset + h))

    def head_cols(rows, offset):
        return pl.BlockSpec((rows, HEAD_DIM), lambda b, h: (0, offset + h))

    return pl.pallas_call(
        _mlstm_kernel,
        grid=(n_batch, N_HEADS),
        in_specs=[
            head_block(0), head_block(N_HEADS), head_block(2 * N_HEADS), head_block(3 * N_HEADS),
            pl.BlockSpec((TOK, LANES), lambda b, h: (b, 0)),
            pl.BlockSpec((1, LANES), lambda b, h: (0, 0)),
            head_cols(CONV_WIDTH, 0), head_cols(CONV_WIDTH, N_HEADS),
            head_cols(1, 0), head_cols(1, N_HEADS),
            head_cols(1, 0),
        ],
        out_specs=pl.BlockSpec((TOK, HEAD_DIM), lambda b, h: (b, h)),
        out_shape=jax.ShapeDtypeStruct((n, WIDTH), F32),
        scratch_shapes=[
            pltpu.VMEM((TOK, HEAD_DIM), F32),
            pltpu.VMEM((TOK, HEAD_DIM), F32),
            pltpu.VMEM((TOK + 2 * CONV_PAD, HEAD_DIM), F32),
            pltpu.VMEM((TOK, LANES), F32),
            pltpu.VMEM((N_CHUNKS, LANES, CHUNK), F32),
            pltpu.VMEM((2, HEAD_DIM, HEAD_DIM), F32),
        ],
        compiler_params=_params(("parallel", "parallel")),
        name="mlstm",
    )(ml, ml, ml, ml, gates, gate_b, conv_w, conv_w, conv_b, conv_b, norm_g.reshape(1, WIDTH))


def _na_kernel(q_ref, k_ref, v_ref, bias_ref, o_ref):
    step = pl.program_id(2)
    q = q_ref[...].astype(BF16)
    k_ctx = k_ref[0:CTX_LEN, :].astype(BF16)
    v_ctx = v_ref[0:CTX_LEN, :].astype(BF16)
    s_ctx = _dot_nt(q, k_ctx) * QK_SCALE

    @pl.when(step == 0)
    def _():
        m = jnp.max(s_ctx, axis=-1, keepdims=True)
        p = jnp.exp(s_ctx - m)
        o_ref[...] = _dot(p.astype(BF16), v_ctx) / jnp.sum(p, axis=-1, keepdims=True)

    @pl.when(step > 0)
    def _():
        start = CTX_LEN + NA_Q * jnp.clip(step - 2, 0, NA_STEPS - NA_K_ROWS // NA_Q_ROWS)
        rows = pl.ds(pl.multiple_of(start, NA_Q), NA_K)
        s_loc = _dot_nt(q, k_ref[rows, :].astype(BF16)) * QK_SCALE + bias_ref[0, 0, 0]
        m = jnp.maximum(jnp.max(s_loc, axis=-1, keepdims=True), jnp.max(s_ctx, axis=-1, keepdims=True))
        p_loc = jnp.exp(s_loc - m)
        p_ctx = jnp.exp(s_ctx - m)
        denom = jnp.sum(p_loc, axis=-1, keepdims=True) + jnp.sum(p_ctx, axis=-1, keepdims=True)
        o = _dot(p_loc.astype(BF16), v_ref[rows, :].astype(BF16)) + _dot(p_ctx.astype(BF16), v_ctx)
        o_ref[...] = o / denom


def _na_bias_tables(rpb):
    lead = rpb.shape[:-2]
    n_row_off, n_col_off = 2 * NA_WIN_ROWS - 1, 2 * NA_WIN_COLS - 1
    qc = np.arange(GRID_W)[:, None]
    kc = np.arange(GRID_W)[None, :]
    col_start = np.clip(qc - NA_WIN_COLS // 2, 0, GRID_W - NA_WIN_COLS)
    col_ok = (kc >= col_start) & (kc < col_start + NA_WIN_COLS)
    col_idx = np.clip(kc - qc + NA_WIN_COLS - 1, 0, n_col_off - 1)
    onehot = ((col_idx[None] == np.arange(n_col_off)[:, None, None]) & col_ok[None]).astype(np.float32)
    slabs = jnp.einsum('...rc,cqk->...rqk', rpb.astype(F32), onehot, precision=lax.Precision.HIGHEST)
    slabs = jnp.where(col_ok, slabs, NEG_INF)
    pad = [(0, 0)] * len(lead) + [(NA_K_ROWS, NA_K_ROWS), (0, 0), (0, 0)]
    slabs = jnp.pad(slabs, pad, constant_values=NEG_INF)

    t = np.arange(NA_K_ROWS)
    last_start = NA_K_ROWS - NA_WIN_ROWS
    tables = []
    for case in range(3):
        per_query_row = []
        for a in range(NA_Q_ROWS):
            row_ok, first = [
                (t < NA_WIN_ROWS, NA_WIN_ROWS - 1 - a),
                ((t >= a) & (t < a + NA_WIN_ROWS), NA_WIN_ROWS // 2 - 1 - a),
                (t >= last_start, NA_Q_ROWS - NA_K_ROWS + NA_WIN_ROWS - 1 - a),
            ][case]
            rows = slabs[..., NA_K_ROWS + first:NA_K_ROWS + first + NA_K_ROWS, :, :]
            rows = jnp.where(row_ok[:, None, None], rows, NEG_INF)
            per_query_row.append(jnp.swapaxes(rows, -3, -2))
        tables.append(jnp.stack(per_query_row, axis=-4).reshape(*lead, NA_Q, NA_K))
    return jnp.stack(tables, axis=-3)


def _neighbourhood_attention(na, bias, layer, n_batch):
    n = na.shape[0]
    steps = 1 + NA_STEPS
    tiles = TOK // NA_Q

    def table(b, h, j):
        return (layer, h, jnp.where(j <= 1, 0, jnp.where(j == NA_STEPS, 2, 1)), 0, 0)

    return pl.pallas_call(
        _na_kernel,
        grid=(n_batch, N_HEADS, steps),
        in_specs=[
            pl.BlockSpec((NA_Q, HEAD_DIM), lambda b, h, j: (b * tiles + j, h)),
            pl.BlockSpec((TOK, HEAD_DIM), lambda b, h, j: (b, N_HEADS + h)),
            pl.BlockSpec((TOK, HEAD_DIM), lambda b, h, j: (b, 2 * N_HEADS + h)),
            pl.BlockSpec((1, 1, 1, NA_Q, NA_K), table),
        ],
        out_specs=pl.BlockSpec((NA_Q, HEAD_DIM), lambda b, h, j: (b * tiles + j, h)),
        out_shape=jax.ShapeDtypeStruct((n, WIDTH), F32),
        compiler_params=_params(("parallel", "parallel", "arbitrary")),
        name="neighbourhood_attention",
    )(na, na, na, bias)


def _merge_kernel(r_ref, m_ref, a_ref, bg_ref, x_ref, mod_ref, wb_ref, wo_ref, g2_ref, wr_hi_ref, wr_lo_ref,
                  x_out_ref, h2_ref, logit_ref):
    gate = jax.nn.sigmoid(bg_ref[...])
    mix = (gate[:, 0:D_MODEL] * _dot(r_ref[...].astype(BF16), wb_ref[0, 0])
           + gate[:, D_MODEL:2 * D_MODEL] * _dot(m_ref[...].astype(BF16), wb_ref[0, 1])
           + gate[:, 2 * D_MODEL:] * _dot(a_ref[...].astype(BF16), wb_ref[0, 2]))
    y = _dot(mix.astype(BF16), wo_ref[0])
    x_new = x_ref[...] + mod_ref[0, 2:3, :] * y
    x_out_ref[...] = x_new
    h2 = _rms_modulate(x_new, g2_ref[0], mod_ref[0, 3:4, :], mod_ref[0, 4:5, :])
    _store_token_tiles(h2_ref, h2, ROW_TILE)
    hi = h2.astype(BF16)
    lo = (h2 - hi.astype(F32)).astype(BF16)
    logit_ref[...] = _dot(hi, wr_hi_ref[0]) + (_dot(lo, wr_hi_ref[0]) + _dot(hi, wr_lo_ref[0]))


def _merge(r, m, a, bg, x_all, mod, w_branch, w_out, norm2_g, wr_hi, wr_lo, layer, n_batch):
    n = x_all.shape[0]
    row = lambda i: (i, 0)

    def layer_block(w):
        return pl.BlockSpec((1,) + w.shape[1:], lambda i: (layer,) + (0,) * (w.ndim - 1))

    return pl.pallas_call(
        _merge_kernel,
        grid=(n // TM,),
        in_specs=[
            pl.BlockSpec((TM, WIDTH), row), pl.BlockSpec((TM, WIDTH), row), pl.BlockSpec((TM, WIDTH), row),
            pl.BlockSpec((TM, 3 * D_MODEL), row),
            pl.BlockSpec((TM, D_MODEL), row),
            pl.BlockSpec((1, 6, D_MODEL), _mod_row(n_batch)),
            layer_block(w_branch), layer_block(w_out),
            pl.BlockSpec((1, 1, D_MODEL), lambda i: (layer, 0, 0)),
            layer_block(wr_hi), layer_block(wr_lo),
        ],
        out_specs=[pl.BlockSpec((TM, D_MODEL), row), pl.BlockSpec((TM * ROW_TILE, LANES), row),
                   pl.BlockSpec((TM, LANES), row)],
        out_shape=[jax.ShapeDtypeStruct((n, D_MODEL), F32), jax.ShapeDtypeStruct((n * ROW_TILE, LANES), F32),
                   jax.ShapeDtypeStruct((n, LANES), F32)],
        compiler_params=_params(("parallel",), VMEM_LIMIT),
        name="merge",
    )(r, m, a, bg, x_all, mod, w_branch, w_out, norm2_g.reshape(-1, 1, D_MODEL), wr_hi, wr_lo)


ROUTE_E1, ROUTE_E2, ROUTE_RANK1, ROUTE_RANK2, ROUTE_W1, ROUTE_W2 = range(6)


def _route_kernel(logit_ref, route_ref, count_ref, cnt_ref):
    @pl.when(pl.program_id(0) == 0)
    def _():
        cnt_ref[...] = jnp.zeros_like(cnt_ref)

    lg = logit_ref[...]
    lane = lax.broadcasted_iota(jnp.int32, lg.shape, 1)
    lane_f = lane.astype(F32)

    def first_argmax(vals):
        top = jnp.max(vals, axis=-1, keepdims=True)
        idx = jnp.min(jnp.where(vals == top, lane_f, float(LANES)), axis=-1, keepdims=True)
        return top, idx

    group_logits = jnp.where(lane < N_GROUPS, lg, -jnp.inf)
    g_top, g_idx = first_argmax(group_logits)
    group_w = 1.0 / jnp.sum(jnp.exp(group_logits - g_top), axis=-1, keepdims=True)

    first = N_GROUPS + EXPERTS_PER_GROUP * g_idx
    in_group = (lane_f >= first) & (lane_f < first + EXPERTS_PER_GROUP)
    expert_logits = jnp.where(in_group, lg, -jnp.inf)
    v1, i1 = first_argmax(expert_logits)
    v2, i2 = first_argmax(jnp.where(lane_f == i1, -jnp.inf, expert_logits))
    t = jnp.exp(v2 - v1)
    w1 = group_w / (1.0 + t)
    w2 = group_w * t / (1.0 + t)

    oh1 = (lane_f == i1).astype(F32)
    oh2 = (lane_f == i2).astype(F32)
    both = oh1 + oh2
    rows = lg.shape[0]
    earlier = (lax.broadcasted_iota(jnp.int32, (rows, rows), 1)
               < lax.broadcasted_iota(jnp.int32, (rows, rows), 0)).astype(BF16)
    before = _dot(earlier, both.astype(BF16)) + cnt_ref[...]
    rank1 = jnp.sum(oh1 * before, axis=-1, keepdims=True)
    rank2 = jnp.sum(oh2 * before, axis=-1, keepdims=True)
    cnt_ref[...] += jnp.sum(both, axis=0, keepdims=True)
    count_ref[...] = jnp.broadcast_to(cnt_ref[...], count_ref.shape)

    out = jnp.zeros_like(lg)
    for slot, val in ((ROUTE_E1, i1 - N_GROUPS), (ROUTE_E2, i2 - N_GROUPS), (ROUTE_RANK1, rank1),
                      (ROUTE_RANK2, rank2), (ROUTE_W1, w1), (ROUTE_W2, w2)):
        out = jnp.where(lane == slot, val, out)
    route_ref[...] = out


def _route(logits):
    n = logits.shape[0]
    return pl.pallas_call(
        _route_kernel,
        grid=(n // ROUTE_TM,),
        in_specs=[pl.BlockSpec((ROUTE_TM, LANES), lambda i: (i, 0))],
        out_specs=[pl.BlockSpec((ROUTE_TM, LANES), lambda i: (i, 0)), pl.BlockSpec((8, LANES), lambda i: (0, 0))],
        out_shape=[jax.ShapeDtypeStruct((n, LANES), F32), jax.ShapeDtypeStruct((8, LANES), F32)],
        scratch_shapes=[pltpu.VMEM((1, LANES), F32)],
        compiler_params=_params(("arbitrary",)),
        name="route",
    )(logits)


GROUP = 8


def _dispatch_plan(route, counts):
    n = route.shape[0]
    counts = counts[0, N_GROUPS:N_GROUPS + N_EXPERTS].astype(jnp.int32)
    padded = (counts + EXPERT_BLOCK - 1) // EXPERT_BLOCK * EXPERT_BLOCK
    pad_end = jnp.cumsum(padded)
    pad_start = pad_end - padded
    e = route[:, ROUTE_E1:ROUTE_E2 + 1].astype(jnp.int32)
    rank = route[:, ROUTE_RANK1:ROUTE_RANK2 + 1].astype(jnp.int32)
    start_of = jnp.sum(jnp.where(e[..., None] == jnp.arange(N_EXPERTS), pad_start, 0), axis=-1)
    dest = (start_of + rank).reshape(-1)
    n_blocks = (2 * n + N_EXPERTS * (EXPERT_BLOCK - 1) + EXPERT_BLOCK - 1) // EXPERT_BLOCK
    n_rows = n_blocks * EXPERT_BLOCK
    assign = jnp.full((n_rows,), -1, jnp.int32).at[dest].set(jnp.arange(2 * n, dtype=jnp.int32))
    block = jnp.arange(n_blocks, dtype=jnp.int32)
    block_expert = jnp.minimum(jnp.sum(block[:, None] * EXPERT_BLOCK >= pad_end[None, :], axis=1), N_EXPERTS - 1)
    valid = jnp.clip(jnp.sum(jnp.where(block_expert[:, None] == jnp.arange(N_EXPERTS), pad_start + counts, 0), axis=1)
                     - block * EXPERT_BLOCK, 0, EXPERT_BLOCK)
    copies = (valid + GROUP - 1) // GROUP * GROUP
    used_blocks = (pad_end[-1] // EXPERT_BLOCK).reshape(1)
    row = jnp.arange(n_rows, dtype=jnp.int32)
    spare = 2 * n + ((row // EXPERT_BLOCK) % 2) * GROUP + row % GROUP
    src_token = jnp.where(assign < 0, 0, assign // 2).reshape(n_blocks, 1, EXPERT_BLOCK)
    dst_slot = jnp.where(assign < 0, spare, assign).reshape(n_blocks, 1, EXPERT_BLOCK)
    return (src_token, dst_slot, block_expert.astype(jnp.int32), copies.astype(jnp.int32),
            used_blocks.astype(jnp.int32))


def _tile_rows(i):
    return pl.ds(pl.multiple_of(i * ROW_TILE, ROW_TILE), ROW_TILE)


def _expert_kernel(be_ref, copies_ref, used_ref, src_ref, src_next_ref, dst_ref, h_ref, wg_ref, wu_ref, wd_ref,
                   slots_ref, x_buf, y_buf, wg_s, wu_s, wd_s, gather_sem, scatter_sem):
    i = pl.program_id(0)
    used = used_ref[0]
    cur = i % 2

    def gather_copy(idx_ref, r, buf):
        return pltpu.make_async_copy(h_ref.at[_tile_rows(idx_ref[0, 0, r])], x_buf.at[buf, _tile_rows(r)],
                                     gather_sem.at[buf])

    def scatter_copy(r, buf):
        return pltpu.make_async_copy(y_buf.at[buf, _tile_rows(r)], slots_ref.at[_tile_rows(dst_ref[0, 0, r])],
                                     scatter_sem.at[buf])

    def for_each_group(block, fn):
        def body(g, carry):
            for j in range(GROUP):
                fn(g * GROUP + j)
            return carry
        lax.fori_loop(0, copies_ref[block] // GROUP, body, 0)

    @pl.when(i == 0)
    def _():
        x_buf[...] = jnp.zeros_like(x_buf)
        spare_rows = 2 * GROUP * ROW_TILE
        zero_spare = pltpu.make_async_copy(x_buf.at[0, pl.ds(0, spare_rows)],
                                           slots_ref.at[pl.ds(slots_ref.shape[0] - spare_rows, spare_rows)],
                                           scatter_sem.at[0])
        zero_spare.start()
        zero_spare.wait()
        for_each_group(0, lambda r: gather_copy(src_ref, r, 0).start())

    @pl.when(i + 1 < used)
    def _():
        for_each_group(i + 1, lambda r: gather_copy(src_next_ref, r, 1 - cur).start())

    @pl.when(jnp.logical_or(i == 0, be_ref[i] != be_ref[jnp.maximum(i - 1, 0)]))
    def _():
        wg_s[...] = wg_ref[0, 0].astype(BF16)
        wu_s[...] = wu_ref[0, 0].astype(BF16)
        wd_s[...] = wd_ref[0, 0].astype(BF16)

    @pl.when(i < used)
    def _():
        for_each_group(i, lambda r: gather_copy(src_ref, 0, cur).wait())
        x = jnp.concatenate([_load_token_slab(x_buf.at[cur], EXPERT_BLOCK, ROW_TILE, c) for c in range(ROW_TILE)],
                            axis=1).astype(BF16)
        hidden = _silu(_dot(x, wg_s[...])) * _dot(x, wu_s[...])
        _store_token_tiles(y_buf.at[cur], _dot(hidden.astype(BF16), wd_s[...]), ROW_TILE)
        for_each_group(i, lambda r: scatter_copy(r, cur).start())

        @pl.when(i > 0)
        def _():
            for_each_group(i - 1, lambda r: scatter_copy(0, 1 - cur).wait())

        @pl.when(i == used - 1)
        def _():
            for_each_group(i, lambda r: scatter_copy(0, cur).wait())


def _experts(h2_tiles, plan, w_gate, w_up, w_down, layer):
    src_token, dst_slot, block_expert, copies, used_blocks = plan
    n_blocks = src_token.shape[0]
    n_slots = 2 * (h2_tiles.shape[0] // ROW_TILE) + 2 * GROUP
    last = n_blocks - 1

    def smem_block(index):
        return pl.BlockSpec((1, 1, EXPERT_BLOCK), index, memory_space=pltpu.SMEM)

    def weight(shape):
        return pl.BlockSpec((1, 1) + shape, lambda i, be, copies, used: (layer, be[i], 0, 0))

    grid_spec = pltpu.PrefetchScalarGridSpec(
        num_scalar_prefetch=3,
        grid=(n_blocks,),
        in_specs=[
            smem_block(lambda i, be, copies, used: (i, 0, 0)),
            smem_block(lambda i, be, copies, used: (jnp.minimum(i + 1, last), 0, 0)),
            smem_block(lambda i, be, copies, used: (i, 0, 0)),
            pl.BlockSpec(memory_space=pl.ANY),
            weight((D_MODEL, EXPERT_FF)), weight((D_MODEL, EXPERT_FF)), weight((EXPERT_FF, D_MODEL)),
        ],
        out_specs=pl.BlockSpec(memory_space=pl.ANY),
        scratch_shapes=[
            pltpu.VMEM((2, EXPERT_BLOCK * ROW_TILE, LANES), F32),
            pltpu.VMEM((2, EXPERT_BLOCK * ROW_TILE, LANES), F32),
            pltpu.VMEM((D_MODEL, EXPERT_FF), BF16),
            pltpu.VMEM((D_MODEL, EXPERT_FF), BF16),
            pltpu.VMEM((EXPERT_FF, D_MODEL), BF16),
            pltpu.SemaphoreType.DMA((2,)),
            pltpu.SemaphoreType.DMA((2,)),
        ],
    )
    return pl.pallas_call(
        _expert_kernel,
        grid_spec=grid_spec,
        out_shape=jax.ShapeDtypeStruct((n_slots * ROW_TILE, LANES), F32),
        compiler_params=_params(("arbitrary",), VMEM_LIMIT),
        name="experts",
    )(block_expert, copies, used_blocks, src_token, src_token, dst_slot, h2_tiles, w_gate, w_up, w_down)


def _combine_kernel(slots_ref, x_ref, route_ref, mod_ref, fg_ref, o_ref, *, final):
    route = route_ref[...]
    w1 = route[:, ROUTE_W1:ROUTE_W1 + 1]
    w2 = route[:, ROUTE_W2:ROUTE_W2 + 1]
    rows = x_ref.shape[0]
    sq = jnp.zeros((rows, 1), F32)
    for c in range(ROW_TILE):
        lanes = slice(c * LANES, (c + 1) * LANES)
        y1 = _load_token_slab(slots_ref, rows, 2 * ROW_TILE, c)
        y2 = _load_token_slab(slots_ref, rows, 2 * ROW_TILE, c, offset=ROW_TILE)
        x_new = x_ref[:, lanes] + mod_ref[0, 5:6, lanes] * (y1 * w1 + y2 * w2)
        o_ref[:, lanes] = x_new
        sq = sq + jnp.sum(x_new * x_new, axis=-1, keepdims=True)
    if final:
        o_ref[...] = o_ref[...] * lax.rsqrt(sq / D_MODEL + EPS) * fg_ref[...]


def _combine(slots, x_all, route, mod, final_g, n_batch, final):
    n = x_all.shape[0]
    row = lambda i: (i, 0)
    return pl.pallas_call(
        functools.partial(_combine_kernel, final=final),
        grid=(n // TM,),
        in_specs=[
            pl.BlockSpec((TM * 2 * ROW_TILE, LANES), row),
            pl.BlockSpec((TM, D_MODEL), row),
            pl.BlockSpec((TM, LANES), row),
            pl.BlockSpec((1, 6, D_MODEL), _mod_row(n_batch)),
            pl.BlockSpec((1, D_MODEL), lambda i: (0, 0)),
        ],
        out_specs=pl.BlockSpec((TM, D_MODEL), row),
        out_shape=jax.ShapeDtypeStruct((n, D_MODEL), F32),
        compiler_params=_params(("parallel",)),
        name="combine",
    )(slots, x_all, route, mod, final_g.reshape(1, D_MODEL))


def _rope_tables():
    t = jnp.arange(SEQ)
    rows = (t // GRID_W).astype(F32)
    cols = (t % GRID_W).astype(F32)
    n_freq = HEAD_DIM // 4
    inv_freq = ROPE_BASE ** (-jnp.arange(n_freq, dtype=F32) / n_freq)
    ang_r = rows[:, None] * inv_freq
    ang_c = cols[:, None] * inv_freq
    cos = jnp.concatenate([jnp.cos(ang_r)] * 2 + [jnp.cos(ang_c)] * 2, axis=1)
    sin = jnp.concatenate([-jnp.sin(ang_r), jnp.sin(ang_r), -jnp.sin(ang_c), jnp.sin(ang_c)], axis=1)
    return cos, sin


def kernel(x, c, ctx, c_ctx, w_mod, b_mod, norm1_g, norm2_g, w_in, ret_decay, ret_norm_g, conv_w, conv_b,
           mlstm_gate_b, mlstm_norm_g, na_rpb, w_branch, w_out, w_group, w_router, w_expert_gate,
           w_expert_up, w_expert_down, final_norm_g):
    n_batch, seq, d = x.shape
    depth = w_mod.shape[0]
    assert (seq, d, ctx.shape[1]) == (SEQ, D_MODEL, CTX_LEN)
    n = n_batch * TOK

    cond_rows = -(-(n_batch + 1) // 8) * 8
    cond = jnp.zeros((cond_rows, d), F32).at[:n_batch].set(c).at[n_batch].set(c_ctx)
    mod_all = _modulation(cond, w_mod, b_mod).reshape(depth, cond_rows, 6, d)

    cos, sin = _rope_tables()
    x_all = jnp.concatenate([ctx, x], axis=1).reshape(n, d)

    w_packed = _pack_in_weights(w_in)
    w_branch_b = w_branch.astype(BF16)
    w_out_b = w_out.astype(BF16)
    w_route = jnp.concatenate([w_group, w_router], axis=-1)
    w_route = jnp.pad(w_route, ((0, 0), (0, 0), (0, LANES - w_route.shape[-1])))
    wr_hi = w_route.astype(BF16)
    wr_lo = (w_route - wr_hi.astype(F32)).astype(BF16)
    gate_b = jnp.pad(mlstm_gate_b.reshape(depth, 1, 4 * N_HEADS), ((0, 0), (0, 0), (0, LANES - 4 * N_HEADS)))
    na_bias = _na_bias_tables(na_rpb)

    for layer in range(depth):
        mod = mod_all[layer]
        ret, ml, gates, na, bg = _in_projection(x_all, mod, norm1_g, w_packed, layer, n_batch)
        r_out = _retention(ret, ret_decay[layer], ret_norm_g[layer], cos, sin, n_batch)
        m_out = _mlstm(ml, gates, gate_b[layer], conv_w[layer], conv_b[layer].reshape(1, 2 * WIDTH),
                       mlstm_norm_g[layer], n_batch)
        a_out = _neighbourhood_attention(na, na_bias, layer, n_batch)
        x_all, h2, logits = _merge(r_out, m_out, a_out, bg, x_all, mod, w_branch_b, w_out_b, norm2_g,
                                   wr_hi, wr_lo, layer, n_batch)
        route, counts = _route(logits)
        plan = _dispatch_plan(route, counts)
        slots = _experts(h2, plan, w_expert_gate, w_expert_up, w_expert_down, layer)
        x_all = _combine(slots, x_all, route, mod, final_norm_g, n_batch, final=layer == depth - 1)

    return x_all.reshape(n_batch, TOK, d)[:, CTX_LEN:]
```

```python
import functools

import numpy as np
import jax
import jax.numpy as jnp
from jax import lax
from jax.experimental import pallas as pl
from jax.experimental.pallas import tpu as pltpu

F32 = jnp.float32
BF16 = jnp.bfloat16

D_MODEL = 1024
SEQ = 2048
CTX_LEN = 256
TOK = CTX_LEN + SEQ
GRID_W = 64
GRID_ROWS = SEQ // GRID_W
HEAD_DIM = 128
N_HEADS = 4
WIDTH = N_HEADS * HEAD_DIM
CHUNK = 256
N_CHUNKS = TOK // CHUNK
CTX_CHUNKS = CTX_LEN // CHUNK
CONV_WIDTH = 5
NA_WIN_ROWS = 8
NA_WIN_COLS = 16
NA_Q_ROWS = 4
NA_Q = NA_Q_ROWS * GRID_W
NA_K_ROWS = NA_Q_ROWS + NA_WIN_ROWS
NA_K = NA_K_ROWS * GRID_W
NA_STEPS = SEQ // NA_Q
ROPE_BASE = 10000.0
N_GROUPS = 4
EXPERTS_PER_GROUP = 8
N_EXPERTS = N_GROUPS * EXPERTS_PER_GROUP
EXPERT_FF = 512
EXPERT_BLOCK = 256
EPS = 1e-6
NEG_INF = -1e30
QK_SCALE = HEAD_DIM ** -0.5

TM = 256
ROUTE_TM = 512
LANES = 128
VMEM_LIMIT = 56 * 1024 * 1024


def _dot(a, b):
    return jnp.dot(a, b, preferred_element_type=F32)


def _dot_nt(a, b):
    return lax.dot_general(a, b, (((1,), (1,)), ((), ())), preferred_element_type=F32)


def _bdot(a, b):
    return _dot(a.astype(BF16), b.astype(BF16))


def _bdot_nt(a, b):
    return _dot_nt(a.astype(BF16), b.astype(BF16))


def _bdot_tn(a, b):
    return _dot(a.T.astype(BF16), b.astype(BF16))


def _log_sigmoid(x):
    return jnp.minimum(x, 0.0) - jnp.log1p(jnp.exp(-jnp.abs(x)))


def _silu(x):
    return x * jax.nn.sigmoid(x)


def _params(sem, vmem=None):
    return pltpu.CompilerParams(dimension_semantics=sem, vmem_limit_bytes=vmem)


def _mod_row(n_batch):
    tiles = TOK // TM

    def index(i):
        return (jnp.where(i % tiles == 0, n_batch, i // tiles), 0, 0)

    return index


def _mod_kernel(c_ref, w_ref, b_ref, o_ref):
    cond = _silu(c_ref[...])
    o_ref[0] = _bdot(cond, w_ref[0]) + b_ref[0]


def _modulation(cond, w_mod, b_mod):
    depth, d, cols = w_mod.shape
    rows = cond.shape[0]
    tn = 1536
    return pl.pallas_call(
        _mod_kernel,
        grid=(depth, cols // tn),
        in_specs=[
            pl.BlockSpec((rows, d), lambda l, j: (0, 0)),
            pl.BlockSpec((1, d, tn), lambda l, j: (l, 0, j)),
            pl.BlockSpec((1, 1, tn), lambda l, j: (l, 0, j)),
        ],
        out_specs=pl.BlockSpec((1, rows, tn), lambda l, j: (l, 0, j)),
        out_shape=jax.ShapeDtypeStruct((depth, rows, cols), F32),
        compiler_params=_params(("parallel", "parallel")),
        name="modulation",
    )(cond, w_mod, b_mod.reshape(depth, 1, cols))


def _rms_modulate(x, g, shift, scale):
    y = x * lax.rsqrt(jnp.mean(x * x, axis=-1, keepdims=True) + EPS) * g
    return y * (1.0 + scale) + shift


IN_WIDTHS = (4 * WIDTH, 4 * WIDTH, LANES, 3 * WIDTH, 3 * D_MODEL)
IN_OFFSETS = tuple(int(v) for v in np.cumsum((0,) + IN_WIDTHS))


def _pack_in_weights(w_in):
    g1 = 2 * 4 * WIDTH + 4 * N_HEADS
    lane_pad = jnp.zeros(w_in.shape[:-1] + (LANES - 4 * N_HEADS,), w_in.dtype)
    return jnp.concatenate([w_in[..., :g1], lane_pad, w_in[..., g1:]], axis=-1).astype(BF16)


def _inproj_kernel(x_ref, mod_ref, g_ref, w_ref, *out_refs):
    h = _rms_modulate(x_ref[...], g_ref[0], mod_ref[0, 0:1, :], mod_ref[0, 1:2, :]).astype(BF16)
    for o_ref, lo, hi in zip(out_refs, IN_OFFSETS[:-1], IN_OFFSETS[1:]):
        o_ref[...] = _dot(h, w_ref[0, :, lo:hi])


def _in_projection(x_all, mod, norm_g, w_packed, layer, n_batch):
    n = x_all.shape[0]
    row = lambda i: (i, 0)
    return pl.pallas_call(
        _inproj_kernel,
        grid=(n // TM,),
        in_specs=[
            pl.BlockSpec((TM, D_MODEL), row),
            pl.BlockSpec((1, 6, D_MODEL), _mod_row(n_batch)),
            pl.BlockSpec((1, 1, D_MODEL), lambda i: (layer, 0, 0)),
            pl.BlockSpec((1,) + w_packed.shape[1:], lambda i: (layer, 0, 0), pipeline_mode=pl.Buffered(1)),
        ],
        out_specs=[pl.BlockSpec((TM, w), row) for w in IN_WIDTHS],
        out_shape=[jax.ShapeDtypeStruct((n, w), F32) for w in IN_WIDTHS],
        compiler_params=_params(("parallel",), VMEM_LIMIT),
        name="in_projection",
    )(x_all, mod, norm_g.reshape(-1, 1, D_MODEL), w_packed)


def _chunk_order(t):
    fwd = t
    bwd = jnp.where(t < CTX_CHUNKS, CTX_CHUNKS - 1 - t, N_CHUNKS + CTX_CHUNKS - 1 - t)
    return fwd, bwd


def _chunk_slice(c):
    return pl.ds(pl.multiple_of(c * CHUNK, CHUNK), CHUNK)


def _head_norm(y, gain):
    mu = jnp.mean(y, axis=-1, keepdims=True)
    yc = y - mu
    var = jnp.mean(yc * yc, axis=-1, keepdims=True)
    return yc * lax.rsqrt(var + EPS) * gain


ROW_TILE = D_MODEL // LANES


def _store_token_tiles(ref, x, pitch, offset=0):
    rows = x.shape[0]
    for c in range(ROW_TILE):
        ref[pl.ds(offset + c, rows, stride=pitch), :] = x[:, c * LANES:(c + 1) * LANES]


def _load_token_slab(ref, rows, pitch, c, offset=0):
    return ref[pl.ds(offset + c, rows, stride=pitch), :]


def _select_lane(x, lane, idx):
    return jnp.sum(jnp.where(lane == idx, x, 0.0), axis=-1, keepdims=True)


HEADS_PER_STEP = 2
HEAD_STEPS = N_HEADS // HEADS_PER_STEP
STEP_WIDTH = HEADS_PER_STEP * HEAD_DIM
SCAN_VMEM = 48 * 1024 * 1024


def _head_lanes(hh):
    return slice(hh * HEAD_DIM, (hh + 1) * HEAD_DIM)


def _retention_kernel(dec_ref, q_ref, k_ref, v_ref, g_ref, cos_ref, sin_ref, gn_ref, o_ref,
                      qs_ref, ks_ref, ob_ref, intra_ref, st_ref):
    first_head = pl.program_id(1) * HEADS_PER_STEP

    lane = lax.broadcasted_iota(jnp.int32, (CHUNK, HEAD_DIM), 1)
    first_half = (lane % (HEAD_DIM // 2)) < (HEAD_DIM // 4)

    def rope(x, cos, sin):
        rot = jnp.where(first_half, pltpu.roll(x, HEAD_DIM - HEAD_DIM // 4, 1), pltpu.roll(x, HEAD_DIM // 4, 1))
        return x * cos + rot * sin

    qs_ref[0:CTX_LEN, :] = q_ref[0:CTX_LEN, :]
    ks_ref[0:CTX_LEN, :] = k_ref[0:CTX_LEN, :] * QK_SCALE

    def rotate_chunk(c, carry):
        rows = _chunk_slice(c)
        pos = _chunk_slice(c - CTX_CHUNKS)
        cos = cos_ref[pos, :]
        sin = sin_ref[pos, :]
        for hh in range(HEADS_PER_STEP):
            qs_ref[rows, _head_lanes(hh)] = rope(q_ref[rows, _head_lanes(hh)], cos, sin)
            ks_ref[rows, _head_lanes(hh)] = rope(k_ref[rows, _head_lanes(hh)], cos, sin) * QK_SCALE
        return carry

    lax.fori_loop(CTX_CHUNKS, N_CHUNKS, rotate_chunk, 0)

    ii = lax.broadcasted_iota(jnp.int32, (CHUNK, CHUNK), 0).astype(F32)
    jj = lax.broadcasted_iota(jnp.int32, (CHUNK, CHUNK), 1).astype(F32)
    col = lax.broadcasted_iota(jnp.int32, (CHUNK, 1), 0).astype(F32)

    def decay_mat(dist, lg_dir):
        ok = dist >= 0
        return jnp.where(ok, jnp.exp(jnp.where(ok, dist, 0.0) * lg_dir), 0.0)

    dec = _log_sigmoid(dec_ref[...])
    hl = lax.broadcasted_iota(jnp.int32, dec.shape, 1)
    consts = []
    for hh in range(HEADS_PER_STEP):
        lg = jnp.sum(jnp.where(hl == first_head + hh, dec, 0.0), axis=-1, keepdims=True)
        lg_f, lg_b = lg[0:1, :], lg[1:2, :]
        intra_ref[2 * hh] = decay_mat(ii - jj, lg_f)
        intra_ref[2 * hh + 1] = decay_mat(jj - ii, lg_b)
        consts.append(dict(
            q_decay=(jnp.exp((col + 1.0) * lg_f), jnp.exp((CHUNK - col) * lg_b)),
            k_decay=(jnp.exp((CHUNK - 1.0 - col) * lg_f), jnp.exp(col * lg_b)),
            chunk_decay=(jnp.exp(CHUNK * lg_f), jnp.exp(CHUNK * lg_b))))

    st_ref[...] = jnp.zeros_like(st_ref)
    out_refs = (o_ref, ob_ref)

    def step(t, carry):
        for hh in range(HEADS_PER_STEP):
            cst = consts[hh]
            for d, c in enumerate(_chunk_order(t)):
                rows = _chunk_slice(c)
                q = qs_ref[rows, _head_lanes(hh)]
                k = ks_ref[rows, _head_lanes(hh)]
                v = v_ref[rows, _head_lanes(hh)]
                s_prev = st_ref[2 * hh + d]
                scores = _bdot_nt(q, k) * intra_ref[2 * hh + d]
                out_refs[d][rows, _head_lanes(hh)] = _bdot(scores, v) + _bdot(q * cst["q_decay"][d], s_prev)
                st_ref[2 * hh + d] = s_prev * cst["chunk_decay"][d] + _bdot_tn(k * cst["k_decay"][d], v)
        return carry

    lax.fori_loop(0, N_CHUNKS, step, 0)

    def finish_chunk(c, carry):
        rows = _chunk_slice(c)
        for hh in range(HEADS_PER_STEP):
            lanes = _head_lanes(hh)
            y = o_ref[rows, lanes] + ob_ref[rows, lanes]
            o_ref[rows, lanes] = _head_norm(y, gn_ref[:, lanes]) * _silu(g_ref[rows, lanes])
        return carry

    lax.fori_loop(0, N_CHUNKS, finish_chunk, 0)


def _retention(ret, ret_decay, norm_g, cos, sin, n_batch):
    n = ret.shape[0]

    def head_block(offset):
        return pl.BlockSpec((TOK, STEP_WIDTH), lambda b, h: (b, offset * HEAD_STEPS + h))

    return pl.pallas_call(
        _retention_kernel,
        grid=(n_batch, HEAD_STEPS),
        in_specs=[
            pl.BlockSpec(ret_decay.shape, lambda b, h: (0, 0)),
            head_block(0), head_block(1), head_block(2), head_block(3),
            pl.BlockSpec((SEQ, HEAD_DIM), lambda b, h: (0, 0)),
            pl.BlockSpec((SEQ, HEAD_DIM), lambda b, h: (0, 0)),
            pl.BlockSpec((1, STEP_WIDTH), lambda b, h: (0, h)),
        ],
        out_specs=pl.BlockSpec((TOK, STEP_WIDTH), lambda b, h: (b, h)),
        out_shape=jax.ShapeDtypeStruct((n, WIDTH), F32),
        scratch_shapes=[
            pltpu.VMEM((TOK, STEP_WIDTH), F32),
            pltpu.VMEM((TOK, STEP_WIDTH), F32),
            pltpu.VMEM((TOK, STEP_WIDTH), F32),
            pltpu.VMEM((2 * HEADS_PER_STEP, CHUNK, CHUNK), F32),
            pltpu.VMEM((2 * HEADS_PER_STEP, HEAD_DIM, HEAD_DIM), F32),
        ],
        compiler_params=_params(("parallel", "parallel"), SCAN_VMEM),
        name="retention",
    )(ret_decay, ret, ret, ret, ret, cos, sin, norm_g.reshape(1, WIDTH))


CONV_PAD = 8
CONV_ROWS = 128


def _mlstm_kernel(q_ref, k_ref, v_ref, og_ref, gate_ref, gb_ref, cwq_ref, cwk_ref, cbq_ref, cbk_ref,
                  gn_ref, o_ref, qs_ref, ks_ref, ob_ref, pad_ref, gx_ref, gxt_ref, cst_ref):
    first_head = pl.program_id(1) * HEADS_PER_STEP

    pad_ref[0:CONV_PAD, :] = jnp.zeros((CONV_PAD, STEP_WIDTH), F32)
    pad_ref[CONV_PAD + TOK:, :] = jnp.zeros((CONV_PAD, STEP_WIDTH), F32)
    crow = lax.broadcasted_iota(jnp.int32, (CONV_ROWS, 1), 0)

    def conv(u_ref, w_ref, b_ref, dst_ref, scale):
        def fill(c, carry):
            rows = _chunk_slice(c)
            pad_ref[pl.ds(pl.multiple_of(c * CHUNK + CONV_PAD, CONV_PAD), CHUNK), :] = u_ref[rows, :]
            return carry

        lax.fori_loop(0, N_CHUNKS, fill, 0)

        for blk in range(TOK // CONV_ROWS):
            first = blk * CONV_ROWS
            acc = jnp.zeros((CONV_ROWS, STEP_WIDTH), F32)
            for j in range(CONV_WIDTH):
                shift = j - CONV_WIDTH // 2
                tap = pad_ref[first + CONV_PAD + shift:first + CONV_PAD + shift + CONV_ROWS, :]
                if (first + CONV_ROWS == CTX_LEN and shift > 0) or (first == CTX_LEN and shift < 0):
                    trow = crow + first
                    tap = jnp.where((trow < CTX_LEN) == (trow + shift < CTX_LEN), tap, 0.0)
                acc = acc + tap * w_ref[j:j + 1, :]
            dst_ref[first:first + CONV_ROWS, :] = _silu(acc + b_ref[...]) * scale

    conv(q_ref, cwq_ref, cbq_ref, qs_ref, 1.0)
    conv(k_ref, cwk_ref, cbk_ref, ks_ref, QK_SCALE)

    clane = lax.broadcasted_iota(jnp.int32, (CHUNK, LANES), 1)
    is_forget = ((clane // N_HEADS) % 2) == 1

    def gate_chunk(c, carry):
        rows = _chunk_slice(c)
        g = gate_ref[rows, :] + gb_ref[...]
        gx = jnp.where(is_forget, _log_sigmoid(g), g)
        gx_ref[rows, :] = gx
        gxt_ref[c] = gx.T
        return carry

    lax.fori_loop(0, N_CHUNKS, gate_chunk, 0)

    ii = lax.broadcasted_iota(jnp.int32, (CHUNK, CHUNK), 0)
    jj = lax.broadcasted_iota(jnp.int32, (CHUNK, CHUNK), 1)
    causal = (jj <= ii, jj >= ii)

    cst_ref[...] = jnp.zeros_like(cst_ref)
    out_refs = (o_ref, ob_ref)

    def step(t, carry):
        new_carry = []
        for hh in range(HEADS_PER_STEP):
            head = first_head + hh
            lanes = _head_lanes(hh)
            for d, c in enumerate(_chunk_order(t)):
                slot = 2 * hh + d
                n_prev, m_prev = carry[2 * slot], carry[2 * slot + 1]
                rows = _chunk_slice(c)
                q = qs_ref[rows, lanes]
                k = ks_ref[rows, lanes]
                v = v_ref[rows, lanes]
                gc = gx_ref[rows, :]
                i_col = _select_lane(gc, clane, 2 * d * N_HEADS + head)
                f_col = _select_lane(gc, clane, (2 * d + 1) * N_HEADS + head)
                i_row = gxt_ref[c, pl.ds(2 * d * N_HEADS + head, 1), :]
                f_row = gxt_ref[c, pl.ds((2 * d + 1) * N_HEADS + head, 1), :]
                vis = causal[d]
                cum_col = jnp.sum(jnp.where(vis, f_row, 0.0), axis=1, keepdims=True)
                cum_row = jnp.sum(jnp.where(causal[1 - d], f_col, 0.0), axis=0, keepdims=True)
                total = jnp.sum(f_row, axis=1, keepdims=True)
                c_prev = cst_ref[slot]

                log_kw = total - cum_col + i_col
                m_new = jnp.maximum(total + m_prev, jnp.max(log_kw, axis=0, keepdims=True))
                kw = jnp.exp(log_kw - m_new)
                pw = jnp.exp(total + m_prev - m_new)
                cst_ref[slot] = pw * c_prev + _bdot_tn(k * kw, v)
                n_new = pw * n_prev + jnp.sum(kw * k, axis=0, keepdims=True)

                log_w = jnp.where(vis, cum_col - cum_row + i_row, -jnp.inf)
                log_p = cum_col + m_prev
                m_t = jnp.maximum(log_p, jnp.max(log_w, axis=1, keepdims=True))
                w = jnp.exp(log_w - m_t)
                p = jnp.exp(log_p - m_t)
                qk = _bdot_nt(q, k) * w
                num = _bdot(qk, v) + p * _bdot(q, c_prev)
                den = jnp.sum(qk, axis=1, keepdims=True) + p * jnp.sum(q * n_prev, axis=1, keepdims=True)
                out_refs[d][rows, lanes] = num / jnp.maximum(jnp.abs(den), jnp.exp(-m_t))
                new_carry += [n_new, m_new]
        return tuple(new_carry)

    zero_n = jnp.zeros((1, HEAD_DIM), F32)
    zero_m = jnp.zeros((1, 1), F32)
    lax.fori_loop(0, N_CHUNKS, step, (zero_n, zero_m) * (2 * HEADS_PER_STEP))

    def finish_chunk(c, carry):
        rows = _chunk_slice(c)
        for hh in range(HEADS_PER_STEP):
            lanes = _head_lanes(hh)
            gated = (o_ref[rows, lanes] + ob_ref[rows, lanes]) * jax.nn.sigmoid(og_ref[rows, lanes])
            o_ref[rows, lanes] = _head_norm(gated, gn_ref[:, lanes])
        return carry

    lax.fori_loop(0, N_CHUNKS, finish_chunk, 0)


def _mlstm(ml, gates, gate_b, conv_w, conv_b, norm_g, n_batch):
    n = ml.shape[0]

    def head_block(offset):
        return pl.BlockSpec((TOK, STEP_WIDTH), lambda b, h: (b, offset * HEAD_STEPS + h))

    def head_cols(rows, offset):
        return pl.BlockSpec((rows, STEP_WIDTH), lambda b, h: (0, offset * HEAD_STEPS + h))

    return pl.pallas_call(
        _mlstm_kernel,
        grid=(n_batch, HEAD_STEPS),
        in_specs=[
            head_block(0), head_block(1), head_block(2), head_block(3),
            pl.BlockSpec((TOK, LANES), lambda b, h: (b, 0)),
            pl.BlockSpec((1, LANES), lambda b, h: (0, 0)),
            head_cols(CONV_WIDTH, 0), head_cols(CONV_WIDTH, 1),
            head_cols(1, 0), head_cols(1, 1),
            head_cols(1, 0),
        ],
        out_specs=pl.BlockSpec((TOK, STEP_WIDTH), lambda b, h: (b, h)),
        out_shape=jax.ShapeDtypeStruct((n, WIDTH), F32),
        scratch_shapes=[
            pltpu.VMEM((TOK, STEP_WIDTH), F32),
            pltpu.VMEM((TOK, STEP_WIDTH), F32),
            pltpu.VMEM((TOK, STEP_WIDTH), F32),
            pltpu.VMEM((TOK + 2 * CONV_PAD, STEP_WIDTH), F32),
            pltpu.VMEM((TOK, LANES), F32),
            pltpu.VMEM((N_CHUNKS, LANES, CHUNK), F32),
            pltpu.VMEM((2 * HEADS_PER_STEP, HEAD_DIM, HEAD_DIM), F32),
        ],
        compiler_params=_params(("parallel", "parallel"), SCAN_VMEM),
        name="mlstm",
    )(ml, ml, ml, ml, gates, gate_b, conv_w, conv_w, conv_b, conv_b, norm_g.reshape(1, WIDTH))


def _na_kernel(q_ref, k_ref, v_ref, bias_ref, o_ref):
    step = pl.program_id(2)

    def context_scores(hh):
        lanes = _head_lanes(hh)
        q = q_ref[:, lanes].astype(BF16)
        v_ctx = v_ref[0:CTX_LEN, lanes].astype(BF16)
        return q, v_ctx, _dot_nt(q, k_ref[0:CTX_LEN, lanes].astype(BF16)) * QK_SCALE

    @pl.when(step == 0)
    def _():
        for hh in range(HEADS_PER_STEP):
            _, v_ctx, s_ctx = context_scores(hh)
            m = jnp.max(s_ctx, axis=-1, keepdims=True)
            p = jnp.exp(s_ctx - m)
            o_ref[:, _head_lanes(hh)] = _dot(p.astype(BF16), v_ctx) / jnp.sum(p, axis=-1, keepdims=True)

    @pl.when(step > 0)
    def _():
        start = CTX_LEN + NA_Q * jnp.clip(step - 2, 0, NA_STEPS - NA_K_ROWS // NA_Q_ROWS)
        rows = pl.ds(pl.multiple_of(start, NA_Q), NA_K)
        for hh in range(HEADS_PER_STEP):
            lanes = _head_lanes(hh)
            q, v_ctx, s_ctx = context_scores(hh)
            s_loc = _dot_nt(q, k_ref[rows, lanes].astype(BF16)) * QK_SCALE + bias_ref[0, hh, 0]
            m = jnp.maximum(jnp.max(s_loc, axis=-1, keepdims=True), jnp.max(s_ctx, axis=-1, keepdims=True))
            p_loc = jnp.exp(s_loc - m)
            p_ctx = jnp.exp(s_ctx - m)
            denom = jnp.sum(p_loc, axis=-1, keepdims=True) + jnp.sum(p_ctx, axis=-1, keepdims=True)
            o = _dot(p_loc.astype(BF16), v_ref[rows, lanes].astype(BF16)) + _dot(p_ctx.astype(BF16), v_ctx)
            o_ref[:, lanes] = o / denom


def _na_bias_tables(rpb):
    lead = rpb.shape[:-2]
    n_row_off, n_col_off = 2 * NA_WIN_ROWS - 1, 2 * NA_WIN_COLS - 1
    qc = np.arange(GRID_W)[:, None]
    kc = np.arange(GRID_W)[None, :]
    col_start = np.clip(qc - NA_WIN_COLS // 2, 0, GRID_W - NA_WIN_COLS)
    col_ok = (kc >= col_start) & (kc < col_start + NA_WIN_COLS)
    col_idx = np.clip(kc - qc + NA_WIN_COLS - 1, 0, n_col_off - 1)
    onehot = ((col_idx[None] == np.arange(n_col_off)[:, None, None]) & col_ok[None]).astype(np.float32)
    slabs = jnp.einsum('...rc,cqk->...rqk', rpb.astype(F32), onehot, precision=lax.Precision.HIGHEST)
    slabs = jnp.where(col_ok, slabs, NEG_INF)
    pad = [(0, 0)] * len(lead) + [(NA_K_ROWS, NA_K_ROWS), (0, 0), (0, 0)]
    slabs = jnp.pad(slabs, pad, constant_values=NEG_INF)

    t = np.arange(NA_K_ROWS)
    last_start = NA_K_ROWS - NA_WIN_ROWS
    tables = []
    for case in range(3):
        per_query_row = []
        for a in range(NA_Q_ROWS):
            row_ok, first = [
                (t < NA_WIN_ROWS, NA_WIN_ROWS - 1 - a),
                ((t >= a) & (t < a + NA_WIN_ROWS), NA_WIN_ROWS // 2 - 1 - a),
                (t >= last_start, NA_Q_ROWS - NA_K_ROWS + NA_WIN_ROWS - 1 - a),
            ][case]
            rows = slabs[..., NA_K_ROWS + first:NA_K_ROWS + first + NA_K_ROWS, :, :]
            rows = jnp.where(row_ok[:, None, None], rows, NEG_INF)
            per_query_row.append(jnp.swapaxes(rows, -3, -2))
        tables.append(jnp.stack(per_query_row, axis=-4).reshape(*lead, NA_Q, NA_K))
    return jnp.stack(tables, axis=-3)


def _neighbourhood_attention(na, bias, layer, n_batch):
    n = na.shape[0]
    steps = 1 + NA_STEPS
    tiles = TOK // NA_Q

    def table(b, h, j):
        return (layer, h, jnp.where(j <= 1, 0, jnp.where(j == NA_STEPS, 2, 1)), 0, 0)

    return pl.pallas_call(
        _na_kernel,
        grid=(n_batch, HEAD_STEPS, steps),
        in_specs=[
            pl.BlockSpec((NA_Q, STEP_WIDTH), lambda b, h, j: (b * tiles + j, h)),
            pl.BlockSpec((TOK, STEP_WIDTH), lambda b, h, j: (b, HEAD_STEPS + h)),
            pl.BlockSpec((TOK, STEP_WIDTH), lambda b, h, j: (b, 2 * HEAD_STEPS + h)),
            pl.BlockSpec((1, HEADS_PER_STEP, 1, NA_Q, NA_K), table),
        ],
        out_specs=pl.BlockSpec((NA_Q, STEP_WIDTH), lambda b, h, j: (b * tiles + j, h)),
        out_shape=jax.ShapeDtypeStruct((n, WIDTH), F32),
        compiler_params=_params(("parallel", "parallel", "arbitrary")),
        name="neighbourhood_attention",
    )(na, na, na, bias)


def _merge_kernel(r_ref, m_ref, a_ref, bg_ref, x_ref, mod_ref, wb_ref, wo_ref, g2_ref, wr_hi_ref, wr_lo_ref,
                  x_out_ref, h2_ref, logit_ref):
    gate = jax.nn.sigmoid(bg_ref[...])
    mix = (gate[:, 0:D_MODEL] * _dot(r_ref[...].astype(BF16), wb_ref[0, 0])
           + gate[:, D_MODEL:2 * D_MODEL] * _dot(m_ref[...].astype(BF16), wb_ref[0, 1])
           + gate[:, 2 * D_MODEL:] * _dot(a_ref[...].astype(BF16), wb_ref[0, 2]))
    y = _dot(mix.astype(BF16), wo_ref[0])
    x_new = x_ref[...] + mod_ref[0, 2:3, :] * y
    x_out_ref[...] = x_new
    h2 = _rms_modulate(x_new, g2_ref[0], mod_ref[0, 3:4, :], mod_ref[0, 4:5, :])
    _store_token_tiles(h2_ref, h2, ROW_TILE)
    hi = h2.astype(BF16)
    lo = (h2 - hi.astype(F32)).astype(BF16)
    logit_ref[...] = _dot(hi, wr_hi_ref[0]) + (_dot(lo, wr_hi_ref[0]) + _dot(hi, wr_lo_ref[0]))


def _merge(r, m, a, bg, x_all, mod, w_branch, w_out, norm2_g, wr_hi, wr_lo, layer, n_batch):
    n = x_all.shape[0]
    row = lambda i: (i, 0)

    def layer_block(w):
        return pl.BlockSpec((1,) + w.shape[1:], lambda i: (layer,) + (0,) * (w.ndim - 1))

    return pl.pallas_call(
        _merge_kernel,
        grid=(n // TM,),
        in_specs=[
            pl.BlockSpec((TM, WIDTH), row), pl.BlockSpec((TM, WIDTH), row), pl.BlockSpec((TM, WIDTH), row),
            pl.BlockSpec((TM, 3 * D_MODEL), row),
            pl.BlockSpec((TM, D_MODEL), row),
            pl.BlockSpec((1, 6, D_MODEL), _mod_row(n_batch)),
            layer_block(w_branch), layer_block(w_out),
            pl.BlockSpec((1, 1, D_MODEL), lambda i: (layer, 0, 0)),
            layer_block(wr_hi), layer_block(wr_lo),
        ],
        out_specs=[pl.BlockSpec((TM, D_MODEL), row), pl.BlockSpec((TM * ROW_TILE, LANES), row),
                   pl.BlockSpec((TM, LANES), row)],
        out_shape=[jax.ShapeDtypeStruct((n, D_MODEL), F32), jax.ShapeDtypeStruct((n * ROW_TILE, LANES), F32),
                   jax.ShapeDtypeStruct((n, LANES), F32)],
        compiler_params=_params(("parallel",), VMEM_LIMIT),
        name="merge",
    )(r, m, a, bg, x_all, mod, w_branch, w_out, norm2_g.reshape(-1, 1, D_MODEL), wr_hi, wr_lo)


ROUTE_E1, ROUTE_E2, ROUTE_RANK1, ROUTE_RANK2, ROUTE_W1, ROUTE_W2 = range(6)


def _route_kernel(logit_ref, route_ref, count_ref, cnt_ref):
    @pl.when(pl.program_id(0) == 0)
    def _():
        cnt_ref[...] = jnp.zeros_like(cnt_ref)

    lg = logit_ref[...]
    lane = lax.broadcasted_iota(jnp.int32, lg.shape, 1)
    lane_f = lane.astype(F32)

    def first_argmax(vals):
        top = jnp.max(vals, axis=-1, keepdims=True)
        idx = jnp.min(jnp.where(vals == top, lane_f, float(LANES)), axis=-1, keepdims=True)
        return top, idx

    group_logits = jnp.where(lane < N_GROUPS, lg, -jnp.inf)
    g_top, g_idx = first_argmax(group_logits)
    group_w = 1.0 / jnp.sum(jnp.exp(group_logits - g_top), axis=-1, keepdims=True)

    first = N_GROUPS + EXPERTS_PER_GROUP * g_idx
    in_group = (lane_f >= first) & (lane_f < first + EXPERTS_PER_GROUP)
    expert_logits = jnp.where(in_group, lg, -jnp.inf)
    v1, i1 = first_argmax(expert_logits)
    v2, i2 = first_argmax(jnp.where(lane_f == i1, -jnp.inf, expert_logits))
    t = jnp.exp(v2 - v1)
    w1 = group_w / (1.0 + t)
    w2 = group_w * t / (1.0 + t)

    oh1 = (lane_f == i1).astype(F32)
    oh2 = (lane_f == i2).astype(F32)
    both = oh1 + oh2
    rows = lg.shape[0]
    earlier = (lax.broadcasted_iota(jnp.int32, (rows, rows), 1)
               < lax.broadcasted_iota(jnp.int32, (rows, rows), 0)).astype(BF16)
    before = _dot(earlier, both.astype(BF16)) + cnt_ref[...]
    rank1 = jnp.sum(oh1 * before, axis=-1, keepdims=True)
    rank2 = jnp.sum(oh2 * before, axis=-1, keepdims=True)
    cnt_ref[...] += jnp.sum(both, axis=0, keepdims=True)
    count_ref[...] = jnp.broadcast_to(cnt_ref[...], count_ref.shape)

    out = jnp.zeros_like(lg)
    for slot, val in ((ROUTE_E1, i1 - N_GROUPS), (ROUTE_E2, i2 - N_GROUPS), (ROUTE_RANK1, rank1),
                      (ROUTE_RANK2, rank2), (ROUTE_W1, w1), (ROUTE_W2, w2)):
        out = jnp.where(lane == slot, val, out)
    route_ref[...] = out


def _route(logits):
    n = logits.shape[0]
    return pl.pallas_call(
        _route_kernel,
        grid=(n // ROUTE_TM,),
        in_specs=[pl.BlockSpec((ROUTE_TM, LANES), lambda i: (i, 0))],
        out_specs=[pl.BlockSpec((ROUTE_TM, LANES), lambda i: (i, 0)), pl.BlockSpec((8, LANES), lambda i: (0, 0))],
        out_shape=[jax.ShapeDtypeStruct((n, LANES), F32), jax.ShapeDtypeStruct((8, LANES), F32)],
        scratch_shapes=[pltpu.VMEM((1, LANES), F32)],
        compiler_params=_params(("arbitrary",)),
        name="route",
    )(logits)


GROUP = 8


def _dispatch_plan(route, counts):
    n = route.shape[0]
    counts = counts[0, N_GROUPS:N_GROUPS + N_EXPERTS].astype(jnp.int32)
    padded = (counts + EXPERT_BLOCK - 1) // EXPERT_BLOCK * EXPERT_BLOCK
    pad_end = jnp.cumsum(padded)
    pad_start = pad_end - padded
    e = route[:, ROUTE_E1:ROUTE_E2 + 1].astype(jnp.int32)
    rank = route[:, ROUTE_RANK1:ROUTE_RANK2 + 1].astype(jnp.int32)
    start_of = jnp.sum(jnp.where(e[..., None] == jnp.arange(N_EXPERTS), pad_start, 0), axis=-1)
    dest = (start_of + rank).reshape(-1)
    n_blocks = (2 * n + N_EXPERTS * (EXPERT_BLOCK - 1) + EXPERT_BLOCK - 1) // EXPERT_BLOCK
    n_rows = n_blocks * EXPERT_BLOCK
    assign = jnp.full((n_rows,), -1, jnp.int32).at[dest].set(jnp.arange(2 * n, dtype=jnp.int32), unique_indices=True)
    block = jnp.arange(n_blocks, dtype=jnp.int32)
    block_expert = jnp.minimum(jnp.sum(block[:, None] * EXPERT_BLOCK >= pad_end[None, :], axis=1), N_EXPERTS - 1)
    valid = jnp.clip(jnp.sum(jnp.where(block_expert[:, None] == jnp.arange(N_EXPERTS), pad_start + counts, 0), axis=1)
                     - block * EXPERT_BLOCK, 0, EXPERT_BLOCK)
    copies = (valid + GROUP - 1) // GROUP * GROUP
    used_blocks = (pad_end[-1] // EXPERT_BLOCK).reshape(1)
    row = jnp.arange(n_rows, dtype=jnp.int32)
    spare = 2 * n + ((row // EXPERT_BLOCK) % 2) * GROUP + row % GROUP
    src_token = jnp.where(assign < 0, 0, assign // 2).reshape(n_blocks, 1, EXPERT_BLOCK)
    dst_slot = jnp.where(assign < 0, spare, assign).reshape(n_blocks, 1, EXPERT_BLOCK)
    return (src_token, dst_slot, block_expert.astype(jnp.int32), copies.astype(jnp.int32),
            used_blocks.astype(jnp.int32))


def _tile_rows(i):
    return pl.ds(pl.multiple_of(i * ROW_TILE, ROW_TILE), ROW_TILE)


def _expert_kernel(be_ref, copies_ref, used_ref, src_ref, src_next_ref, dst_ref, h_ref, wg_ref, wu_ref, wd_ref,
                   slots_ref, x_buf, y_buf, wg_s, wu_s, wd_s, gather_sem, scatter_sem):
    i = pl.program_id(0)
    used = used_ref[0]
    cur = i % 2

    def gather_copy(idx_ref, r, buf):
        return pltpu.make_async_copy(h_ref.at[_tile_rows(idx_ref[0, 0, r])], x_buf.at[buf, _tile_rows(r)],
                                     gather_sem.at[buf])

    def scatter_copy(r, buf):
        return pltpu.make_async_copy(y_buf.at[buf, _tile_rows(r)], slots_ref.at[_tile_rows(dst_ref[0, 0, r])],
                                     scatter_sem.at[buf])

    def for_each_group(block, fn):
        def body(g, carry):
            for j in range(GROUP):
                fn(g * GROUP + j, j % 2)
            return carry
        lax.fori_loop(0, copies_ref[block] // GROUP, body, 0)

    @pl.when(i == 0)
    def _():
        x_buf[...] = jnp.zeros_like(x_buf)
        spare_rows = 2 * GROUP * ROW_TILE
        zero_spare = pltpu.make_async_copy(x_buf.at[0, pl.ds(0, spare_rows)],
                                           slots_ref.at[pl.ds(slots_ref.shape[0] - spare_rows, spare_rows)],
                                           scatter_sem.at[0])
        zero_spare.start()
        zero_spare.wait()
        for_each_group(0, lambda r, p: gather_copy(src_ref, r, 0).start(priority=p))

    @pl.when(i + 1 < used)
    def _():
        for_each_group(i + 1, lambda r, p: gather_copy(src_next_ref, r, 1 - cur).start(priority=p))

    @pl.when(jnp.logical_or(i == 0, be_ref[i] != be_ref[jnp.maximum(i - 1, 0)]))
    def _():
        wg_s[...] = wg_ref[0, 0].astype(BF16)
        wu_s[...] = wu_ref[0, 0].astype(BF16)
        wd_s[...] = wd_ref[0, 0].astype(BF16)

    @pl.when(i < used)
    def _():
        for_each_group(i, lambda r, p: gather_copy(src_ref, 0, cur).wait())
        x = jnp.concatenate([_load_token_slab(x_buf.at[cur], EXPERT_BLOCK, ROW_TILE, c) for c in range(ROW_TILE)],
                            axis=1).astype(BF16)
        hidden = _silu(_dot(x, wg_s[...])) * _dot(x, wu_s[...])
        _store_token_tiles(y_buf.at[cur], _dot(hidden.astype(BF16), wd_s[...]), ROW_TILE)
        for_each_group(i, lambda r, p: scatter_copy(r, cur).start(priority=p))

        @pl.when(i > 0)
        def _():
            for_each_group(i - 1, lambda r, p: scatter_copy(0, 1 - cur).wait())

        @pl.when(i == used - 1)
        def _():
            for_each_group(i, lambda r, p: scatter_copy(0, cur).wait())


def _experts(h2_tiles, plan, w_gate, w_up, w_down, layer):
    src_token, dst_slot, block_expert, copies, used_blocks = plan
    n_blocks = src_token.shape[0]
    n_slots = 2 * (h2_tiles.shape[0] // ROW_TILE) + 2 * GROUP
    last = n_blocks - 1

    def smem_block(index):
        return pl.BlockSpec((1, 1, EXPERT_BLOCK), index, memory_space=pltpu.SMEM)

    def weight(shape):
        return pl.BlockSpec((1, 1) + shape, lambda i, be, copies, used: (layer, be[i], 0, 0))

    grid_spec = pltpu.PrefetchScalarGridSpec(
        num_scalar_prefetch=3,
        grid=(n_blocks,),
        in_specs=[
            smem_block(lambda i, be, copies, used: (i, 0, 0)),
            smem_block(lambda i, be, copies, used: (jnp.minimum(i + 1, last), 0, 0)),
            smem_block(lambda i, be, copies, used: (i, 0, 0)),
            pl.BlockSpec(memory_space=pl.ANY),
            weight((D_MODEL, EXPERT_FF)), weight((D_MODEL, EXPERT_FF)), weight((EXPERT_FF, D_MODEL)),
        ],
        out_specs=pl.BlockSpec(memory_space=pl.ANY),
        scratch_shapes=[
            pltpu.VMEM((2, EXPERT_BLOCK * ROW_TILE, LANES), F32),
            pltpu.VMEM((2, EXPERT_BLOCK * ROW_TILE, LANES), F32),
            pltpu.VMEM((D_MODEL, EXPERT_FF), BF16),
            pltpu.VMEM((D_MODEL, EXPERT_FF), BF16),
            pltpu.VMEM((EXPERT_FF, D_MODEL), BF16),
            pltpu.SemaphoreType.DMA((2,)),
            pltpu.SemaphoreType.DMA((2,)),
        ],
    )
    return pl.pallas_call(
        _expert_kernel,
        grid_spec=grid_spec,
        out_shape=jax.ShapeDtypeStruct((n_slots * ROW_TILE, LANES), F32),
        compiler_params=_params(("arbitrary",), VMEM_LIMIT),
        name="experts",
    )(block_expert, copies, used_blocks, src_token, src_token, dst_slot, h2_tiles, w_gate, w_up, w_down)


def _combine_kernel(slots_ref, x_ref, route_ref, mod_ref, fg_ref, o_ref, *, final):
    route = route_ref[...]
    w1 = route[:, ROUTE_W1:ROUTE_W1 + 1]
    w2 = route[:, ROUTE_W2:ROUTE_W2 + 1]
    rows = x_ref.shape[0]
    sq = jnp.zeros((rows, 1), F32)
    for c in range(ROW_TILE):
        lanes = slice(c * LANES, (c + 1) * LANES)
        y1 = _load_token_slab(slots_ref, rows, 2 * ROW_TILE, c)
        y2 = _load_token_slab(slots_ref, rows, 2 * ROW_TILE, c, offset=ROW_TILE)
        x_new = x_ref[:, lanes] + mod_ref[0, 5:6, lanes] * (y1 * w1 + y2 * w2)
        o_ref[:, lanes] = x_new
        sq = sq + jnp.sum(x_new * x_new, axis=-1, keepdims=True)
    if final:
        o_ref[...] = o_ref[...] * lax.rsqrt(sq / D_MODEL + EPS) * fg_ref[...]


def _combine(slots, x_all, route, mod, final_g, n_batch, final):
    n = x_all.shape[0]
    row = lambda i: (i, 0)
    return pl.pallas_call(
        functools.partial(_combine_kernel, final=final),
        grid=(n // TM,),
        in_specs=[
            pl.BlockSpec((TM * 2 * ROW_TILE, LANES), row),
            pl.BlockSpec((TM, D_MODEL), row),
            pl.BlockSpec((TM, LANES), row),
            pl.BlockSpec((1, 6, D_MODEL), _mod_row(n_batch)),
            pl.BlockSpec((1, D_MODEL), lambda i: (0, 0)),
        ],
        out_specs=pl.BlockSpec((TM, D_MODEL), row),
        out_shape=jax.ShapeDtypeStruct((n, D_MODEL), F32),
        compiler_params=_params(("parallel",)),
        name="combine",
    )(slots, x_all, route, mod, final_g.reshape(1, D_MODEL))


def _rope_tables():
    t = jnp.arange(SEQ)
    rows = (t // GRID_W).astype(F32)
    cols = (t % GRID_W).astype(F32)
    n_freq = HEAD_DIM // 4
    inv_freq = ROPE_BASE ** (-jnp.arange(n_freq, dtype=F32) / n_freq)
    ang_r = rows[:, None] * inv_freq
    ang_c = cols[:, None] * inv_freq
    cos = jnp.concatenate([jnp.cos(ang_r)] * 2 + [jnp.cos(ang_c)] * 2, axis=1)
    sin = jnp.concatenate([-jnp.sin(ang_r), jnp.sin(ang_r), -jnp.sin(ang_c), jnp.sin(ang_c)], axis=1)
    return cos, sin


def kernel(x, c, ctx, c_ctx, w_mod, b_mod, norm1_g, norm2_g, w_in, ret_decay, ret_norm_g, conv_w, conv_b,
           mlstm_gate_b, mlstm_norm_g, na_rpb, w_branch, w_out, w_group, w_router, w_expert_gate,
           w_expert_up, w_expert_down, final_norm_g):
    n_batch, seq, d = x.shape
    depth = w_mod.shape[0]
    assert (seq, d, ctx.shape[1]) == (SEQ, D_MODEL, CTX_LEN)
    n = n_batch * TOK

    cond_rows = -(-(n_batch + 1) // 8) * 8
    cond = jnp.zeros((cond_rows, d), F32).at[:n_batch].set(c).at[n_batch].set(c_ctx)
    mod_all = _modulation(cond, w_mod, b_mod).reshape(depth, cond_rows, 6, d)

    cos, sin = _rope_tables()
    x_all = jnp.concatenate([ctx, x], axis=1).reshape(n, d)

    w_packed = _pack_in_weights(w_in)
    w_branch_b = w_branch.astype(BF16)
    w_out_b = w_out.astype(BF16)
    w_route = jnp.concatenate([w_group, w_router], axis=-1)
    w_route = jnp.pad(w_route, ((0, 0), (0, 0), (0, LANES - w_route.shape[-1])))
    wr_hi = w_route.astype(BF16)
    wr_lo = (w_route - wr_hi.astype(F32)).astype(BF16)
    gate_b = jnp.pad(mlstm_gate_b.reshape(depth, 1, 4 * N_HEADS), ((0, 0), (0, 0), (0, LANES - 4 * N_HEADS)))
    na_bias = _na_bias_tables(na_rpb)

    for layer in range(depth):
        mod = mod_all[layer]
        ret, ml, gates, na, bg = _in_projection(x_all, mod, norm1_g, w_packed, layer, n_batch)
        r_out = _retention(ret, ret_decay[layer], ret_norm_g[layer], cos, sin, n_batch)
        m_out = _mlstm(ml, gates, gate_b[layer], conv_w[layer], conv_b[layer].reshape(1, 2 * WIDTH),
                       mlstm_norm_g[layer], n_batch)
        a_out = _neighbourhood_attention(na, na_bias, layer, n_batch)
        x_all, h2, logits = _merge(r_out, m_out, a_out, bg, x_all, mod, w_branch_b, w_out_b, norm2_g,
                                   wr_hi, wr_lo, layer, n_batch)
        route, counts = _route(logits)
        plan = _dispatch_plan(route, counts)
        slots = _experts(h2, plan, w_expert_gate, w_expert_up, w_expert_down, layer)
        x_all = _combine(slots, x_all, route, mod, final_norm_g, n_batch, final=layer == depth - 1)

    return x_all.reshape(n_batch, TOK, d)[:, CTX_LEN:]
```

```python
import functools

import numpy as np
import jax
import jax.numpy as jnp
from jax import lax
from jax.experimental import pallas as pl
from jax.experimental.pallas import tpu as pltpu

F32 = jnp.float32
BF16 = jnp.bfloat16

D_MODEL = 1024
SEQ = 2048
CTX_LEN = 256
TOK = CTX_LEN + SEQ
GRID_W = 64
GRID_ROWS = SEQ // GRID_W
HEAD_DIM = 128
N_HEADS = 4
WIDTH = N_HEADS * HEAD_DIM
CHUNK = 256
N_CHUNKS = TOK // CHUNK
CTX_CHUNKS = CTX_LEN // CHUNK
CONV_WIDTH = 5
NA_WIN_ROWS = 8
NA_WIN_COLS = 16
NA_Q_ROWS = 4
NA_Q = NA_Q_ROWS * GRID_W
NA_K_ROWS = NA_Q_ROWS + NA_WIN_ROWS
NA_K = NA_K_ROWS * GRID_W
NA_STEPS = SEQ // NA_Q
ROPE_BASE = 10000.0
N_GROUPS = 4
EXPERTS_PER_GROUP = 8
N_EXPERTS = N_GROUPS * EXPERTS_PER_GROUP
EXPERT_FF = 512
EXPERT_BLOCK = 256
EPS = 1e-6
NEG_INF = -1e30
QK_SCALE = HEAD_DIM ** -0.5

TM = 256
ROUTE_TM = 512
LANES = 128
VMEM_LIMIT = 56 * 1024 * 1024


def _dot(a, b):
    return jnp.dot(a, b, preferred_element_type=F32)


def _dot_nt(a, b):
    return lax.dot_general(a, b, (((1,), (1,)), ((), ())), preferred_element_type=F32)


def _bdot(a, b):
    return _dot(a.astype(BF16), b.astype(BF16))


def _bdot_nt(a, b):
    return _dot_nt(a.astype(BF16), b.astype(BF16))


def _bdot_tn(a, b):
    return _dot(a.T.astype(BF16), b.astype(BF16))


def _log_sigmoid(x):
    return jnp.minimum(x, 0.0) - jnp.log1p(jnp.exp(-jnp.abs(x)))


def _silu(x):
    return x * jax.nn.sigmoid(x)


def _params(sem, vmem=None):
    return pltpu.CompilerParams(dimension_semantics=sem, vmem_limit_bytes=vmem)


def _mod_row(n_batch):
    tiles = TOK // TM

    def index(i):
        return (jnp.where(i % tiles == 0, n_batch, i // tiles), 0, 0)

    return index


def _mod_kernel(c_ref, w_ref, b_ref, o_ref):
    cond = _silu(c_ref[...])
    o_ref[0] = _bdot(cond, w_ref[0]) + b_ref[0]


def _modulation(cond, w_mod, b_mod):
    depth, d, cols = w_mod.shape
    rows = cond.shape[0]
    tn = 1536
    return pl.pallas_call(
        _mod_kernel,
        grid=(depth, cols // tn),
        in_specs=[
            pl.BlockSpec((rows, d), lambda l, j: (0, 0)),
            pl.BlockSpec((1, d, tn), lambda l, j: (l, 0, j)),
            pl.BlockSpec((1, 1, tn), lambda l, j: (l, 0, j)),
        ],
        out_specs=pl.BlockSpec((1, rows, tn), lambda l, j: (l, 0, j)),
        out_shape=jax.ShapeDtypeStruct((depth, rows, cols), F32),
        compiler_params=_params(("parallel", "parallel")),
        name="modulation",
    )(cond, w_mod, b_mod.reshape(depth, 1, cols))


def _rms_modulate(x, g, shift, scale):
    y = x * lax.rsqrt(jnp.mean(x * x, axis=-1, keepdims=True) + EPS) * g
    return y * (1.0 + scale) + shift


IN_WIDTHS = (4 * WIDTH, 4 * WIDTH, LANES, 3 * WIDTH, 3 * D_MODEL)
IN_OFFSETS = tuple(int(v) for v in np.cumsum((0,) + IN_WIDTHS))


def _pack_kernel(w_ref, o_ref):
    gate0 = IN_OFFSETS[2]
    n_gate = 4 * N_HEADS
    o_ref[0, :, :gate0] = w_ref[0, :, :gate0].astype(BF16)
    window = w_ref[0, :, gate0:gate0 + LANES]
    lane = lax.broadcasted_iota(jnp.int32, window.shape, 1)
    o_ref[0, :, gate0:gate0 + LANES] = jnp.where(lane < n_gate, window, 0.0).astype(BF16)
    o_ref[0, :, gate0 + LANES:] = w_ref[0, :, gate0 + n_gate:].astype(BF16)


def _pack_in_weights(w_in):
    depth, d, cols = w_in.shape
    rows = 256
    return pl.pallas_call(
        _pack_kernel,
        grid=(depth, d // rows),
        in_specs=[pl.BlockSpec((1, rows, cols), lambda l, r: (l, r, 0))],
        out_specs=pl.BlockSpec((1, rows, IN_OFFSETS[-1]), lambda l, r: (l, r, 0)),
        out_shape=jax.ShapeDtypeStruct((depth, d, IN_OFFSETS[-1]), BF16),
        compiler_params=_params(("parallel", "parallel"), VMEM_LIMIT),
        name="pack_in_weights",
    )(w_in)


def _project(x, mod_ref, g_ref, w_ref, out_refs):
    h = _rms_modulate(x, g_ref[0], mod_ref[0, 0:1, :], mod_ref[0, 1:2, :]).astype(BF16)
    for o_ref, lo, hi in zip(out_refs, IN_OFFSETS[:-1], IN_OFFSETS[1:]):
        o_ref[...] = _dot(h, w_ref[0, :, lo:hi])


def _inproj_kernel(x_ref, mod_ref, g_ref, w_ref, *out_refs):
    _project(x_ref[...], mod_ref, g_ref, w_ref, out_refs)


def _combine_inproj_kernel(slots_ref, x_ref, route_ref, prev_mod_ref, mod_ref, g_ref, w_ref, x_out_ref, *out_refs):
    _moe_residual(slots_ref, x_ref, route_ref, prev_mod_ref, x_out_ref)
    _project(x_out_ref[...], mod_ref, g_ref, w_ref, out_refs)


def _in_projection(x_all, mod, norm_g, w_packed, layer, n_batch, moe=None):
    n = x_all.shape[0]
    row = lambda i: (i, 0)
    in_specs = [
        pl.BlockSpec((TM, D_MODEL), row),
        pl.BlockSpec((1, 6, D_MODEL), _mod_row(n_batch)),
        pl.BlockSpec((1, 1, D_MODEL), lambda i: (layer, 0, 0)),
        pl.BlockSpec((1,) + w_packed.shape[1:], lambda i: (layer, 0, 0), pipeline_mode=pl.Buffered(1)),
    ]
    out_specs = [pl.BlockSpec((TM, w), row) for w in IN_WIDTHS]
    out_shape = [jax.ShapeDtypeStruct((n, w), F32) for w in IN_WIDTHS]
    args = (x_all, mod, norm_g.reshape(-1, 1, D_MODEL), w_packed)
    if moe is None:
        body = _inproj_kernel
    else:
        slots, route, prev_mod = moe
        body = _combine_inproj_kernel
        in_specs = ([pl.BlockSpec((TM * 2 * ROW_TILE, LANES), row), in_specs[0], pl.BlockSpec((TM, LANES), row),
                     pl.BlockSpec((1, 6, D_MODEL), _mod_row(n_batch))] + in_specs[1:])
        out_specs = [pl.BlockSpec((TM, D_MODEL), row)] + out_specs
        out_shape = [jax.ShapeDtypeStruct((n, D_MODEL), F32)] + out_shape
        args = (slots, x_all, route, prev_mod) + args[1:]
    return pl.pallas_call(
        body,
        grid=(n // TM,),
        in_specs=in_specs,
        out_specs=out_specs,
        out_shape=out_shape,
        compiler_params=_params(("parallel",), VMEM_LIMIT),
        name="in_projection",
    )(*args)


def _chunk_order(t):
    fwd = t
    bwd = jnp.where(t < CTX_CHUNKS, CTX_CHUNKS - 1 - t, N_CHUNKS + CTX_CHUNKS - 1 - t)
    return fwd, bwd


def _chunk_slice(c):
    return pl.ds(pl.multiple_of(c * CHUNK, CHUNK), CHUNK)


def _head_norm(y, gain):
    mu = jnp.mean(y, axis=-1, keepdims=True)
    yc = y - mu
    var = jnp.mean(yc * yc, axis=-1, keepdims=True)
    return yc * lax.rsqrt(var + EPS) * gain


ROW_TILE = D_MODEL // LANES


def _store_token_tiles(ref, x, pitch, offset=0):
    rows = x.shape[0]
    for c in range(ROW_TILE):
        ref[pl.ds(offset + c, rows, stride=pitch), :] = x[:, c * LANES:(c + 1) * LANES]


def _load_token_slab(ref, rows, pitch, c, offset=0):
    return ref[pl.ds(offset + c, rows, stride=pitch), :]


def _moe_residual(slots_ref, x_ref, route_ref, mod_ref, o_ref):
    route = route_ref[...]
    w1 = route[:, ROUTE_W1:ROUTE_W1 + 1]
    w2 = route[:, ROUTE_W2:ROUTE_W2 + 1]
    rows = x_ref.shape[0]
    sq = jnp.zeros((rows, 1), F32)
    for c in range(ROW_TILE):
        lanes = slice(c * LANES, (c + 1) * LANES)
        y1 = _load_token_slab(slots_ref, rows, 2 * ROW_TILE, c)
        y2 = _load_token_slab(slots_ref, rows, 2 * ROW_TILE, c, offset=ROW_TILE)
        x_new = x_ref[:, lanes] + mod_ref[0, 5:6, lanes] * (y1 * w1 + y2 * w2)
        o_ref[:, lanes] = x_new
        sq = sq + jnp.sum(x_new * x_new, axis=-1, keepdims=True)
    return sq


def _select_lane(x, lane, idx):
    return jnp.sum(jnp.where(lane == idx, x, 0.0), axis=-1, keepdims=True)


HEADS_PER_STEP = 2
HEAD_STEPS = N_HEADS // HEADS_PER_STEP
STEP_WIDTH = HEADS_PER_STEP * HEAD_DIM
SCAN_VMEM = 48 * 1024 * 1024


def _head_lanes(hh):
    return slice(hh * HEAD_DIM, (hh + 1) * HEAD_DIM)


def _retention_kernel(dec_ref, q_ref, k_ref, v_ref, g_ref, cos_ref, sin_ref, gn_ref, o_ref,
                      qs_ref, ks_ref, ob_ref, intra_ref, st_ref):
    first_head = pl.program_id(1) * HEADS_PER_STEP

    lane = lax.broadcasted_iota(jnp.int32, (CHUNK, HEAD_DIM), 1)
    first_half = (lane % (HEAD_DIM // 2)) < (HEAD_DIM // 4)

    def rope(x, cos, sin):
        rot = jnp.where(first_half, pltpu.roll(x, HEAD_DIM - HEAD_DIM // 4, 1), pltpu.roll(x, HEAD_DIM // 4, 1))
        return x * cos + rot * sin

    qs_ref[0:CTX_LEN, :] = q_ref[0:CTX_LEN, :]
    ks_ref[0:CTX_LEN, :] = k_ref[0:CTX_LEN, :] * QK_SCALE

    def rotate_chunk(c, carry):
        rows = _chunk_slice(c)
        pos = _chunk_slice(c - CTX_CHUNKS)
        cos = cos_ref[pos, :]
        sin = sin_ref[pos, :]
        for hh in range(HEADS_PER_STEP):
            qs_ref[rows, _head_lanes(hh)] = rope(q_ref[rows, _head_lanes(hh)], cos, sin)
            ks_ref[rows, _head_lanes(hh)] = rope(k_ref[rows, _head_lanes(hh)], cos, sin) * QK_SCALE
        return carry

    lax.fori_loop(CTX_CHUNKS, N_CHUNKS, rotate_chunk, 0)

    ii = lax.broadcasted_iota(jnp.int32, (CHUNK, CHUNK), 0).astype(F32)
    jj = lax.broadcasted_iota(jnp.int32, (CHUNK, CHUNK), 1).astype(F32)
    col = lax.broadcasted_iota(jnp.int32, (CHUNK, 1), 0).astype(F32)

    def decay_mat(dist, lg_dir):
        ok = dist >= 0
        return jnp.where(ok, jnp.exp(jnp.where(ok, dist, 0.0) * lg_dir), 0.0)

    dec = _log_sigmoid(dec_ref[...])
    hl = lax.broadcasted_iota(jnp.int32, dec.shape, 1)
    consts = []
    for hh in range(HEADS_PER_STEP):
        lg = jnp.sum(jnp.where(hl == first_head + hh, dec, 0.0), axis=-1, keepdims=True)
        lg_f, lg_b = lg[0:1, :], lg[1:2, :]
        intra_ref[2 * hh] = decay_mat(ii - jj, lg_f)
        intra_ref[2 * hh + 1] = decay_mat(jj - ii, lg_b)
        consts.append(dict(
            q_decay=(jnp.exp((col + 1.0) * lg_f), jnp.exp((CHUNK - col) * lg_b)),
            k_decay=(jnp.exp((CHUNK - 1.0 - col) * lg_f), jnp.exp(col * lg_b)),
            chunk_decay=(jnp.exp(CHUNK * lg_f), jnp.exp(CHUNK * lg_b))))

    st_ref[...] = jnp.zeros_like(st_ref)
    out_refs = (o_ref, ob_ref)

    def step(t, carry):
        for hh in range(HEADS_PER_STEP):
            cst = consts[hh]
            for d, c in enumerate(_chunk_order(t)):
                rows = _chunk_slice(c)
                q = qs_ref[rows, _head_lanes(hh)]
                k = ks_ref[rows, _head_lanes(hh)]
                v = v_ref[rows, _head_lanes(hh)]
                s_prev = st_ref[2 * hh + d]
                scores = _bdot_nt(q, k) * intra_ref[2 * hh + d]
                out_refs[d][rows, _head_lanes(hh)] = _bdot(scores, v) + _bdot(q * cst["q_decay"][d], s_prev)
                st_ref[2 * hh + d] = s_prev * cst["chunk_decay"][d] + _bdot_tn(k * cst["k_decay"][d], v)
        return carry

    lax.fori_loop(0, N_CHUNKS, step, 0)

    def finish_chunk(c, carry):
        rows = _chunk_slice(c)
        for hh in range(HEADS_PER_STEP):
            lanes = _head_lanes(hh)
            y = o_ref[rows, lanes] + ob_ref[rows, lanes]
            o_ref[rows, lanes] = _head_norm(y, gn_ref[:, lanes]) * _silu(g_ref[rows, lanes])
        return carry

    lax.fori_loop(0, N_CHUNKS, finish_chunk, 0)


def _retention(ret, ret_decay, norm_g, cos, sin, n_batch):
    n = ret.shape[0]

    def head_block(offset):
        return pl.BlockSpec((TOK, STEP_WIDTH), lambda b, h: (b, offset * HEAD_STEPS + h))

    return pl.pallas_call(
        _retention_kernel,
        grid=(n_batch, HEAD_STEPS),
        in_specs=[
            pl.BlockSpec(ret_decay.shape, lambda b, h: (0, 0)),
            head_block(0), head_block(1), head_block(2), head_block(3),
            pl.BlockSpec((SEQ, HEAD_DIM), lambda b, h: (0, 0)),
            pl.BlockSpec((SEQ, HEAD_DIM), lambda b, h: (0, 0)),
            pl.BlockSpec((1, STEP_WIDTH), lambda b, h: (0, h)),
        ],
        out_specs=pl.BlockSpec((TOK, STEP_WIDTH), lambda b, h: (b, h)),
        out_shape=jax.ShapeDtypeStruct((n, WIDTH), F32),
        scratch_shapes=[
            pltpu.VMEM((TOK, STEP_WIDTH), F32),
            pltpu.VMEM((TOK, STEP_WIDTH), F32),
            pltpu.VMEM((TOK, STEP_WIDTH), F32),
            pltpu.VMEM((2 * HEADS_PER_STEP, CHUNK, CHUNK), F32),
            pltpu.VMEM((2 * HEADS_PER_STEP, HEAD_DIM, HEAD_DIM), F32),
        ],
        compiler_params=_params(("parallel", "parallel"), SCAN_VMEM),
        name="retention",
    )(ret_decay, ret, ret, ret, ret, cos, sin, norm_g.reshape(1, WIDTH))


CONV_PAD = 8
CONV_ROWS = 128


def _mlstm_kernel(q_ref, k_ref, v_ref, og_ref, gate_ref, gb_ref, cwq_ref, cwk_ref, cbq_ref, cbk_ref,
                  gn_ref, o_ref, qs_ref, ks_ref, ob_ref, pad_ref, gx_ref, gxt_ref, cst_ref):
    first_head = pl.program_id(1) * HEADS_PER_STEP

    pad_ref[0:CONV_PAD, :] = jnp.zeros((CONV_PAD, STEP_WIDTH), F32)
    pad_ref[CONV_PAD + TOK:, :] = jnp.zeros((CONV_PAD, STEP_WIDTH), F32)
    crow = lax.broadcasted_iota(jnp.int32, (CONV_ROWS, 1), 0)

    def conv(u_ref, w_ref, b_ref, dst_ref, scale):
        def fill(c, carry):
            rows = _chunk_slice(c)
            pad_ref[pl.ds(pl.multiple_of(c * CHUNK + CONV_PAD, CONV_PAD), CHUNK), :] = u_ref[rows, :]
            return carry

        lax.fori_loop(0, N_CHUNKS, fill, 0)

        for blk in range(TOK // CONV_ROWS):
            first = blk * CONV_ROWS
            acc = jnp.zeros((CONV_ROWS, STEP_WIDTH), F32)
            for j in range(CONV_WIDTH):
                shift = j - CONV_WIDTH // 2
                tap = pad_ref[first + CONV_PAD + shift:first + CONV_PAD + shift + CONV_ROWS, :]
                if (first + CONV_ROWS == CTX_LEN and shift > 0) or (first == CTX_LEN and shift < 0):
                    trow = crow + first
                    tap = jnp.where((trow < CTX_LEN) == (trow + shift < CTX_LEN), tap, 0.0)
                acc = acc + tap * w_ref[j:j + 1, :]
            dst_ref[first:first + CONV_ROWS, :] = _silu(acc + b_ref[...]) * scale

    conv(q_ref, cwq_ref, cbq_ref, qs_ref, 1.0)
    conv(k_ref, cwk_ref, cbk_ref, ks_ref, QK_SCALE)

    clane = lax.broadcasted_iota(jnp.int32, (CHUNK, LANES), 1)
    is_forget = ((clane // N_HEADS) % 2) == 1

    def gate_chunk(c, carry):
        rows = _chunk_slice(c)
        g = gate_ref[rows, :] + gb_ref[...]
        gx = jnp.where(is_forget, _log_sigmoid(g), g)
        gx_ref[rows, :] = gx
        gxt_ref[c] = gx.T
        return carry

    lax.fori_loop(0, N_CHUNKS, gate_chunk, 0)

    ii = lax.broadcasted_iota(jnp.int32, (CHUNK, CHUNK), 0)
    jj = lax.broadcasted_iota(jnp.int32, (CHUNK, CHUNK), 1)
    causal = (jj <= ii, jj >= ii)

    cst_ref[...] = jnp.zeros_like(cst_ref)
    out_refs = (o_ref, ob_ref)

    def step(t, carry):
        new_carry = []
        for hh in range(HEADS_PER_STEP):
            head = first_head + hh
            lanes = _head_lanes(hh)
            for d, c in enumerate(_chunk_order(t)):
                slot = 2 * hh + d
                n_prev, m_prev = carry[2 * slot], carry[2 * slot + 1]
                rows = _chunk_slice(c)
                q = qs_ref[rows, lanes]
                k = ks_ref[rows, lanes]
                v = v_ref[rows, lanes]
                gc = gx_ref[rows, :]
                i_col = _select_lane(gc, clane, 2 * d * N_HEADS + head)
                f_col = _select_lane(gc, clane, (2 * d + 1) * N_HEADS + head)
                i_row = gxt_ref[c, pl.ds(2 * d * N_HEADS + head, 1), :]
                f_row = gxt_ref[c, pl.ds((2 * d + 1) * N_HEADS + head, 1), :]
                vis = causal[d]
                cum_col = jnp.sum(jnp.where(vis, f_row, 0.0), axis=1, keepdims=True)
                cum_row = jnp.sum(jnp.where(causal[1 - d], f_col, 0.0), axis=0, keepdims=True)
                total = jnp.sum(f_row, axis=1, keepdims=True)
                c_prev = cst_ref[slot]

                log_kw = total - cum_col + i_col
                m_new = jnp.maximum(total + m_prev, jnp.max(log_kw, axis=0, keepdims=True))
                kw = jnp.exp(log_kw - m_new)
                pw = jnp.exp(total + m_prev - m_new)
                cst_ref[slot] = pw * c_prev + _bdot_tn(k * kw, v)
                n_new = pw * n_prev + jnp.sum(kw * k, axis=0, keepdims=True)

                log_w = jnp.where(vis, cum_col - cum_row + i_row, -jnp.inf)
                log_p = cum_col + m_prev
                m_t = jnp.maximum(log_p, jnp.max(log_w, axis=1, keepdims=True))
                w = jnp.exp(log_w - m_t)
                p = jnp.exp(log_p - m_t)
                qk = _bdot_nt(q, k) * w
                num = _bdot(qk, v) + p * _bdot(q, c_prev)
                den = jnp.sum(qk, axis=1, keepdims=True) + p * jnp.sum(q * n_prev, axis=1, keepdims=True)
                out_refs[d][rows, lanes] = num / jnp.maximum(jnp.abs(den), jnp.exp(-m_t))
                new_carry += [n_new, m_new]
        return tuple(new_carry)

    zero_n = jnp.zeros((1, HEAD_DIM), F32)
    zero_m = jnp.zeros((1, 1), F32)
    lax.fori_loop(0, N_CHUNKS, step, (zero_n, zero_m) * (2 * HEADS_PER_STEP))

    def finish_chunk(c, carry):
        rows = _chunk_slice(c)
        for hh in range(HEADS_PER_STEP):
            lanes = _head_lanes(hh)
            gated = (o_ref[rows, lanes] + ob_ref[rows, lanes]) * jax.nn.sigmoid(og_ref[rows, lanes])
            o_ref[rows, lanes] = _head_norm(gated, gn_ref[:, lanes])
        return carry

    lax.fori_loop(0, N_CHUNKS, finish_chunk, 0)


def _mlstm(ml, gates, gate_b, conv_w, conv_b, norm_g, n_batch):
    n = ml.shape[0]

    def head_block(offset):
        return pl.BlockSpec((TOK, STEP_WIDTH), lambda b, h: (b, offset * HEAD_STEPS + h))

    def head_cols(rows, offset):
        return pl.BlockSpec((rows, STEP_WIDTH), lambda b, h: (0, offset * HEAD_STEPS + h))

    return pl.pallas_call(
        _mlstm_kernel,
        grid=(n_batch, HEAD_STEPS),
        in_specs=[
            head_block(0), head_block(1), head_block(2), head_block(3),
            pl.BlockSpec((TOK, LANES), lambda b, h: (b, 0)),
            pl.BlockSpec((1, LANES), lambda b, h: (0, 0)),
            head_cols(CONV_WIDTH, 0), head_cols(CONV_WIDTH, 1),
            head_cols(1, 0), head_cols(1, 1),
            head_cols(1, 0),
        ],
        out_specs=pl.BlockSpec((TOK, STEP_WIDTH), lambda b, h: (b, h)),
        out_shape=jax.ShapeDtypeStruct((n, WIDTH), F32),
        scratch_shapes=[
            pltpu.VMEM((TOK, STEP_WIDTH), F32),
            pltpu.VMEM((TOK, STEP_WIDTH), F32),
            pltpu.VMEM((TOK, STEP_WIDTH), F32),
            pltpu.VMEM((TOK + 2 * CONV_PAD, STEP_WIDTH), F32),
            pltpu.VMEM((TOK, LANES), F32),
            pltpu.VMEM((N_CHUNKS, LANES, CHUNK), F32),
            pltpu.VMEM((2 * HEADS_PER_STEP, HEAD_DIM, HEAD_DIM), F32),
        ],
        compiler_params=_params(("parallel", "parallel"), SCAN_VMEM),
        name="mlstm",
    )(ml, ml, ml, ml, gates, gate_b, conv_w, conv_w, conv_b, conv_b, norm_g.reshape(1, WIDTH))


def _na_kernel(q_ref, k_ref, v_ref, bias_ref, o_ref):
    step = pl.program_id(2)

    def context_scores(hh):
        lanes = _head_lanes(hh)
        q = q_ref[:, lanes].astype(BF16)
        v_ctx = v_ref[0:CTX_LEN, lanes].astype(BF16)
        return q, v_ctx, _dot_nt(q, k_ref[0:CTX_LEN, lanes].astype(BF16)) * QK_SCALE

    @pl.when(step == 0)
    def _():
        for hh in range(HEADS_PER_STEP):
            _, v_ctx, s_ctx = context_scores(hh)
            m = jnp.max(s_ctx, axis=-1, keepdims=True)
            p = jnp.exp(s_ctx - m)
            o_ref[:, _head_lanes(hh)] = _dot(p.astype(BF16), v_ctx) / jnp.sum(p, axis=-1, keepdims=True)

    @pl.when(step > 0)
    def _():
        start = CTX_LEN + NA_Q * jnp.clip(step - 2, 0, NA_STEPS - NA_K_ROWS // NA_Q_ROWS)
        rows = pl.ds(pl.multiple_of(start, NA_Q), NA_K)
        for hh in range(HEADS_PER_STEP):
            lanes = _head_lanes(hh)
            q, v_ctx, s_ctx = context_scores(hh)
            s_loc = _dot_nt(q, k_ref[rows, lanes].astype(BF16)) * QK_SCALE + bias_ref[0, hh, 0]
            m = jnp.maximum(jnp.max(s_loc, axis=-1, keepdims=True), jnp.max(s_ctx, axis=-1, keepdims=True))
            p_loc = jnp.exp(s_loc - m)
            p_ctx = jnp.exp(s_ctx - m)
            denom = jnp.sum(p_loc, axis=-1, keepdims=True) + jnp.sum(p_ctx, axis=-1, keepdims=True)
            o = _dot(p_loc.astype(BF16), v_ref[rows, lanes].astype(BF16)) + _dot(p_ctx.astype(BF16), v_ctx)
            o_ref[:, lanes] = o / denom


def _na_row_case(case, a):
    t = np.arange(NA_K_ROWS)
    last_start = NA_K_ROWS - NA_WIN_ROWS
    return [
        (t < NA_WIN_ROWS, NA_WIN_ROWS - 1 - a),
        ((t >= a) & (t < a + NA_WIN_ROWS), NA_WIN_ROWS // 2 - 1 - a),
        (t >= last_start, NA_Q_ROWS - NA_K_ROWS + NA_WIN_ROWS - 1 - a),
    ][case]


def _na_expand_kernel(slab_ref, o_ref):
    masked = jnp.full((GRID_W, GRID_W), NEG_INF, F32)
    for case in range(3):
        for a in range(NA_Q_ROWS):
            row_ok, first = _na_row_case(case, a)
            for pair in range(NA_K_ROWS // 2):
                tiles = [slab_ref[0, first + t] if row_ok[t] else masked for t in (2 * pair, 2 * pair + 1)]
                o_ref[0, case, a * GRID_W:(a + 1) * GRID_W, pair * LANES:(pair + 1) * LANES] = (
                    jnp.concatenate(tiles, axis=1))


def _na_bias_tables(rpb):
    lead = rpb.shape[:-2]
    n_row_off, n_col_off = rpb.shape[-2:]
    qc = np.arange(GRID_W)[:, None]
    kc = np.arange(GRID_W)[None, :]
    col_start = np.clip(qc - NA_WIN_COLS // 2, 0, GRID_W - NA_WIN_COLS)
    col_ok = (kc >= col_start) & (kc < col_start + NA_WIN_COLS)
    col_idx = np.clip(kc - qc + NA_WIN_COLS - 1, 0, n_col_off - 1)
    onehot = ((col_idx[None] == np.arange(n_col_off)[:, None, None]) & col_ok[None]).astype(np.float32)
    slabs = jnp.einsum('...rc,cqk->...rqk', rpb.astype(F32), onehot, precision=lax.Precision.HIGHEST)
    slabs = jnp.where(col_ok, slabs, NEG_INF).reshape(-1, n_row_off, GRID_W, GRID_W)
    n_tables = slabs.shape[0]
    tables = pl.pallas_call(
        _na_expand_kernel,
        grid=(n_tables,),
        in_specs=[pl.BlockSpec((1, n_row_off, GRID_W, GRID_W), lambda i: (i, 0, 0, 0))],
        out_specs=pl.BlockSpec((1, 3, NA_Q, NA_K), lambda i: (i, 0, 0, 0)),
        out_shape=jax.ShapeDtypeStruct((n_tables, 3, NA_Q, NA_K), F32),
        compiler_params=_params(("parallel",)),
        name="na_bias_tables",
    )(slabs)
    return tables.reshape(*lead, 3, NA_Q, NA_K)


def _neighbourhood_attention(na, bias, layer, n_batch):
    n = na.shape[0]
    steps = 1 + NA_STEPS
    tiles = TOK // NA_Q

    def table(b, h, j):
        return (layer, h, jnp.where(j <= 1, 0, jnp.where(j == NA_STEPS, 2, 1)), 0, 0)

    return pl.pallas_call(
        _na_kernel,
        grid=(n_batch, HEAD_STEPS, steps),
        in_specs=[
            pl.BlockSpec((NA_Q, STEP_WIDTH), lambda b, h, j: (b * tiles + j, h)),
            pl.BlockSpec((TOK, STEP_WIDTH), lambda b, h, j: (b, HEAD_STEPS + h)),
            pl.BlockSpec((TOK, STEP_WIDTH), lambda b, h, j: (b, 2 * HEAD_STEPS + h)),
            pl.BlockSpec((1, HEADS_PER_STEP, 1, NA_Q, NA_K), table),
        ],
        out_specs=pl.BlockSpec((NA_Q, STEP_WIDTH), lambda b, h, j: (b * tiles + j, h)),
        out_shape=jax.ShapeDtypeStruct((n, WIDTH), F32),
        compiler_params=_params(("parallel", "parallel", "arbitrary")),
        name="neighbourhood_attention",
    )(na, na, na, bias)


def _merge_kernel(r_ref, m_ref, a_ref, bg_ref, x_ref, mod_ref, wb_ref, wo_ref, g2_ref, wr_hi_ref, wr_lo_ref,
                  x_out_ref, h2_ref, logit_ref):
    gate = jax.nn.sigmoid(bg_ref[...])
    mix = (gate[:, 0:D_MODEL] * _dot(r_ref[...].astype(BF16), wb_ref[0, 0])
           + gate[:, D_MODEL:2 * D_MODEL] * _dot(m_ref[...].astype(BF16), wb_ref[0, 1])
           + gate[:, 2 * D_MODEL:] * _dot(a_ref[...].astype(BF16), wb_ref[0, 2]))
    y = _dot(mix.astype(BF16), wo_ref[0])
    x_new = x_ref[...] + mod_ref[0, 2:3, :] * y
    x_out_ref[...] = x_new
    h2 = _rms_modulate(x_new, g2_ref[0], mod_ref[0, 3:4, :], mod_ref[0, 4:5, :])
    _store_token_tiles(h2_ref, h2, ROW_TILE)
    hi = h2.astype(BF16)
    lo = (h2 - hi.astype(F32)).astype(BF16)
    logit_ref[...] = _dot(hi, wr_hi_ref[0]) + (_dot(lo, wr_hi_ref[0]) + _dot(hi, wr_lo_ref[0]))


def _merge(r, m, a, bg, x_all, mod, w_branch, w_out, norm2_g, wr_hi, wr_lo, layer, n_batch):
    n = x_all.shape[0]
    row = lambda i: (i, 0)

    def layer_block(w):
        return pl.BlockSpec((1,) + w.shape[1:], lambda i: (layer,) + (0,) * (w.ndim - 1))

    return pl.pallas_call(
        _merge_kernel,
        grid=(n // TM,),
        in_specs=[
            pl.BlockSpec((TM, WIDTH), row), pl.BlockSpec((TM, WIDTH), row), pl.BlockSpec((TM, WIDTH), row),
            pl.BlockSpec((TM, 3 * D_MODEL), row),
            pl.BlockSpec((TM, D_MODEL), row),
            pl.BlockSpec((1, 6, D_MODEL), _mod_row(n_batch)),
            layer_block(w_branch), layer_block(w_out),
            pl.BlockSpec((1, 1, D_MODEL), lambda i: (layer, 0, 0)),
            layer_block(wr_hi), layer_block(wr_lo),
        ],
        out_specs=[pl.BlockSpec((TM, D_MODEL), row), pl.BlockSpec((TM * ROW_TILE, LANES), row),
                   pl.BlockSpec((TM, LANES), row)],
        out_shape=[jax.ShapeDtypeStruct((n, D_MODEL), F32), jax.ShapeDtypeStruct((n * ROW_TILE, LANES), F32),
                   jax.ShapeDtypeStruct((n, LANES), F32)],
        compiler_params=_params(("parallel",), VMEM_LIMIT),
        name="merge",
    )(r, m, a, bg, x_all, mod, w_branch, w_out, norm2_g.reshape(-1, 1, D_MODEL), wr_hi, wr_lo)


ROUTE_E1, ROUTE_E2, ROUTE_RANK1, ROUTE_RANK2, ROUTE_W1, ROUTE_W2 = range(6)


def _route_kernel(logit_ref, route_ref, count_ref, cnt_ref):
    @pl.when(pl.program_id(0) == 0)
    def _():
        cnt_ref[...] = jnp.zeros_like(cnt_ref)

    lg = logit_ref[...]
    lane = lax.broadcasted_iota(jnp.int32, lg.shape, 1)
    lane_f = lane.astype(F32)

    def first_argmax(vals):
        top = jnp.max(vals, axis=-1, keepdims=True)
        idx = jnp.min(jnp.where(vals == top, lane_f, float(LANES)), axis=-1, keepdims=True)
        return top, idx

    group_logits = jnp.where(lane < N_GROUPS, lg, -jnp.inf)
    g_top, g_idx = first_argmax(group_logits)
    group_w = 1.0 / jnp.sum(jnp.exp(group_logits - g_top), axis=-1, keepdims=True)

    first = N_GROUPS + EXPERTS_PER_GROUP * g_idx
    in_group = (lane_f >= first) & (lane_f < first + EXPERTS_PER_GROUP)
    expert_logits = jnp.where(in_group, lg, -jnp.inf)
    v1, i1 = first_argmax(expert_logits)
    v2, i2 = first_argmax(jnp.where(lane_f == i1, -jnp.inf, expert_logits))
    t = jnp.exp(v2 - v1)
    w1 = group_w / (1.0 + t)
    w2 = group_w * t / (1.0 + t)

    oh1 = (lane_f == i1).astype(F32)
    oh2 = (lane_f == i2).astype(F32)
    both = oh1 + oh2
    rows = lg.shape[0]
    earlier = (lax.broadcasted_iota(jnp.int32, (rows, rows), 1)
               < lax.broadcasted_iota(jnp.int32, (rows, rows), 0)).astype(BF16)
    before = _dot(earlier, both.astype(BF16)) + cnt_ref[...]
    rank1 = jnp.sum(oh1 * before, axis=-1, keepdims=True)
    rank2 = jnp.sum(oh2 * before, axis=-1, keepdims=True)
    cnt_ref[...] += jnp.sum(both, axis=0, keepdims=True)
    count_ref[...] = jnp.broadcast_to(cnt_ref[...], count_ref.shape)

    out = jnp.zeros_like(lg)
    for slot, val in ((ROUTE_E1, i1 - N_GROUPS), (ROUTE_E2, i2 - N_GROUPS), (ROUTE_RANK1, rank1),
                      (ROUTE_RANK2, rank2), (ROUTE_W1, w1), (ROUTE_W2, w2)):
        out = jnp.where(lane == slot, val, out)
    route_ref[...] = out


def _route(logits):
    n = logits.shape[0]
    return pl.pallas_call(
        _route_kernel,
        grid=(n // ROUTE_TM,),
        in_specs=[pl.BlockSpec((ROUTE_TM, LANES), lambda i: (i, 0))],
        out_specs=[pl.BlockSpec((ROUTE_TM, LANES), lambda i: (i, 0)), pl.BlockSpec((8, LANES), lambda i: (0, 0))],
        out_shape=[jax.ShapeDtypeStruct((n, LANES), F32), jax.ShapeDtypeStruct((8, LANES), F32)],
        scratch_shapes=[pltpu.VMEM((1, LANES), F32)],
        compiler_params=_params(("arbitrary",)),
        name="route",
    )(logits)


GROUP = 8


def _dispatch_plan(route, counts):
    n = route.shape[0]
    counts = counts[0, N_GROUPS:N_GROUPS + N_EXPERTS].astype(jnp.int32)
    padded = (counts + EXPERT_BLOCK - 1) // EXPERT_BLOCK * EXPERT_BLOCK
    pad_end = jnp.cumsum(padded)
    pad_start = pad_end - padded
    e = route[:, ROUTE_E1:ROUTE_E2 + 1].astype(jnp.int32)
    rank = route[:, ROUTE_RANK1:ROUTE_RANK2 + 1].astype(jnp.int32)
    start_of = jnp.sum(jnp.where(e[..., None] == jnp.arange(N_EXPERTS), pad_start, 0), axis=-1)
    dest = (start_of + rank).reshape(-1)
    n_blocks = (2 * n + N_EXPERTS * (EXPERT_BLOCK - 1) + EXPERT_BLOCK - 1) // EXPERT_BLOCK
    n_rows = n_blocks * EXPERT_BLOCK
    assign = jnp.full((n_rows,), -1, jnp.int32).at[dest].set(jnp.arange(2 * n, dtype=jnp.int32), unique_indices=True)
    block = jnp.arange(n_blocks, dtype=jnp.int32)
    block_expert = jnp.minimum(jnp.sum(block[:, None] * EXPERT_BLOCK >= pad_end[None, :], axis=1), N_EXPERTS - 1)
    valid = jnp.clip(jnp.sum(jnp.where(block_expert[:, None] == jnp.arange(N_EXPERTS), pad_start + counts, 0), axis=1)
                     - block * EXPERT_BLOCK, 0, EXPERT_BLOCK)
    copies = (valid + GROUP - 1) // GROUP * GROUP
    used_blocks = (pad_end[-1] // EXPERT_BLOCK).reshape(1)
    row = jnp.arange(n_rows, dtype=jnp.int32)
    spare = 2 * n + ((row // EXPERT_BLOCK) % 2) * GROUP + row % GROUP
    src_token = jnp.where(assign < 0, 0, assign // 2).reshape(n_blocks, 1, EXPERT_BLOCK)
    dst_slot = jnp.where(assign < 0, spare, assign).reshape(n_blocks, 1, EXPERT_BLOCK)
    return (src_token, dst_slot, block_expert.astype(jnp.int32), copies.astype(jnp.int32),
            used_blocks.astype(jnp.int32))


def _tile_rows(i):
    return pl.ds(pl.multiple_of(i * ROW_TILE, ROW_TILE), ROW_TILE)


def _expert_kernel(be_ref, copies_ref, used_ref, src_ref, src_next_ref, dst_ref, h_ref, wg_ref, wu_ref, wd_ref,
                   slots_ref, x_buf, y_buf, wg_s, wu_s, wd_s, gather_sem, scatter_sem):
    i = pl.program_id(0)
    used = used_ref[0]
    cur = i % 2

    def gather_copy(idx_ref, r, buf):
        return pltpu.make_async_copy(h_ref.at[_tile_rows(idx_ref[0, 0, r])], x_buf.at[buf, _tile_rows(r)],
                                     gather_sem.at[buf])

    def scatter_copy(r, buf):
        return pltpu.make_async_copy(y_buf.at[buf, _tile_rows(r)], slots_ref.at[_tile_rows(dst_ref[0, 0, r])],
                                     scatter_sem.at[buf])

    def for_each_group(block, fn):
        def body(g, carry):
            for j in range(GROUP):
                fn(g * GROUP + j, j % 2)
            return carry
        lax.fori_loop(0, copies_ref[block] // GROUP, body, 0)

    @pl.when(i == 0)
    def _():
        x_buf[...] = jnp.zeros_like(x_buf)
        spare_rows = 2 * GROUP * ROW_TILE
        zero_spare = pltpu.make_async_copy(x_buf.at[0, pl.ds(0, spare_rows)],
                                           slots_ref.at[pl.ds(slots_ref.shape[0] - spare_rows, spare_rows)],
                                           scatter_sem.at[0])
        zero_spare.start()
        zero_spare.wait()
        for_each_group(0, lambda r, p: gather_copy(src_ref, r, 0).start(priority=p))

    @pl.when(i + 1 < used)
    def _():
        for_each_group(i + 1, lambda r, p: gather_copy(src_next_ref, r, 1 - cur).start(priority=p))

    @pl.when(jnp.logical_or(i == 0, be_ref[i] != be_ref[jnp.maximum(i - 1, 0)]))
    def _():
        wg_s[...] = wg_ref[0, 0].astype(BF16)
        wu_s[...] = wu_ref[0, 0].astype(BF16)
        wd_s[...] = wd_ref[0, 0].astype(BF16)

    @pl.when(i < used)
    def _():
        for_each_group(i, lambda r, p: gather_copy(src_ref, 0, cur).wait())
        x = jnp.concatenate([_load_token_slab(x_buf.at[cur], EXPERT_BLOCK, ROW_TILE, c) for c in range(ROW_TILE)],
                            axis=1).astype(BF16)
        hidden = _silu(_dot(x, wg_s[...])) * _dot(x, wu_s[...])
        _store_token_tiles(y_buf.at[cur], _dot(hidden.astype(BF16), wd_s[...]), ROW_TILE)
        for_each_group(i, lambda r, p: scatter_copy(r, cur).start(priority=p))

        @pl.when(i > 0)
        def _():
            for_each_group(i - 1, lambda r, p: scatter_copy(0, 1 - cur).wait())

        @pl.when(i == used - 1)
        def _():
            for_each_group(i, lambda r, p: scatter_copy(0, cur).wait())


def _experts(h2_tiles, plan, w_gate, w_up, w_down, layer):
    src_token, dst_slot, block_expert, copies, used_blocks = plan
    n_blocks = src_token.shape[0]
    n_slots = 2 * (h2_tiles.shape[0] // ROW_TILE) + 2 * GROUP
    last = n_blocks - 1

    def smem_block(index):
        return pl.BlockSpec((1, 1, EXPERT_BLOCK), index, memory_space=pltpu.SMEM)

    def weight(shape):
        return pl.BlockSpec((1, 1) + shape, lambda i, be, copies, used: (layer, be[i], 0, 0))

    grid_spec = pltpu.PrefetchScalarGridSpec(
        num_scalar_prefetch=3,
        grid=(n_blocks,),
        in_specs=[
            smem_block(lambda i, be, copies, used: (i, 0, 0)),
            smem_block(lambda i, be, copies, used: (jnp.minimum(i + 1, last), 0, 0)),
            smem_block(lambda i, be, copies, used: (i, 0, 0)),
            pl.BlockSpec(memory_space=pl.ANY),
            weight((D_MODEL, EXPERT_FF)), weight((D_MODEL, EXPERT_FF)), weight((EXPERT_FF, D_MODEL)),
        ],
        out_specs=pl.BlockSpec(memory_space=pl.ANY),
        scratch_shapes=[
            pltpu.VMEM((2, EXPERT_BLOCK * ROW_TILE, LANES), F32),
            pltpu.VMEM((2, EXPERT_BLOCK * ROW_TILE, LANES), F32),
            pltpu.VMEM((D_MODEL, EXPERT_FF), BF16),
            pltpu.VMEM((D_MODEL, EXPERT_FF), BF16),
            pltpu.VMEM((EXPERT_FF, D_MODEL), BF16),
            pltpu.SemaphoreType.DMA((2,)),
            pltpu.SemaphoreType.DMA((2,)),
        ],
    )
    return pl.pallas_call(
        _expert_kernel,
        grid_spec=grid_spec,
        out_shape=jax.ShapeDtypeStruct((n_slots * ROW_TILE, LANES), F32),
        compiler_params=_params(("arbitrary",), VMEM_LIMIT),
        name="experts",
    )(block_expert, copies, used_blocks, src_token, src_token, dst_slot, h2_tiles, w_gate, w_up, w_down)


def _final_kernel(slots_ref, x_ref, route_ref, mod_ref, fg_ref, o_ref):
    sq = _moe_residual(slots_ref, x_ref, route_ref, mod_ref, o_ref)
    o_ref[...] = o_ref[...] * lax.rsqrt(sq / D_MODEL + EPS) * fg_ref[...]


def _final_combine(slots, x_all, route, mod, final_g, n_batch):
    tiles = TOK // TM
    lat_tiles = SEQ // TM
    tok = lambda b, j: (b * tiles + CTX_LEN // TM + j, 0)
    return pl.pallas_call(
        _final_kernel,
        grid=(n_batch, lat_tiles),
        in_specs=[
            pl.BlockSpec((TM * 2 * ROW_TILE, LANES), tok),
            pl.BlockSpec((TM, D_MODEL), tok),
            pl.BlockSpec((TM, LANES), tok),
            pl.BlockSpec((1, 6, D_MODEL), lambda b, j: (b, 0, 0)),
            pl.BlockSpec((1, D_MODEL), lambda b, j: (0, 0)),
        ],
        out_specs=pl.BlockSpec((TM, D_MODEL), lambda b, j: (b * lat_tiles + j, 0)),
        out_shape=jax.ShapeDtypeStruct((n_batch * SEQ, D_MODEL), F32),
        compiler_params=_params(("parallel", "parallel")),
        name="final_combine",
    )(slots, x_all, route, mod, final_g.reshape(1, D_MODEL))


def _rope_tables():
    t = jnp.arange(SEQ)
    rows = (t // GRID_W).astype(F32)
    cols = (t % GRID_W).astype(F32)
    n_freq = HEAD_DIM // 4
    inv_freq = ROPE_BASE ** (-jnp.arange(n_freq, dtype=F32) / n_freq)
    ang_r = rows[:, None] * inv_freq
    ang_c = cols[:, None] * inv_freq
    cos = jnp.concatenate([jnp.cos(ang_r)] * 2 + [jnp.cos(ang_c)] * 2, axis=1)
    sin = jnp.concatenate([-jnp.sin(ang_r), jnp.sin(ang_r), -jnp.sin(ang_c), jnp.sin(ang_c)], axis=1)
    return cos, sin


def kernel(x, c, ctx, c_ctx, w_mod, b_mod, norm1_g, norm2_g, w_in, ret_decay, ret_norm_g, conv_w, conv_b,
           mlstm_gate_b, mlstm_norm_g, na_rpb, w_branch, w_out, w_group, w_router, w_expert_gate,
           w_expert_up, w_expert_down, final_norm_g):
    n_batch, seq, d = x.shape
    depth = w_mod.shape[0]
    assert (seq, d, ctx.shape[1]) == (SEQ, D_MODEL, CTX_LEN)
    n = n_batch * TOK

    cond_rows = -(-(n_batch + 1) // 8) * 8
    cond = jnp.zeros((cond_rows, d), F32).at[:n_batch].set(c).at[n_batch].set(c_ctx)
    mod_all = _modulation(cond, w_mod, b_mod).reshape(depth, cond_rows, 6, d)

    cos, sin = _rope_tables()
    x_all = jnp.concatenate([ctx, x], axis=1).reshape(n, d)

    w_packed = _pack_in_weights(w_in)
    w_branch_b = w_branch.astype(BF16)
    w_out_b = w_out.astype(BF16)
    w_route = jnp.concatenate([w_group, w_router], axis=-1)
    w_route = jnp.pad(w_route, ((0, 0), (0, 0), (0, LANES - w_route.shape[-1])))
    wr_hi = w_route.astype(BF16)
    wr_lo = (w_route - wr_hi.astype(F32)).astype(BF16)
    gate_b = jnp.pad(mlstm_gate_b.reshape(depth, 1, 4 * N_HEADS), ((0, 0), (0, 0), (0, LANES - 4 * N_HEADS)))
    na_bias = _na_bias_tables(na_rpb)

    moe = None
    for layer in range(depth):
        mod = mod_all[layer]
        if moe is None:
            ret, ml, gates, na, bg = _in_projection(x_all, mod, norm1_g, w_packed, layer, n_batch)
        else:
            x_all, ret, ml, gates, na, bg = _in_projection(x_all, mod, norm1_g, w_packed, layer, n_batch, moe)
        r_out = _retention(ret, ret_decay[layer], ret_norm_g[layer], cos, sin, n_batch)
        m_out = _mlstm(ml, gates, gate_b[layer], conv_w[layer], conv_b[layer].reshape(1, 2 * WIDTH),
                       mlstm_norm_g[layer], n_batch)
        a_out = _neighbourhood_attention(na, na_bias, layer, n_batch)
        x_all, h2, logits = _merge(r_out, m_out, a_out, bg, x_all, mod, w_branch_b, w_out_b, norm2_g,
                                   wr_hi, wr_lo, layer, n_batch)
        route, counts = _route(logits)
        plan = _dispatch_plan(route, counts)
        slots = _experts(h2, plan, w_expert_gate, w_expert_up, w_expert_down, layer)
        moe = (slots, route, mod)

    return _final_combine(slots, x_all, route, mod, final_norm_g, n_batch).reshape(n_batch, SEQ, d)
```

```python
import functools

import numpy as np
import jax
import jax.numpy as jnp
from jax import lax
from jax.experimental import pallas as pl
from jax.experimental.pallas import tpu as pltpu

F32 = jnp.float32
BF16 = jnp.bfloat16

D_MODEL = 1024
SEQ = 2048
CTX_LEN = 256
TOK = CTX_LEN + SEQ
GRID_W = 64
GRID_ROWS = SEQ // GRID_W
HEAD_DIM = 128
N_HEADS = 4
WIDTH = N_HEADS * HEAD_DIM
CHUNK = 256
N_CHUNKS = TOK // CHUNK
CTX_CHUNKS = CTX_LEN // CHUNK
CONV_WIDTH = 5
NA_WIN_ROWS = 8
NA_WIN_COLS = 16
NA_Q_ROWS = 4
NA_Q = NA_Q_ROWS * GRID_W
NA_K_ROWS = NA_Q_ROWS + NA_WIN_ROWS
NA_K = NA_K_ROWS * GRID_W
NA_STEPS = SEQ // NA_Q
ROPE_BASE = 10000.0
N_GROUPS = 4
EXPERTS_PER_GROUP = 8
N_EXPERTS = N_GROUPS * EXPERTS_PER_GROUP
EXPERT_FF = 512
EXPERT_BLOCK = 256
EPS = 1e-6
NEG_INF = -1e30
QK_SCALE = HEAD_DIM ** -0.5

TM = 256
ROUTE_TM = 512
LANES = 128
VMEM_LIMIT = 56 * 1024 * 1024


def _dot(a, b):
    return jnp.dot(a, b, preferred_element_type=F32)


def _dot_nt(a, b):
    return lax.dot_general(a, b, (((1,), (1,)), ((), ())), preferred_element_type=F32)


def _bdot(a, b):
    return _dot(a.astype(BF16), b.astype(BF16))


def _bdot_nt(a, b):
    return _dot_nt(a.astype(BF16), b.astype(BF16))


def _bdot_tn(a, b):
    return _dot(a.T.astype(BF16), b.astype(BF16))


def _log_sigmoid(x):
    return jnp.minimum(x, 0.0) - jnp.log1p(jnp.exp(-jnp.abs(x)))


def _silu(x):
    return x * jax.nn.sigmoid(x)


def _params(sem, vmem=None):
    return pltpu.CompilerParams(dimension_semantics=sem, vmem_limit_bytes=vmem)


def _mod_row(n_batch):
    tiles = TOK // TM

    def index(i):
        return (jnp.where(i % tiles == 0, n_batch, i // tiles), 0, 0)

    return index


def _mod_kernel(c_ref, w_ref, b_ref, o_ref):
    cond = _silu(c_ref[...])
    o_ref[0] = _bdot(cond, w_ref[0]) + b_ref[0]


def _modulation(cond, w_mod, b_mod):
    depth, d, cols = w_mod.shape
    rows = cond.shape[0]
    tn = 1536
    return pl.pallas_call(
        _mod_kernel,
        grid=(depth, cols // tn),
        in_specs=[
            pl.BlockSpec((rows, d), lambda l, j: (0, 0)),
            pl.BlockSpec((1, d, tn), lambda l, j: (l, 0, j)),
            pl.BlockSpec((1, 1, tn), lambda l, j: (l, 0, j)),
        ],
        out_specs=pl.BlockSpec((1, rows, tn), lambda l, j: (l, 0, j)),
        out_shape=jax.ShapeDtypeStruct((depth, rows, cols), F32),
        compiler_params=_params(("parallel", "parallel")),
        name="modulation",
    )(cond, w_mod, b_mod.reshape(depth, 1, cols))


def _rms_modulate(x, g, shift, scale):
    y = x * lax.rsqrt(jnp.mean(x * x, axis=-1, keepdims=True) + EPS) * g
    return y * (1.0 + scale) + shift


IN_WIDTHS = (4 * WIDTH, 4 * WIDTH, LANES, 3 * WIDTH, 3 * D_MODEL)
IN_OFFSETS = tuple(int(v) for v in np.cumsum((0,) + IN_WIDTHS))


def _pack_kernel(w_ref, o_ref):
    gate0 = IN_OFFSETS[2]
    n_gate = 4 * N_HEADS
    o_ref[0, :, :gate0] = w_ref[0, :, :gate0].astype(BF16)
    window = w_ref[0, :, gate0:gate0 + LANES]
    lane = lax.broadcasted_iota(jnp.int32, window.shape, 1)
    o_ref[0, :, gate0:gate0 + LANES] = jnp.where(lane < n_gate, window, 0.0).astype(BF16)
    o_ref[0, :, gate0 + LANES:] = w_ref[0, :, gate0 + n_gate:].astype(BF16)


def _pack_in_weights(w_in):
    depth, d, cols = w_in.shape
    rows = 256
    return pl.pallas_call(
        _pack_kernel,
        grid=(depth, d // rows),
        in_specs=[pl.BlockSpec((1, rows, cols), lambda l, r: (l, r, 0))],
        out_specs=pl.BlockSpec((1, rows, IN_OFFSETS[-1]), lambda l, r: (l, r, 0)),
        out_shape=jax.ShapeDtypeStruct((depth, d, IN_OFFSETS[-1]), BF16),
        compiler_params=_params(("parallel", "parallel"), VMEM_LIMIT),
        name="pack_in_weights",
    )(w_in)


def _project(x, mod_ref, g_ref, w_ref, out_refs):
    h = _rms_modulate(x, g_ref[0], mod_ref[0, 0:1, :], mod_ref[0, 1:2, :]).astype(BF16)
    for o_ref, lo, hi in zip(out_refs, IN_OFFSETS[:-1], IN_OFFSETS[1:]):
        o_ref[...] = _dot(h, w_ref[0, :, lo:hi])


def _inproj_kernel(x_ref, mod_ref, g_ref, w_ref, *out_refs):
    _project(x_ref[...], mod_ref, g_ref, w_ref, out_refs)


def _combine_inproj_kernel(slots_ref, x_ref, route_ref, prev_mod_ref, mod_ref, g_ref, w_ref, x_out_ref, *out_refs):
    _moe_residual(slots_ref, x_ref, route_ref, prev_mod_ref, x_out_ref)
    _project(x_out_ref[...], mod_ref, g_ref, w_ref, out_refs)


def _in_projection(x_all, mod, norm_g, w_packed, layer, n_batch, moe=None):
    n = x_all.shape[0]
    row = lambda i: (i, 0)
    in_specs = [
        pl.BlockSpec((TM, D_MODEL), row),
        pl.BlockSpec((1, 6, D_MODEL), _mod_row(n_batch)),
        pl.BlockSpec((1, 1, D_MODEL), lambda i: (layer, 0, 0)),
        pl.BlockSpec((1,) + w_packed.shape[1:], lambda i: (layer, 0, 0), pipeline_mode=pl.Buffered(1)),
    ]
    out_specs = [pl.BlockSpec((TM, w), row) for w in IN_WIDTHS]
    out_shape = [jax.ShapeDtypeStruct((n, w), F32) for w in IN_WIDTHS]
    args = (x_all, mod, norm_g.reshape(-1, 1, D_MODEL), w_packed)
    if moe is None:
        body = _inproj_kernel
    else:
        slots, route, prev_mod = moe
        body = _combine_inproj_kernel
        in_specs = ([pl.BlockSpec((TM * 2 * ROW_TILE, LANES), row), in_specs[0], pl.BlockSpec((TM, LANES), row),
                     pl.BlockSpec((1, 6, D_MODEL), _mod_row(n_batch))] + in_specs[1:])
        out_specs = [pl.BlockSpec((TM, D_MODEL), row)] + out_specs
        out_shape = [jax.ShapeDtypeStruct((n, D_MODEL), F32)] + out_shape
        args = (slots, x_all, route, prev_mod) + args[1:]
    return pl.pallas_call(
        body,
        grid=(n // TM,),
        in_specs=in_specs,
        out_specs=out_specs,
        out_shape=out_shape,
        compiler_params=_params(("parallel",), VMEM_LIMIT),
        name="in_projection",
    )(*args)


def _chunk_order(t):
    fwd = t
    bwd = jnp.where(t < CTX_CHUNKS, CTX_CHUNKS - 1 - t, N_CHUNKS + CTX_CHUNKS - 1 - t)
    return fwd, bwd


def _chunk_slice(c):
    return pl.ds(pl.multiple_of(c * CHUNK, CHUNK), CHUNK)


def _head_norm(y, gain):
    mu = jnp.mean(y, axis=-1, keepdims=True)
    yc = y - mu
    var = jnp.mean(yc * yc, axis=-1, keepdims=True)
    return yc * lax.rsqrt(var + EPS) * gain


ROW_TILE = D_MODEL // LANES


def _store_token_tiles(ref, x, pitch, offset=0):
    rows = x.shape[0]
    for c in range(ROW_TILE):
        ref[pl.ds(offset + c, rows, stride=pitch), :] = x[:, c * LANES:(c + 1) * LANES]


def _load_token_slab(ref, rows, pitch, c, offset=0):
    return ref[pl.ds(offset + c, rows, stride=pitch), :]


def _moe_residual(slots_ref, x_ref, route_ref, mod_ref, o_ref):
    route = route_ref[...]
    w1 = route[:, ROUTE_W1:ROUTE_W1 + 1]
    w2 = route[:, ROUTE_W2:ROUTE_W2 + 1]
    rows = x_ref.shape[0]
    sq = jnp.zeros((rows, 1), F32)
    for c in range(ROW_TILE):
        lanes = slice(c * LANES, (c + 1) * LANES)
        y1 = _load_token_slab(slots_ref, rows, 2 * ROW_TILE, c)
        y2 = _load_token_slab(slots_ref, rows, 2 * ROW_TILE, c, offset=ROW_TILE)
        x_new = x_ref[:, lanes] + mod_ref[0, 5:6, lanes] * (y1 * w1 + y2 * w2)
        o_ref[:, lanes] = x_new
        sq = sq + jnp.sum(x_new * x_new, axis=-1, keepdims=True)
    return sq


def _select_lane(x, lane, idx):
    return jnp.sum(jnp.where(lane == idx, x, 0.0), axis=-1, keepdims=True)


HEADS_PER_STEP = 2
HEAD_STEPS = N_HEADS // HEADS_PER_STEP
STEP_WIDTH = HEADS_PER_STEP * HEAD_DIM
SCAN_VMEM = 48 * 1024 * 1024


def _head_lanes(hh):
    return slice(hh * HEAD_DIM, (hh + 1) * HEAD_DIM)


def _retention_kernel(dec_ref, q_ref, k_ref, v_ref, g_ref, cos_ref, sin_ref, gn_ref, o_ref,
                      qs_ref, ks_ref, ob_ref, intra_ref, st_ref):
    first_head = pl.program_id(1) * HEADS_PER_STEP

    lane = lax.broadcasted_iota(jnp.int32, (CHUNK, HEAD_DIM), 1)
    first_half = (lane % (HEAD_DIM // 2)) < (HEAD_DIM // 4)

    def rope(x, cos, sin):
        rot = jnp.where(first_half, pltpu.roll(x, HEAD_DIM - HEAD_DIM // 4, 1), pltpu.roll(x, HEAD_DIM // 4, 1))
        return x * cos + rot * sin

    qs_ref[0:CTX_LEN, :] = q_ref[0:CTX_LEN, :]
    ks_ref[0:CTX_LEN, :] = k_ref[0:CTX_LEN, :] * QK_SCALE

    def rotate_chunk(c, carry):
        rows = _chunk_slice(c)
        pos = _chunk_slice(c - CTX_CHUNKS)
        cos = cos_ref[pos, :]
        sin = sin_ref[pos, :]
        for hh in range(HEADS_PER_STEP):
            qs_ref[rows, _head_lanes(hh)] = rope(q_ref[rows, _head_lanes(hh)], cos, sin)
            ks_ref[rows, _head_lanes(hh)] = rope(k_ref[rows, _head_lanes(hh)], cos, sin) * QK_SCALE
        return carry

    lax.fori_loop(CTX_CHUNKS, N_CHUNKS, rotate_chunk, 0)

    ii = lax.broadcasted_iota(jnp.int32, (CHUNK, CHUNK), 0).astype(F32)
    jj = lax.broadcasted_iota(jnp.int32, (CHUNK, CHUNK), 1).astype(F32)
    col = lax.broadcasted_iota(jnp.int32, (CHUNK, 1), 0).astype(F32)

    def decay_mat(dist, lg_dir):
        ok = dist >= 0
        return jnp.where(ok, jnp.exp(jnp.where(ok, dist, 0.0) * lg_dir), 0.0)

    dec = _log_sigmoid(dec_ref[...])
    hl = lax.broadcasted_iota(jnp.int32, dec.shape, 1)
    consts = []
    for hh in range(HEADS_PER_STEP):
        lg = jnp.sum(jnp.where(hl == first_head + hh, dec, 0.0), axis=-1, keepdims=True)
        lg_f, lg_b = lg[0:1, :], lg[1:2, :]
        intra_ref[2 * hh] = decay_mat(ii - jj, lg_f)
        intra_ref[2 * hh + 1] = decay_mat(jj - ii, lg_b)
        consts.append(dict(
            q_decay=(jnp.exp((col + 1.0) * lg_f), jnp.exp((CHUNK - col) * lg_b)),
            k_decay=(jnp.exp((CHUNK - 1.0 - col) * lg_f), jnp.exp(col * lg_b)),
            chunk_decay=(jnp.exp(CHUNK * lg_f), jnp.exp(CHUNK * lg_b))))

    st_ref[...] = jnp.zeros_like(st_ref)
    out_refs = (o_ref, ob_ref)

    def step(t, carry):
        for hh in range(HEADS_PER_STEP):
            cst = consts[hh]
            for d, c in enumerate(_chunk_order(t)):
                rows = _chunk_slice(c)
                q = qs_ref[rows, _head_lanes(hh)]
                k = ks_ref[rows, _head_lanes(hh)]
                v = v_ref[rows, _head_lanes(hh)]
                s_prev = st_ref[2 * hh + d]
                scores = _bdot_nt(q, k) * intra_ref[2 * hh + d]
                out_refs[d][rows, _head_lanes(hh)] = _bdot(scores, v) + _bdot(q * cst["q_decay"][d], s_prev)
                st_ref[2 * hh + d] = s_prev * cst["chunk_decay"][d] + _bdot_tn(k * cst["k_decay"][d], v)
        return carry

    lax.fori_loop(0, N_CHUNKS, step, 0)

    def finish_chunk(c, carry):
        rows = _chunk_slice(c)
        for hh in range(HEADS_PER_STEP):
            lanes = _head_lanes(hh)
            y = o_ref[rows, lanes] + ob_ref[rows, lanes]
            o_ref[rows, lanes] = _head_norm(y, gn_ref[:, lanes]) * _silu(g_ref[rows, lanes])
        return carry

    lax.fori_loop(0, N_CHUNKS, finish_chunk, 0)


def _retention(ret, ret_decay, norm_g, cos, sin, n_batch):
    n = ret.shape[0]

    def head_block(offset):
        return pl.BlockSpec((TOK, STEP_WIDTH), lambda b, h: (b, offset * HEAD_STEPS + h))

    return pl.pallas_call(
        _retention_kernel,
        grid=(n_batch, HEAD_STEPS),
        in_specs=[
            pl.BlockSpec(ret_decay.shape, lambda b, h: (0, 0)),
            head_block(0), head_block(1), head_block(2), head_block(3),
            pl.BlockSpec((SEQ, HEAD_DIM), lambda b, h: (0, 0)),
            pl.BlockSpec((SEQ, HEAD_DIM), lambda b, h: (0, 0)),
            pl.BlockSpec((1, STEP_WIDTH), lambda b, h: (0, h)),
        ],
        out_specs=pl.BlockSpec((TOK, STEP_WIDTH), lambda b, h: (b, h)),
        out_shape=jax.ShapeDtypeStruct((n, WIDTH), F32),
        scratch_shapes=[
            pltpu.VMEM((TOK, STEP_WIDTH), F32),
            pltpu.VMEM((TOK, STEP_WIDTH), F32),
            pltpu.VMEM((TOK, STEP_WIDTH), F32),
            pltpu.VMEM((2 * HEADS_PER_STEP, CHUNK, CHUNK), F32),
            pltpu.VMEM((2 * HEADS_PER_STEP, HEAD_DIM, HEAD_DIM), F32),
        ],
        compiler_params=_params(("parallel", "parallel"), SCAN_VMEM),
        name="retention",
    )(ret_decay, ret, ret, ret, ret, cos, sin, norm_g.reshape(1, WIDTH))


CONV_PAD = 8
CONV_ROWS = 128


def _mlstm_kernel(q_ref, k_ref, v_ref, og_ref, gate_ref, gb_ref, cwq_ref, cwk_ref, cbq_ref, cbk_ref,
                  gn_ref, o_ref, qs_ref, ks_ref, ob_ref, pad_ref, gx_ref, gxt_ref, cst_ref):
    first_head = pl.program_id(1) * HEADS_PER_STEP

    pad_ref[0:CONV_PAD, :] = jnp.zeros((CONV_PAD, STEP_WIDTH), F32)
    pad_ref[CONV_PAD + TOK:, :] = jnp.zeros((CONV_PAD, STEP_WIDTH), F32)
    crow = lax.broadcasted_iota(jnp.int32, (CONV_ROWS, 1), 0)

    def conv(u_ref, w_ref, b_ref, dst_ref, scale):
        def fill(c, carry):
            rows = _chunk_slice(c)
            pad_ref[pl.ds(pl.multiple_of(c * CHUNK + CONV_PAD, CONV_PAD), CHUNK), :] = u_ref[rows, :]
            return carry

        lax.fori_loop(0, N_CHUNKS, fill, 0)

        for blk in range(TOK // CONV_ROWS):
            first = blk * CONV_ROWS
            acc = jnp.zeros((CONV_ROWS, STEP_WIDTH), F32)
            for j in range(CONV_WIDTH):
                shift = j - CONV_WIDTH // 2
                tap = pad_ref[first + CONV_PAD + shift:first + CONV_PAD + shift + CONV_ROWS, :]
                if (first + CONV_ROWS == CTX_LEN and shift > 0) or (first == CTX_LEN and shift < 0):
                    trow = crow + first
                    tap = jnp.where((trow < CTX_LEN) == (trow + shift < CTX_LEN), tap, 0.0)
                acc = acc + tap * w_ref[j:j + 1, :]
            dst_ref[first:first + CONV_ROWS, :] = _silu(acc + b_ref[...]) * scale

    conv(q_ref, cwq_ref, cbq_ref, qs_ref, 1.0)
    conv(k_ref, cwk_ref, cbk_ref, ks_ref, QK_SCALE)

    clane = lax.broadcasted_iota(jnp.int32, (CHUNK, LANES), 1)
    is_forget = ((clane // N_HEADS) % 2) == 1

    def gate_chunk(c, carry):
        rows = _chunk_slice(c)
        g = gate_ref[rows, :] + gb_ref[...]
        gx = jnp.where(is_forget, _log_sigmoid(g), g)
        gx_ref[rows, :] = gx
        gxt_ref[c] = gx.T
        return carry

    lax.fori_loop(0, N_CHUNKS, gate_chunk, 0)

    ii = lax.broadcasted_iota(jnp.int32, (CHUNK, CHUNK), 0)
    jj = lax.broadcasted_iota(jnp.int32, (CHUNK, CHUNK), 1)
    causal = (jj <= ii, jj >= ii)

    cst_ref[...] = jnp.zeros_like(cst_ref)
    out_refs = (o_ref, ob_ref)

    def step(t, carry):
        new_carry = []
        for hh in range(HEADS_PER_STEP):
            head = first_head + hh
            lanes = _head_lanes(hh)
            for d, c in enumerate(_chunk_order(t)):
                slot = 2 * hh + d
                n_prev, m_prev = carry[2 * slot], carry[2 * slot + 1]
                rows = _chunk_slice(c)
                q = qs_ref[rows, lanes]
                k = ks_ref[rows, lanes]
                v = v_ref[rows, lanes]
                gc = gx_ref[rows, :]
                i_col = _select_lane(gc, clane, 2 * d * N_HEADS + head)
                f_col = _select_lane(gc, clane, (2 * d + 1) * N_HEADS + head)
                i_row = gxt_ref[c, pl.ds(2 * d * N_HEADS + head, 1), :]
                f_row = gxt_ref[c, pl.ds((2 * d + 1) * N_HEADS + head, 1), :]
                vis = causal[d]
                cum_col = jnp.sum(jnp.where(vis, f_row, 0.0), axis=1, keepdims=True)
                cum_row = jnp.sum(jnp.where(causal[1 - d], f_col, 0.0), axis=0, keepdims=True)
                total = jnp.sum(f_row, axis=1, keepdims=True)
                c_prev = cst_ref[slot]

                log_kw = total - cum_col + i_col
                m_new = jnp.maximum(total + m_prev, jnp.max(log_kw, axis=0, keepdims=True))
                kw = jnp.exp(log_kw - m_new)
                pw = jnp.exp(total + m_prev - m_new)
                cst_ref[slot] = pw * c_prev + _bdot_tn(k * kw, v)
                n_new = pw * n_prev + jnp.sum(kw * k, axis=0, keepdims=True)

                log_w = jnp.where(vis, cum_col - cum_row + i_row, -jnp.inf)
                log_p = cum_col + m_prev
                m_t = jnp.maximum(log_p, jnp.max(log_w, axis=1, keepdims=True))
                w = jnp.exp(log_w - m_t)
                p = jnp.exp(log_p - m_t)
                qk = _bdot_nt(q, k) * w
                num = _bdot(qk, v) + p * _bdot(q, c_prev)
                den = jnp.sum(qk, axis=1, keepdims=True) + p * jnp.sum(q * n_prev, axis=1, keepdims=True)
                out_refs[d][rows, lanes] = num / jnp.maximum(jnp.abs(den), jnp.exp(-m_t))
                new_carry += [n_new, m_new]
        return tuple(new_carry)

    zero_n = jnp.zeros((1, HEAD_DIM), F32)
    zero_m = jnp.zeros((1, 1), F32)
    lax.fori_loop(0, N_CHUNKS, step, (zero_n, zero_m) * (2 * HEADS_PER_STEP))

    def finish_chunk(c, carry):
        rows = _chunk_slice(c)
        for hh in range(HEADS_PER_STEP):
            lanes = _head_lanes(hh)
            gated = (o_ref[rows, lanes] + ob_ref[rows, lanes]) * jax.nn.sigmoid(og_ref[rows, lanes])
            o_ref[rows, lanes] = _head_norm(gated, gn_ref[:, lanes])
        return carry

    lax.fori_loop(0, N_CHUNKS, finish_chunk, 0)


def _mlstm(ml, gates, gate_b, conv_w, conv_b, norm_g, n_batch):
    n = ml.shape[0]

    def head_block(offset):
        return pl.BlockSpec((TOK, STEP_WIDTH), lambda b, h: (b, offset * HEAD_STEPS + h))

    def head_cols(rows, offset):
        return pl.BlockSpec((rows, STEP_WIDTH), lambda b, h: (0, offset * HEAD_STEPS + h))

    return pl.pallas_call(
        _mlstm_kernel,
        grid=(n_batch, HEAD_STEPS),
        in_specs=[
            head_block(0), head_block(1), head_block(2), head_block(3),
            pl.BlockSpec((TOK, LANES), lambda b, h: (b, 0)),
            pl.BlockSpec((1, LANES), lambda b, h: (0, 0)),
            head_cols(CONV_WIDTH, 0), head_cols(CONV_WIDTH, 1),
            head_cols(1, 0), head_cols(1, 1),
            head_cols(1, 0),
        ],
        out_specs=pl.BlockSpec((TOK, STEP_WIDTH), lambda b, h: (b, h)),
        out_shape=jax.ShapeDtypeStruct((n, WIDTH), F32),
        scratch_shapes=[
            pltpu.VMEM((TOK, STEP_WIDTH), F32),
            pltpu.VMEM((TOK, STEP_WIDTH), F32),
            pltpu.VMEM((TOK, STEP_WIDTH), F32),
            pltpu.VMEM((TOK + 2 * CONV_PAD, STEP_WIDTH), F32),
            pltpu.VMEM((TOK, LANES), F32),
            pltpu.VMEM((N_CHUNKS, LANES, CHUNK), F32),
            pltpu.VMEM((2 * HEADS_PER_STEP, HEAD_DIM, HEAD_DIM), F32),
        ],
        compiler_params=_params(("parallel", "parallel"), SCAN_VMEM),
        name="mlstm",
    )(ml, ml, ml, ml, gates, gate_b, conv_w, conv_w, conv_b, conv_b, norm_g.reshape(1, WIDTH))


def _na_kernel(q_ref, k_ref, v_ref, bias_ref, o_ref):
    step = pl.program_id(2)

    def context_scores(hh):
        lanes = _head_lanes(hh)
        q = q_ref[:, lanes].astype(BF16)
        v_ctx = v_ref[0:CTX_LEN, lanes].astype(BF16)
        return q, v_ctx, _dot_nt(q, k_ref[0:CTX_LEN, lanes].astype(BF16)) * QK_SCALE

    @pl.when(step == 0)
    def _():
        for hh in range(HEADS_PER_STEP):
            _, v_ctx, s_ctx = context_scores(hh)
            m = jnp.max(s_ctx, axis=-1, keepdims=True)
            p = jnp.exp(s_ctx - m)
            o_ref[:, _head_lanes(hh)] = _dot(p.astype(BF16), v_ctx) / jnp.sum(p, axis=-1, keepdims=True)

    @pl.when(step > 0)
    def _():
        start = CTX_LEN + NA_Q * jnp.clip(step - 2, 0, NA_STEPS - NA_K_ROWS // NA_Q_ROWS)
        rows = pl.ds(pl.multiple_of(start, NA_Q), NA_K)
        for hh in range(HEADS_PER_STEP):
            lanes = _head_lanes(hh)
            q, v_ctx, s_ctx = context_scores(hh)
            s_loc = _dot_nt(q, k_ref[rows, lanes].astype(BF16)) * QK_SCALE + bias_ref[0, hh, 0]
            m = jnp.maximum(jnp.max(s_loc, axis=-1, keepdims=True), jnp.max(s_ctx, axis=-1, keepdims=True))
            p_loc = jnp.exp(s_loc - m)
            p_ctx = jnp.exp(s_ctx - m)
            denom = jnp.sum(p_loc, axis=-1, keepdims=True) + jnp.sum(p_ctx, axis=-1, keepdims=True)
            o = _dot(p_loc.astype(BF16), v_ref[rows, lanes].astype(BF16)) + _dot(p_ctx.astype(BF16), v_ctx)
            o_ref[:, lanes] = o / denom


def _na_row_case(case, a):
    t = np.arange(NA_K_ROWS)
    last_start = NA_K_ROWS - NA_WIN_ROWS
    return [
        (t < NA_WIN_ROWS, NA_WIN_ROWS - 1 - a),
        ((t >= a) & (t < a + NA_WIN_ROWS), NA_WIN_ROWS // 2 - 1 - a),
        (t >= last_start, NA_Q_ROWS - NA_K_ROWS + NA_WIN_ROWS - 1 - a),
    ][case]


def _na_expand_kernel(slab_ref, o_ref):
    masked = jnp.full((GRID_W, GRID_W), NEG_INF, F32)
    for case in range(3):
        for a in range(NA_Q_ROWS):
            row_ok, first = _na_row_case(case, a)
            for pair in range(NA_K_ROWS // 2):
                tiles = [slab_ref[0, first + t] if row_ok[t] else masked for t in (2 * pair, 2 * pair + 1)]
                o_ref[0, case, a * GRID_W:(a + 1) * GRID_W, pair * LANES:(pair + 1) * LANES] = (
                    jnp.concatenate(tiles, axis=1))


def _na_bias_tables(rpb):
    lead = rpb.shape[:-2]
    n_row_off, n_col_off = rpb.shape[-2:]
    qc = np.arange(GRID_W)[:, None]
    kc = np.arange(GRID_W)[None, :]
    col_start = np.clip(qc - NA_WIN_COLS // 2, 0, GRID_W - NA_WIN_COLS)
    col_ok = (kc >= col_start) & (kc < col_start + NA_WIN_COLS)
    col_idx = np.clip(kc - qc + NA_WIN_COLS - 1, 0, n_col_off - 1)
    onehot = ((col_idx[None] == np.arange(n_col_off)[:, None, None]) & col_ok[None]).astype(np.float32)
    slabs = jnp.einsum('...rc,cqk->...rqk', rpb.astype(F32), onehot, precision=lax.Precision.HIGHEST)
    slabs = jnp.where(col_ok, slabs, NEG_INF).reshape(-1, n_row_off, GRID_W, GRID_W)
    n_tables = slabs.shape[0]
    tables = pl.pallas_call(
        _na_expand_kernel,
        grid=(n_tables,),
        in_specs=[pl.BlockSpec((1, n_row_off, GRID_W, GRID_W), lambda i: (i, 0, 0, 0))],
        out_specs=pl.BlockSpec((1, 3, NA_Q, NA_K), lambda i: (i, 0, 0, 0)),
        out_shape=jax.ShapeDtypeStruct((n_tables, 3, NA_Q, NA_K), F32),
        compiler_params=_params(("parallel",)),
        name="na_bias_tables",
    )(slabs)
    return tables.reshape(*lead, 3, NA_Q, NA_K)


def _neighbourhood_attention(na, bias, layer, n_batch):
    n = na.shape[0]
    steps = 1 + NA_STEPS
    tiles = TOK // NA_Q

    def table(b, h, j):
        return (layer, h, jnp.where(j <= 1, 0, jnp.where(j == NA_STEPS, 2, 1)), 0, 0)

    return pl.pallas_call(
        _na_kernel,
        grid=(n_batch, HEAD_STEPS, steps),
        in_specs=[
            pl.BlockSpec((NA_Q, STEP_WIDTH), lambda b, h, j: (b * tiles + j, h)),
            pl.BlockSpec((TOK, STEP_WIDTH), lambda b, h, j: (b, HEAD_STEPS + h)),
            pl.BlockSpec((TOK, STEP_WIDTH), lambda b, h, j: (b, 2 * HEAD_STEPS + h)),
            pl.BlockSpec((1, HEADS_PER_STEP, 1, NA_Q, NA_K), table),
        ],
        out_specs=pl.BlockSpec((NA_Q, STEP_WIDTH), lambda b, h, j: (b * tiles + j, h)),
        out_shape=jax.ShapeDtypeStruct((n, WIDTH), F32),
        compiler_params=_params(("parallel", "parallel", "arbitrary")),
        name="neighbourhood_attention",
    )(na, na, na, bias)


def _merge_kernel(r_ref, m_ref, a_ref, bg_ref, x_ref, mod_ref, wb_ref, wo_ref, g2_ref, wr_hi_ref, wr_lo_ref,
                  x_out_ref, h2_ref, logit_ref):
    gate = jax.nn.sigmoid(bg_ref[...])
    mix = (gate[:, 0:D_MODEL] * _dot(r_ref[...].astype(BF16), wb_ref[0, 0])
           + gate[:, D_MODEL:2 * D_MODEL] * _dot(m_ref[...].astype(BF16), wb_ref[0, 1])
           + gate[:, 2 * D_MODEL:] * _dot(a_ref[...].astype(BF16), wb_ref[0, 2]))
    y = _dot(mix.astype(BF16), wo_ref[0])
    x_new = x_ref[...] + mod_ref[0, 2:3, :] * y
    x_out_ref[...] = x_new
    h2 = _rms_modulate(x_new, g2_ref[0], mod_ref[0, 3:4, :], mod_ref[0, 4:5, :])
    _store_token_tiles(h2_ref, h2, ROW_TILE)
    hi = h2.astype(BF16)
    lo = (h2 - hi.astype(F32)).astype(BF16)
    logit_ref[...] = _dot(hi, wr_hi_ref[0]) + (_dot(lo, wr_hi_ref[0]) + _dot(hi, wr_lo_ref[0]))


def _merge(r, m, a, bg, x_all, mod, w_branch, w_out, norm2_g, wr_hi, wr_lo, layer, n_batch):
    n = x_all.shape[0]
    row = lambda i: (i, 0)

    def layer_block(w):
        return pl.BlockSpec((1,) + w.shape[1:], lambda i: (layer,) + (0,) * (w.ndim - 1))

    return pl.pallas_call(
        _merge_kernel,
        grid=(n // TM,),
        in_specs=[
            pl.BlockSpec((TM, WIDTH), row), pl.BlockSpec((TM, WIDTH), row), pl.BlockSpec((TM, WIDTH), row),
            pl.BlockSpec((TM, 3 * D_MODEL), row),
            pl.BlockSpec((TM, D_MODEL), row),
            pl.BlockSpec((1, 6, D_MODEL), _mod_row(n_batch)),
            layer_block(w_branch), layer_block(w_out),
            pl.BlockSpec((1, 1, D_MODEL), lambda i: (layer, 0, 0)),
            layer_block(wr_hi), layer_block(wr_lo),
        ],
        out_specs=[pl.BlockSpec((TM, D_MODEL), row), pl.BlockSpec((TM * ROW_TILE, LANES), row),
                   pl.BlockSpec((TM, LANES), row)],
        out_shape=[jax.ShapeDtypeStruct((n, D_MODEL), F32), jax.ShapeDtypeStruct((n * ROW_TILE, LANES), F32),
                   jax.ShapeDtypeStruct((n, LANES), F32)],
        compiler_params=_params(("parallel",), VMEM_LIMIT),
        name="merge",
    )(r, m, a, bg, x_all, mod, w_branch, w_out, norm2_g.reshape(-1, 1, D_MODEL), wr_hi, wr_lo)


ROUTE_E1, ROUTE_E2, ROUTE_RANK1, ROUTE_RANK2, ROUTE_W1, ROUTE_W2 = range(6)


def _route_kernel(logit_ref, route_ref, count_ref, cnt_ref):
    @pl.when(pl.program_id(0) == 0)
    def _():
        cnt_ref[...] = jnp.zeros_like(cnt_ref)

    lg = logit_ref[...]
    lane = lax.broadcasted_iota(jnp.int32, lg.shape, 1)
    lane_f = lane.astype(F32)

    def first_argmax(vals):
        top = jnp.max(vals, axis=-1, keepdims=True)
        idx = jnp.min(jnp.where(vals == top, lane_f, float(LANES)), axis=-1, keepdims=True)
        return top, idx

    group_logits = jnp.where(lane < N_GROUPS, lg, -jnp.inf)
    g_top, g_idx = first_argmax(group_logits)
    group_w = 1.0 / jnp.sum(jnp.exp(group_logits - g_top), axis=-1, keepdims=True)

    first = N_GROUPS + EXPERTS_PER_GROUP * g_idx
    in_group = (lane_f >= first) & (lane_f < first + EXPERTS_PER_GROUP)
    expert_logits = jnp.where(in_group, lg, -jnp.inf)
    v1, i1 = first_argmax(expert_logits)
    v2, i2 = first_argmax(jnp.where(lane_f == i1, -jnp.inf, expert_logits))
    t = jnp.exp(v2 - v1)
    w1 = group_w / (1.0 + t)
    w2 = group_w * t / (1.0 + t)

    oh1 = (lane_f == i1).astype(F32)
    oh2 = (lane_f == i2).astype(F32)
    both = oh1 + oh2
    rows = lg.shape[0]
    earlier = (lax.broadcasted_iota(jnp.int32, (rows, rows), 1)
               < lax.broadcasted_iota(jnp.int32, (rows, rows), 0)).astype(BF16)
    before = _dot(earlier, both.astype(BF16)) + cnt_ref[...]
    rank1 = jnp.sum(oh1 * before, axis=-1, keepdims=True)
    rank2 = jnp.sum(oh2 * before, axis=-1, keepdims=True)
    cnt_ref[...] += jnp.sum(both, axis=0, keepdims=True)
    count_ref[...] = jnp.broadcast_to(cnt_ref[...], count_ref.shape)

    out = jnp.zeros_like(lg)
    for slot, val in ((ROUTE_E1, i1 - N_GROUPS), (ROUTE_E2, i2 - N_GROUPS), (ROUTE_RANK1, rank1),
                      (ROUTE_RANK2, rank2), (ROUTE_W1, w1), (ROUTE_W2, w2)):
        out = jnp.where(lane == slot, val, out)
    route_ref[...] = out


def _route(logits):
    n = logits.shape[0]
    return pl.pallas_call(
        _route_kernel,
        grid=(n // ROUTE_TM,),
        in_specs=[pl.BlockSpec((ROUTE_TM, LANES), lambda i: (i, 0))],
        out_specs=[pl.BlockSpec((ROUTE_TM, LANES), lambda i: (i, 0)), pl.BlockSpec((8, LANES), lambda i: (0, 0))],
        out_shape=[jax.ShapeDtypeStruct((n, LANES), F32), jax.ShapeDtypeStruct((8, LANES), F32)],
        scratch_shapes=[pltpu.VMEM((1, LANES), F32)],
        compiler_params=_params(("arbitrary",)),
        name="route",
    )(logits)


GROUP = 8


INVERT_TILE = 1024


def _invert_kernel(valid_ref, dest_ref, assign_ref):
    i = pl.program_id(0)

    @pl.when(i == 0)
    def _():
        def mark_block(b, carry):
            def mark(g, inner):
                for j in range(GROUP):
                    assign_ref[b * EXPERT_BLOCK + g * GROUP + j] = -1
                return inner
            lax.fori_loop(valid_ref[b] // GROUP, EXPERT_BLOCK // GROUP, mark, 0)
            return carry
        lax.fori_loop(0, valid_ref.shape[0], mark_block, 0)

    base = i * INVERT_TILE

    def scatter(g, carry):
        for j in range(GROUP):
            k = g * GROUP + j
            assign_ref[dest_ref[0, 0, k]] = base + k
        return carry

    lax.fori_loop(0, INVERT_TILE // GROUP, scatter, 0)


def _invert(dest, valid, n_rows):
    n_assign = dest.shape[0]
    grid_spec = pltpu.PrefetchScalarGridSpec(
        num_scalar_prefetch=1,
        grid=(n_assign // INVERT_TILE,),
        in_specs=[pl.BlockSpec((1, 1, INVERT_TILE), lambda i, valid: (i, 0, 0), memory_space=pltpu.SMEM)],
        out_specs=pl.BlockSpec(memory_space=pltpu.SMEM),
    )
    return pl.pallas_call(
        _invert_kernel,
        grid_spec=grid_spec,
        out_shape=jax.ShapeDtypeStruct((n_rows,), jnp.int32),
        compiler_params=_params(("arbitrary",)),
        name="invert_assignment",
    )(valid, dest.reshape(n_assign // INVERT_TILE, 1, INVERT_TILE))


def _dispatch_plan(route, counts):
    n = route.shape[0]
    counts = counts[0, N_GROUPS:N_GROUPS + N_EXPERTS].astype(jnp.int32)
    padded = (counts + EXPERT_BLOCK - 1) // EXPERT_BLOCK * EXPERT_BLOCK
    pad_end = jnp.cumsum(padded)
    pad_start = pad_end - padded
    e = route[:, ROUTE_E1:ROUTE_E2 + 1].astype(jnp.int32)
    rank = route[:, ROUTE_RANK1:ROUTE_RANK2 + 1].astype(jnp.int32)
    start_of = jnp.sum(jnp.where(e[..., None] == jnp.arange(N_EXPERTS), pad_start, 0), axis=-1)
    dest = (start_of + rank).reshape(-1)
    n_blocks = (2 * n + N_EXPERTS * (EXPERT_BLOCK - 1) + EXPERT_BLOCK - 1) // EXPERT_BLOCK
    n_rows = n_blocks * EXPERT_BLOCK
    block = jnp.arange(n_blocks, dtype=jnp.int32)
    block_expert = jnp.minimum(jnp.sum(block[:, None] * EXPERT_BLOCK >= pad_end[None, :], axis=1), N_EXPERTS - 1)
    valid = jnp.clip(jnp.sum(jnp.where(block_expert[:, None] == jnp.arange(N_EXPERTS), pad_start + counts, 0), axis=1)
                     - block * EXPERT_BLOCK, 0, EXPERT_BLOCK)
    copies = (valid + GROUP - 1) // GROUP * GROUP
    assign = _invert(dest, valid.astype(jnp.int32), n_rows)
    used_blocks = (pad_end[-1] // EXPERT_BLOCK).reshape(1)
    row = jnp.arange(n_rows, dtype=jnp.int32)
    spare = 2 * n + ((row // EXPERT_BLOCK) % 2) * GROUP + row % GROUP
    src_token = jnp.where(assign < 0, 0, assign // 2).reshape(n_blocks, 1, EXPERT_BLOCK)
    dst_slot = jnp.where(assign < 0, spare, assign).reshape(n_blocks, 1, EXPERT_BLOCK)
    return (src_token, dst_slot, block_expert.astype(jnp.int32), copies.astype(jnp.int32),
            used_blocks.astype(jnp.int32))


def _tile_rows(i):
    return pl.ds(pl.multiple_of(i * ROW_TILE, ROW_TILE), ROW_TILE)


def _expert_kernel(be_ref, copies_ref, used_ref, src_ref, src_next_ref, dst_ref, h_ref, wg_ref, wu_ref, wd_ref,
                   slots_ref, x_buf, y_buf, wg_s, wu_s, wd_s, gather_sem, scatter_sem):
    i = pl.program_id(0)
    used = used_ref[0]
    cur = i % 2

    def gather_copy(idx_ref, r, buf):
        return pltpu.make_async_copy(h_ref.at[_tile_rows(idx_ref[0, 0, r])], x_buf.at[buf, _tile_rows(r)],
                                     gather_sem.at[buf])

    def scatter_copy(r, buf):
        return pltpu.make_async_copy(y_buf.at[buf, _tile_rows(r)], slots_ref.at[_tile_rows(dst_ref[0, 0, r])],
                                     scatter_sem.at[buf])

    def for_each_group(block, fn):
        def body(g, carry):
            for j in range(GROUP):
                fn(g * GROUP + j, j % 2)
            return carry
        lax.fori_loop(0, copies_ref[block] // GROUP, body, 0)

    @pl.when(i == 0)
    def _():
        x_buf[...] = jnp.zeros_like(x_buf)
        spare_rows = 2 * GROUP * ROW_TILE
        zero_spare = pltpu.make_async_copy(x_buf.at[0, pl.ds(0, spare_rows)],
                                           slots_ref.at[pl.ds(slots_ref.shape[0] - spare_rows, spare_rows)],
                                           scatter_sem.at[0])
        zero_spare.start()
        zero_spare.wait()
        for_each_group(0, lambda r, p: gather_copy(src_ref, r, 0).start(priority=p))

    @pl.when(i + 1 < used)
    def _():
        for_each_group(i + 1, lambda r, p: gather_copy(src_next_ref, r, 1 - cur).start(priority=p))

    @pl.when(jnp.logical_or(i == 0, be_ref[i] != be_ref[jnp.maximum(i - 1, 0)]))
    def _():
        wg_s[...] = wg_ref[0, 0].astype(BF16)
        wu_s[...] = wu_ref[0, 0].astype(BF16)
        wd_s[...] = wd_ref[0, 0].astype(BF16)

    @pl.when(i < used)
    def _():
        for_each_group(i, lambda r, p: gather_copy(src_ref, 0, cur).wait())
        x = jnp.concatenate([_load_token_slab(x_buf.at[cur], EXPERT_BLOCK, ROW_TILE, c) for c in range(ROW_TILE)],
                            axis=1).astype(BF16)
        hidden = _silu(_dot(x, wg_s[...])) * _dot(x, wu_s[...])
        _store_token_tiles(y_buf.at[cur], _dot(hidden.astype(BF16), wd_s[...]), ROW_TILE)
        for_each_group(i, lambda r, p: scatter_copy(r, cur).start(priority=p))

        @pl.when(i > 0)
        def _():
            for_each_group(i - 1, lambda r, p: scatter_copy(0, 1 - cur).wait())

        @pl.when(i == used - 1)
        def _():
            for_each_group(i, lambda r, p: scatter_copy(0, cur).wait())


def _experts(h2_tiles, plan, w_gate, w_up, w_down, layer):
    src_token, dst_slot, block_expert, copies, used_blocks = plan
    n_blocks = src_token.shape[0]
    n_slots = 2 * (h2_tiles.shape[0] // ROW_TILE) + 2 * GROUP
    last = n_blocks - 1

    def smem_block(index):
        return pl.BlockSpec((1, 1, EXPERT_BLOCK), index, memory_space=pltpu.SMEM)

    def weight(shape):
        return pl.BlockSpec((1, 1) + shape, lambda i, be, copies, used: (layer, be[i], 0, 0))

    grid_spec = pltpu.PrefetchScalarGridSpec(
        num_scalar_prefetch=3,
        grid=(n_blocks,),
        in_specs=[
            smem_block(lambda i, be, copies, used: (i, 0, 0)),
            smem_block(lambda i, be, copies, used: (jnp.minimum(i + 1, last), 0, 0)),
            smem_block(lambda i, be, copies, used: (i, 0, 0)),
            pl.BlockSpec(memory_space=pl.ANY),
            weight((D_MODEL, EXPERT_FF)), weight((D_MODEL, EXPERT_FF)), weight((EXPERT_FF, D_MODEL)),
        ],
        out_specs=pl.BlockSpec(memory_space=pl.ANY),
        scratch_shapes=[
            pltpu.VMEM((2, EXPERT_BLOCK * ROW_TILE, LANES), F32),
            pltpu.VMEM((2, EXPERT_BLOCK * ROW_TILE, LANES), F32),
            pltpu.VMEM((D_MODEL, EXPERT_FF), BF16),
            pltpu.VMEM((D_MODEL, EXPERT_FF), BF16),
            pltpu.VMEM((EXPERT_FF, D_MODEL), BF16),
            pltpu.SemaphoreType.DMA((2,)),
            pltpu.SemaphoreType.DMA((2,)),
        ],
    )
    return pl.pallas_call(
        _expert_kernel,
        grid_spec=grid_spec,
        out_shape=jax.ShapeDtypeStruct((n_slots * ROW_TILE, LANES), F32),
        compiler_params=_params(("arbitrary",), VMEM_LIMIT),
        name="experts",
    )(block_expert, copies, used_blocks, src_token, src_token, dst_slot, h2_tiles, w_gate, w_up, w_down)


def _final_kernel(slots_ref, x_ref, route_ref, mod_ref, fg_ref, o_ref):
    sq = _moe_residual(slots_ref, x_ref, route_ref, mod_ref, o_ref)
    o_ref[...] = o_ref[...] * lax.rsqrt(sq / D_MODEL + EPS) * fg_ref[...]


def _final_combine(slots, x_all, route, mod, final_g, n_batch):
    tiles = TOK // TM
    lat_tiles = SEQ // TM
    tok = lambda b, j: (b * tiles + CTX_LEN // TM + j, 0)
    return pl.pallas_call(
        _final_kernel,
        grid=(n_batch, lat_tiles),
        in_specs=[
            pl.BlockSpec((TM * 2 * ROW_TILE, LANES), tok),
            pl.BlockSpec((TM, D_MODEL), tok),
            pl.BlockSpec((TM, LANES), tok),
            pl.BlockSpec((1, 6, D_MODEL), lambda b, j: (b, 0, 0)),
            pl.BlockSpec((1, D_MODEL), lambda b, j: (0, 0)),
        ],
        out_specs=pl.BlockSpec((TM, D_MODEL), lambda b, j: (b * lat_tiles + j, 0)),
        out_shape=jax.ShapeDtypeStruct((n_batch * SEQ, D_MODEL), F32),
        compiler_params=_params(("parallel", "parallel")),
        name="final_combine",
    )(slots, x_all, route, mod, final_g.reshape(1, D_MODEL))


def _rope_tables():
    t = jnp.arange(SEQ)
    rows = (t // GRID_W).astype(F32)
    cols = (t % GRID_W).astype(F32)
    n_freq = HEAD_DIM // 4
    inv_freq = ROPE_BASE ** (-jnp.arange(n_freq, dtype=F32) / n_freq)
    ang_r = rows[:, None] * inv_freq
    ang_c = cols[:, None] * inv_freq
    cos = jnp.concatenate([jnp.cos(ang_r)] * 2 + [jnp.cos(ang_c)] * 2, axis=1)
    sin = jnp.concatenate([-jnp.sin(ang_r), jnp.sin(ang_r), -jnp.sin(ang_c), jnp.sin(ang_c)], axis=1)
    return cos, sin


def kernel(x, c, ctx, c_ctx, w_mod, b_mod, norm1_g, norm2_g, w_in, ret_decay, ret_norm_g, conv_w, conv_b,
           mlstm_gate_b, mlstm_norm_g, na_rpb, w_branch, w_out, w_group, w_router, w_expert_gate,
           w_expert_up, w_expert_down, final_norm_g):
    n_batch, seq, d = x.shape
    depth = w_mod.shape[0]
    assert (seq, d, ctx.shape[1]) == (SEQ, D_MODEL, CTX_LEN)
    n = n_batch * TOK

    cond_rows = -(-(n_batch + 1) // 8) * 8
    cond = jnp.zeros((cond_rows, d), F32).at[:n_batch].set(c).at[n_batch].set(c_ctx)
    mod_all = _modulation(cond, w_mod, b_mod).reshape(depth, cond_rows, 6, d)

    cos, sin = _rope_tables()
    x_all = jnp.concatenate([ctx, x], axis=1).reshape(n, d)

    w_packed = _pack_in_weights(w_in)
    w_branch_b = w_branch.astype(BF16)
    w_out_b = w_out.astype(BF16)
    w_route = jnp.concatenate([w_group, w_router], axis=-1)
    w_route = jnp.pad(w_route, ((0, 0), (0, 0), (0, LANES - w_route.shape[-1])))
    wr_hi = w_route.astype(BF16)
    wr_lo = (w_route - wr_hi.astype(F32)).astype(BF16)
    gate_b = jnp.pad(mlstm_gate_b.reshape(depth, 1, 4 * N_HEADS), ((0, 0), (0, 0), (0, LANES - 4 * N_HEADS)))
    na_bias = _na_bias_tables(na_rpb)

    moe = None
    for layer in range(depth):
        mod = mod_all[layer]
        if moe is None:
            ret, ml, gates, na, bg = _in_projection(x_all, mod, norm1_g, w_packed, layer, n_batch)
        else:
            x_all, ret, ml, gates, na, bg = _in_projection(x_all, mod, norm1_g, w_packed, layer, n_batch, moe)
        r_out = _retention(ret, ret_decay[layer], ret_norm_g[layer], cos, sin, n_batch)
        m_out = _mlstm(ml, gates, gate_b[layer], conv_w[layer], conv_b[layer].reshape(1, 2 * WIDTH),
                       mlstm_norm_g[layer], n_batch)
        a_out = _neighbourhood_attention(na, na_bias, layer, n_batch)
        x_all, h2, logits = _merge(r_out, m_out, a_out, bg, x_all, mod, w_branch_b, w_out_b, norm2_g,
                                   wr_hi, wr_lo, layer, n_batch)
        route, counts = _route(logits)
        plan = _dispatch_plan(route, counts)
        slots = _experts(h2, plan, w_expert_gate, w_expert_up, w_expert_down, layer)
        moe = (slots, route, mod)

    return _final_combine(slots, x_all, route, mod, final_norm_g, n_batch).reshape(n_batch, SEQ, d)
```

```python
import functools

import numpy as np
import jax
import jax.numpy as jnp
from jax import lax
from jax.experimental import pallas as pl
from jax.experimental.pallas import tpu as pltpu

F32 = jnp.float32
BF16 = jnp.bfloat16

D_MODEL = 1024
SEQ = 2048
CTX_LEN = 256
TOK = CTX_LEN + SEQ
GRID_W = 64
GRID_ROWS = SEQ // GRID_W
HEAD_DIM = 128
N_HEADS = 4
WIDTH = N_HEADS * HEAD_DIM
CHUNK = 256
N_CHUNKS = TOK // CHUNK
CTX_CHUNKS = CTX_LEN // CHUNK
CONV_WIDTH = 5
NA_WIN_ROWS = 8
NA_WIN_COLS = 16
NA_Q_ROWS = 4
NA_Q = NA_Q_ROWS * GRID_W
NA_K_ROWS = NA_Q_ROWS + NA_WIN_ROWS
NA_K = NA_K_ROWS * GRID_W
NA_STEPS = SEQ // NA_Q
ROPE_BASE = 10000.0
N_GROUPS = 4
EXPERTS_PER_GROUP = 8
N_EXPERTS = N_GROUPS * EXPERTS_PER_GROUP
EXPERT_FF = 512
EXPERT_BLOCK = 256
EPS = 1e-6
NEG_INF = -1e30
QK_SCALE = HEAD_DIM ** -0.5

TM = 256
ROUTE_TM = 512
LANES = 128
VMEM_LIMIT = 56 * 1024 * 1024


def _dot(a, b):
    return jnp.dot(a, b, preferred_element_type=F32)


def _dot_nt(a, b):
    return lax.dot_general(a, b, (((1,), (1,)), ((), ())), preferred_element_type=F32)


def _bdot(a, b):
    return _dot(a.astype(BF16), b.astype(BF16))


def _bdot_nt(a, b):
    return _dot_nt(a.astype(BF16), b.astype(BF16))


def _bdot_tn(a, b):
    return _dot(a.T.astype(BF16), b.astype(BF16))


def _log_sigmoid(x):
    return jnp.minimum(x, 0.0) - jnp.log1p(jnp.exp(-jnp.abs(x)))


def _silu(x):
    return x * jax.nn.sigmoid(x)


def _params(sem, vmem=None):
    return pltpu.CompilerParams(dimension_semantics=sem, vmem_limit_bytes=vmem)


def _mod_row(n_batch):
    tiles = TOK // TM

    def index(i):
        return (jnp.where(i % tiles == 0, n_batch, i // tiles), 0, 0)

    return index


def _mod_kernel(c_ref, w_ref, b_ref, o_ref):
    cond = _silu(c_ref[...])
    o_ref[0] = _bdot(cond, w_ref[0]) + b_ref[0]


def _modulation(cond, w_mod, b_mod):
    depth, d, cols = w_mod.shape
    rows = cond.shape[0]
    tn = 1536
    return pl.pallas_call(
        _mod_kernel,
        grid=(depth, cols // tn),
        in_specs=[
            pl.BlockSpec((rows, d), lambda l, j: (0, 0)),
            pl.BlockSpec((1, d, tn), lambda l, j: (l, 0, j)),
            pl.BlockSpec((1, 1, tn), lambda l, j: (l, 0, j)),
        ],
        out_specs=pl.BlockSpec((1, rows, tn), lambda l, j: (l, 0, j)),
        out_shape=jax.ShapeDtypeStruct((depth, rows, cols), F32),
        compiler_params=_params(("parallel", "parallel")),
        name="modulation",
    )(cond, w_mod, b_mod.reshape(depth, 1, cols))


def _rms_modulate(x, g, shift, scale):
    y = x * lax.rsqrt(jnp.mean(x * x, axis=-1, keepdims=True) + EPS) * g
    return y * (1.0 + scale) + shift


IN_WIDTHS = (4 * WIDTH, 4 * WIDTH, LANES, 3 * WIDTH, 3 * D_MODEL)
IN_OFFSETS = tuple(int(v) for v in np.cumsum((0,) + IN_WIDTHS))


def _pack_kernel(w_ref, o_ref):
    gate0 = IN_OFFSETS[2]
    n_gate = 4 * N_HEADS
    o_ref[0, :, :gate0] = w_ref[0, :, :gate0].astype(BF16)
    window = w_ref[0, :, gate0:gate0 + LANES]
    lane = lax.broadcasted_iota(jnp.int32, window.shape, 1)
    o_ref[0, :, gate0:gate0 + LANES] = jnp.where(lane < n_gate, window, 0.0).astype(BF16)
    o_ref[0, :, gate0 + LANES:] = w_ref[0, :, gate0 + n_gate:].astype(BF16)


def _pack_in_weights(w_in):
    depth, d, cols = w_in.shape
    rows = 256
    return pl.pallas_call(
        _pack_kernel,
        grid=(depth, d // rows),
        in_specs=[pl.BlockSpec((1, rows, cols), lambda l, r: (l, r, 0))],
        out_specs=pl.BlockSpec((1, rows, IN_OFFSETS[-1]), lambda l, r: (l, r, 0)),
        out_shape=jax.ShapeDtypeStruct((depth, d, IN_OFFSETS[-1]), BF16),
        compiler_params=_params(("parallel", "parallel"), VMEM_LIMIT),
        name="pack_in_weights",
    )(w_in)


def _project(x, mod_ref, g_ref, w_ref, out_refs):
    h = _rms_modulate(x, g_ref[0], mod_ref[0, 0:1, :], mod_ref[0, 1:2, :]).astype(BF16)
    for o_ref, lo, hi in zip(out_refs, IN_OFFSETS[:-1], IN_OFFSETS[1:]):
        o_ref[...] = _dot(h, w_ref[0, :, lo:hi])


def _inproj_kernel(x_ref, mod_ref, g_ref, w_ref, *out_refs):
    _project(x_ref[...], mod_ref, g_ref, w_ref, out_refs)


def _combine_inproj_kernel(slots_ref, x_ref, route_ref, prev_mod_ref, mod_ref, g_ref, w_ref, x_out_ref, *out_refs):
    _moe_residual(slots_ref, x_ref, route_ref, prev_mod_ref, x_out_ref)
    _project(x_out_ref[...], mod_ref, g_ref, w_ref, out_refs)


def _in_projection(x_all, mod, norm_g, w_packed, layer, n_batch, moe=None):
    n = x_all.shape[0]
    row = lambda i: (i, 0)
    in_specs = [
        pl.BlockSpec((TM, D_MODEL), row),
        pl.BlockSpec((1, 6, D_MODEL), _mod_row(n_batch)),
        pl.BlockSpec((1, 1, D_MODEL), lambda i: (layer, 0, 0)),
        pl.BlockSpec((1,) + w_packed.shape[1:], lambda i: (layer, 0, 0), pipeline_mode=pl.Buffered(1)),
    ]
    out_specs = [pl.BlockSpec((TM, w), row) for w in IN_WIDTHS]
    out_shape = [jax.ShapeDtypeStruct((n, w), F32) for w in IN_WIDTHS]
    args = (x_all, mod, norm_g.reshape(-1, 1, D_MODEL), w_packed)
    if moe is None:
        body = _inproj_kernel
    else:
        slots, route, prev_mod = moe
        body = _combine_inproj_kernel
        in_specs = ([pl.BlockSpec((TM * 2 * ROW_TILE, LANES), row), in_specs[0], pl.BlockSpec((TM, LANES), row),
                     pl.BlockSpec((1, 6, D_MODEL), _mod_row(n_batch))] + in_specs[1:])
        out_specs = [pl.BlockSpec((TM, D_MODEL), row)] + out_specs
        out_shape = [jax.ShapeDtypeStruct((n, D_MODEL), F32)] + out_shape
        args = (slots, x_all, route, prev_mod) + args[1:]
    return pl.pallas_call(
        body,
        grid=(n // TM,),
        in_specs=in_specs,
        out_specs=out_specs,
        out_shape=out_shape,
        compiler_params=_params(("parallel",), VMEM_LIMIT),
        name="in_projection",
    )(*args)


def _chunk_order(t):
    fwd = t
    bwd = jnp.where(t < CTX_CHUNKS, CTX_CHUNKS - 1 - t, N_CHUNKS + CTX_CHUNKS - 1 - t)
    return fwd, bwd


def _chunk_slice(c):
    return pl.ds(pl.multiple_of(c * CHUNK, CHUNK), CHUNK)


def _head_norm(y, gain):
    mu = jnp.mean(y, axis=-1, keepdims=True)
    yc = y - mu
    var = jnp.mean(yc * yc, axis=-1, keepdims=True)
    return yc * lax.rsqrt(var + EPS) * gain


ROW_TILE = D_MODEL // LANES


def _store_token_tiles(ref, x, pitch, offset=0):
    rows = x.shape[0]
    for c in range(ROW_TILE):
        ref[pl.ds(offset + c, rows, stride=pitch), :] = x[:, c * LANES:(c + 1) * LANES]


def _load_token_slab(ref, rows, pitch, c, offset=0):
    return ref[pl.ds(offset + c, rows, stride=pitch), :]


def _moe_residual(slots_ref, x_ref, route_ref, mod_ref, o_ref):
    route = route_ref[...]
    w1 = route[:, ROUTE_W1:ROUTE_W1 + 1]
    w2 = route[:, ROUTE_W2:ROUTE_W2 + 1]
    rows = x_ref.shape[0]
    sq = jnp.zeros((rows, 1), F32)
    for c in range(ROW_TILE):
        lanes = slice(c * LANES, (c + 1) * LANES)
        y1 = _load_token_slab(slots_ref, rows, 2 * ROW_TILE, c)
        y2 = _load_token_slab(slots_ref, rows, 2 * ROW_TILE, c, offset=ROW_TILE)
        x_new = x_ref[:, lanes] + mod_ref[0, 5:6, lanes] * (y1 * w1 + y2 * w2)
        o_ref[:, lanes] = x_new
        sq = sq + jnp.sum(x_new * x_new, axis=-1, keepdims=True)
    return sq


def _select_lane(x, lane, idx):
    return jnp.sum(jnp.where(lane == idx, x, 0.0), axis=-1, keepdims=True)


HEADS_PER_STEP = 2
HEAD_STEPS = N_HEADS // HEADS_PER_STEP
STEP_WIDTH = HEADS_PER_STEP * HEAD_DIM
SCAN_VMEM = 48 * 1024 * 1024


def _head_lanes(hh):
    return slice(hh * HEAD_DIM, (hh + 1) * HEAD_DIM)


def _retention_kernel(dec_ref, q_ref, k_ref, v_ref, g_ref, cos_ref, sin_ref, gn_ref, o_ref,
                      qs_ref, ks_ref, ob_ref, intra_ref, st_ref):
    first_head = pl.program_id(1) * HEADS_PER_STEP

    lane = lax.broadcasted_iota(jnp.int32, (CHUNK, HEAD_DIM), 1)
    first_half = (lane % (HEAD_DIM // 2)) < (HEAD_DIM // 4)

    def rope(x, cos, sin):
        rot = jnp.where(first_half, pltpu.roll(x, HEAD_DIM - HEAD_DIM // 4, 1), pltpu.roll(x, HEAD_DIM // 4, 1))
        return x * cos + rot * sin

    qs_ref[0:CTX_LEN, :] = q_ref[0:CTX_LEN, :]
    ks_ref[0:CTX_LEN, :] = k_ref[0:CTX_LEN, :] * QK_SCALE

    def rotate_chunk(c, carry):
        rows = _chunk_slice(c)
        pos = _chunk_slice(c - CTX_CHUNKS)
        cos = cos_ref[pos, :]
        sin = sin_ref[pos, :]
        for hh in range(HEADS_PER_STEP):
            qs_ref[rows, _head_lanes(hh)] = rope(q_ref[rows, _head_lanes(hh)], cos, sin)
            ks_ref[rows, _head_lanes(hh)] = rope(k_ref[rows, _head_lanes(hh)], cos, sin) * QK_SCALE
        return carry

    lax.fori_loop(CTX_CHUNKS, N_CHUNKS, rotate_chunk, 0)

    ii = lax.broadcasted_iota(jnp.int32, (CHUNK, CHUNK), 0).astype(F32)
    jj = lax.broadcasted_iota(jnp.int32, (CHUNK, CHUNK), 1).astype(F32)
    col = lax.broadcasted_iota(jnp.int32, (CHUNK, 1), 0).astype(F32)

    def decay_mat(dist, lg_dir):
        ok = dist >= 0
        return jnp.where(ok, jnp.exp(jnp.where(ok, dist, 0.0) * lg_dir), 0.0)

    dec = _log_sigmoid(dec_ref[...])
    hl = lax.broadcasted_iota(jnp.int32, dec.shape, 1)
    consts = []
    for hh in range(HEADS_PER_STEP):
        lg = jnp.sum(jnp.where(hl == first_head + hh, dec, 0.0), axis=-1, keepdims=True)
        lg_f, lg_b = lg[0:1, :], lg[1:2, :]
        intra_ref[2 * hh] = decay_mat(ii - jj, lg_f)
        intra_ref[2 * hh + 1] = decay_mat(jj - ii, lg_b)
        consts.append(dict(
            q_decay=(jnp.exp((col + 1.0) * lg_f), jnp.exp((CHUNK - col) * lg_b)),
            k_decay=(jnp.exp((CHUNK - 1.0 - col) * lg_f), jnp.exp(col * lg_b)),
            chunk_decay=(jnp.exp(CHUNK * lg_f), jnp.exp(CHUNK * lg_b))))

    st_ref[...] = jnp.zeros_like(st_ref)
    out_refs = (o_ref, ob_ref)

    def step(t, carry):
        for hh in range(HEADS_PER_STEP):
            cst = consts[hh]
            for d, c in enumerate(_chunk_order(t)):
                rows = _chunk_slice(c)
                q = qs_ref[rows, _head_lanes(hh)]
                k = ks_ref[rows, _head_lanes(hh)]
                v = v_ref[rows, _head_lanes(hh)]
                s_prev = st_ref[2 * hh + d]
                scores = _bdot_nt(q, k) * intra_ref[2 * hh + d]
                out_refs[d][rows, _head_lanes(hh)] = _bdot(scores, v) + _bdot(q * cst["q_decay"][d], s_prev)
                st_ref[2 * hh + d] = s_prev * cst["chunk_decay"][d] + _bdot_tn(k * cst["k_decay"][d], v)
        return carry

    lax.fori_loop(0, N_CHUNKS, step, 0)

    def finish_chunk(c, carry):
        rows = _chunk_slice(c)
        for hh in range(HEADS_PER_STEP):
            lanes = _head_lanes(hh)
            y = o_ref[rows, lanes] + ob_ref[rows, lanes]
            o_ref[rows, lanes] = _head_norm(y, gn_ref[:, lanes]) * _silu(g_ref[rows, lanes])
        return carry

    lax.fori_loop(0, N_CHUNKS, finish_chunk, 0)


def _retention(ret, ret_decay, norm_g, cos, sin, n_batch):
    n = ret.shape[0]

    def head_block(offset):
        return pl.BlockSpec((TOK, STEP_WIDTH), lambda b, h: (b, offset * HEAD_STEPS + h))

    return pl.pallas_call(
        _retention_kernel,
        grid=(n_batch, HEAD_STEPS),
        in_specs=[
            pl.BlockSpec(ret_decay.shape, lambda b, h: (0, 0)),
            head_block(0), head_block(1), head_block(2), head_block(3),
            pl.BlockSpec((SEQ, HEAD_DIM), lambda b, h: (0, 0)),
            pl.BlockSpec((SEQ, HEAD_DIM), lambda b, h: (0, 0)),
            pl.BlockSpec((1, STEP_WIDTH), lambda b, h: (0, h)),
        ],
        out_specs=pl.BlockSpec((TOK, STEP_WIDTH), lambda b, h: (b, h)),
        out_shape=jax.ShapeDtypeStruct((n, WIDTH), F32),
        scratch_shapes=[
            pltpu.VMEM((TOK, STEP_WIDTH), F32),
            pltpu.VMEM((TOK, STEP_WIDTH), F32),
            pltpu.VMEM((TOK, STEP_WIDTH), F32),
            pltpu.VMEM((2 * HEADS_PER_STEP, CHUNK, CHUNK), F32),
            pltpu.VMEM((2 * HEADS_PER_STEP, HEAD_DIM, HEAD_DIM), F32),
        ],
        compiler_params=_params(("parallel", "parallel"), SCAN_VMEM),
        name="retention",
    )(ret_decay, ret, ret, ret, ret, cos, sin, norm_g.reshape(1, WIDTH))


CONV_PAD = 8
CONV_ROWS = 128


def _mlstm_kernel(q_ref, k_ref, v_ref, og_ref, gate_ref, gb_ref, cwq_ref, cwk_ref, cbq_ref, cbk_ref,
                  gn_ref, o_ref, qs_ref, ks_ref, ob_ref, pad_ref, gx_ref, gxt_ref, cst_ref):
    first_head = pl.program_id(1) * HEADS_PER_STEP

    pad_ref[0:CONV_PAD, :] = jnp.zeros((CONV_PAD, STEP_WIDTH), F32)
    pad_ref[CONV_PAD + TOK:, :] = jnp.zeros((CONV_PAD, STEP_WIDTH), F32)
    crow = lax.broadcasted_iota(jnp.int32, (CONV_ROWS, 1), 0)

    def conv(u_ref, w_ref, b_ref, dst_ref, scale):
        def fill(c, carry):
            rows = _chunk_slice(c)
            pad_ref[pl.ds(pl.multiple_of(c * CHUNK + CONV_PAD, CONV_PAD), CHUNK), :] = u_ref[rows, :]
            return carry

        lax.fori_loop(0, N_CHUNKS, fill, 0)

        for blk in range(TOK // CONV_ROWS):
            first = blk * CONV_ROWS
            acc = jnp.zeros((CONV_ROWS, STEP_WIDTH), F32)
            for j in range(CONV_WIDTH):
                shift = j - CONV_WIDTH // 2
                tap = pad_ref[first + CONV_PAD + shift:first + CONV_PAD + shift + CONV_ROWS, :]
                if (first + CONV_ROWS == CTX_LEN and shift > 0) or (first == CTX_LEN and shift < 0):
                    trow = crow + first
                    tap = jnp.where((trow < CTX_LEN) == (trow + shift < CTX_LEN), tap, 0.0)
                acc = acc + tap * w_ref[j:j + 1, :]
            dst_ref[first:first + CONV_ROWS, :] = _silu(acc + b_ref[...]) * scale

    conv(q_ref, cwq_ref, cbq_ref, qs_ref, 1.0)
    conv(k_ref, cwk_ref, cbk_ref, ks_ref, QK_SCALE)

    clane = lax.broadcasted_iota(jnp.int32, (CHUNK, LANES), 1)
    is_forget = ((clane // N_HEADS) % 2) == 1

    def gate_chunk(c, carry):
        rows = _chunk_slice(c)
        g = gate_ref[rows, :] + gb_ref[...]
        gx = jnp.where(is_forget, _log_sigmoid(g), g)
        gx_ref[rows, :] = gx
        gxt_ref[c] = gx.T
        return carry

    lax.fori_loop(0, N_CHUNKS, gate_chunk, 0)

    ii = lax.broadcasted_iota(jnp.int32, (CHUNK, CHUNK), 0)
    jj = lax.broadcasted_iota(jnp.int32, (CHUNK, CHUNK), 1)
    causal = (jj <= ii, jj >= ii)

    cst_ref[...] = jnp.zeros_like(cst_ref)
    out_refs = (o_ref, ob_ref)

    def step(t, carry):
        new_carry = []
        for hh in range(HEADS_PER_STEP):
            head = first_head + hh
            lanes = _head_lanes(hh)
            for d, c in enumerate(_chunk_order(t)):
                slot = 2 * hh + d
                n_prev, m_prev = carry[2 * slot], carry[2 * slot + 1]
                rows = _chunk_slice(c)
                q = qs_ref[rows, lanes]
                k = ks_ref[rows, lanes]
                v = v_ref[rows, lanes]
                gc = gx_ref[rows, :]
                i_col = _select_lane(gc, clane, 2 * d * N_HEADS + head)
                f_col = _select_lane(gc, clane, (2 * d + 1) * N_HEADS + head)
                i_row = gxt_ref[c, pl.ds(2 * d * N_HEADS + head, 1), :]
                f_row = gxt_ref[c, pl.ds((2 * d + 1) * N_HEADS + head, 1), :]
                vis = causal[d]
                cum_col = jnp.sum(jnp.where(vis, f_row, 0.0), axis=1, keepdims=True)
                cum_row = jnp.sum(jnp.where(causal[1 - d], f_col, 0.0), axis=0, keepdims=True)
                total = jnp.sum(f_row, axis=1, keepdims=True)
                c_prev = cst_ref[slot]

                log_kw = total - cum_col + i_col
                m_new = jnp.maximum(total + m_prev, jnp.max(log_kw, axis=0, keepdims=True))
                kw = jnp.exp(log_kw - m_new)
                pw = jnp.exp(total + m_prev - m_new)
                cst_ref[slot] = pw * c_prev + _bdot_tn(k * kw, v)
                n_new = pw * n_prev + jnp.sum(kw * k, axis=0, keepdims=True)

                log_w = jnp.where(vis, cum_col - cum_row + i_row, -jnp.inf)
                log_p = cum_col + m_prev
                m_t = jnp.maximum(log_p, jnp.max(log_w, axis=1, keepdims=True))
                w = jnp.exp(log_w - m_t)
                p = jnp.exp(log_p - m_t)
                qk = _bdot_nt(q, k) * w
                num = _bdot(qk, v) + p * _bdot(q, c_prev)
                den = jnp.sum(qk, axis=1, keepdims=True) + p * jnp.sum(q * n_prev, axis=1, keepdims=True)
                out_refs[d][rows, lanes] = num / jnp.maximum(jnp.abs(den), jnp.exp(-m_t))
                new_carry += [n_new, m_new]
        return tuple(new_carry)

    zero_n = jnp.zeros((1, HEAD_DIM), F32)
    zero_m = jnp.zeros((1, 1), F32)
    lax.fori_loop(0, N_CHUNKS, step, (zero_n, zero_m) * (2 * HEADS_PER_STEP))

    def finish_chunk(c, carry):
        rows = _chunk_slice(c)
        for hh in range(HEADS_PER_STEP):
            lanes = _head_lanes(hh)
            gated = (o_ref[rows, lanes] + ob_ref[rows, lanes]) * jax.nn.sigmoid(og_ref[rows, lanes])
            o_ref[rows, lanes] = _head_norm(gated, gn_ref[:, lanes])
        return carry

    lax.fori_loop(0, N_CHUNKS, finish_chunk, 0)


def _mlstm(ml, gates, gate_b, conv_w, conv_b, norm_g, n_batch):
    n = ml.shape[0]

    def head_block(offset):
        return pl.BlockSpec((TOK, STEP_WIDTH), lambda b, h: (b, offset * HEAD_STEPS + h))

    def head_cols(rows, offset):
        return pl.BlockSpec((rows, STEP_WIDTH), lambda b, h: (0, offset * HEAD_STEPS + h))

    return pl.pallas_call(
        _mlstm_kernel,
        grid=(n_batch, HEAD_STEPS),
        in_specs=[
            head_block(0), head_block(1), head_block(2), head_block(3),
            pl.BlockSpec((TOK, LANES), lambda b, h: (b, 0)),
            pl.BlockSpec((1, LANES), lambda b, h: (0, 0)),
            head_cols(CONV_WIDTH, 0), head_cols(CONV_WIDTH, 1),
            head_cols(1, 0), head_cols(1, 1),
            head_cols(1, 0),
        ],
        out_specs=pl.BlockSpec((TOK, STEP_WIDTH), lambda b, h: (b, h)),
        out_shape=jax.ShapeDtypeStruct((n, WIDTH), F32),
        scratch_shapes=[
            pltpu.VMEM((TOK, STEP_WIDTH), F32),
            pltpu.VMEM((TOK, STEP_WIDTH), F32),
            pltpu.VMEM((TOK, STEP_WIDTH), F32),
            pltpu.VMEM((TOK + 2 * CONV_PAD, STEP_WIDTH), F32),
            pltpu.VMEM((TOK, LANES), F32),
            pltpu.VMEM((N_CHUNKS, LANES, CHUNK), F32),
            pltpu.VMEM((2 * HEADS_PER_STEP, HEAD_DIM, HEAD_DIM), F32),
        ],
        compiler_params=_params(("parallel", "parallel"), SCAN_VMEM),
        name="mlstm",
    )(ml, ml, ml, ml, gates, gate_b, conv_w, conv_w, conv_b, conv_b, norm_g.reshape(1, WIDTH))


def _na_kernel(q_ref, k_ref, v_ref, bias_ref, o_ref):
    step = pl.program_id(2)

    def context_scores(hh):
        lanes = _head_lanes(hh)
        q = q_ref[:, lanes].astype(BF16)
        v_ctx = v_ref[0:CTX_LEN, lanes].astype(BF16)
        return q, v_ctx, _dot_nt(q, k_ref[0:CTX_LEN, lanes].astype(BF16)) * QK_SCALE

    @pl.when(step == 0)
    def _():
        for hh in range(HEADS_PER_STEP):
            _, v_ctx, s_ctx = context_scores(hh)
            m = jnp.max(s_ctx, axis=-1, keepdims=True)
            p = jnp.exp(s_ctx - m)
            o_ref[:, _head_lanes(hh)] = _dot(p.astype(BF16), v_ctx) / jnp.sum(p, axis=-1, keepdims=True)

    @pl.when(step > 0)
    def _():
        start = CTX_LEN + NA_Q * jnp.clip(step - 2, 0, NA_STEPS - NA_K_ROWS // NA_Q_ROWS)
        rows = pl.ds(pl.multiple_of(start, NA_Q), NA_K)
        for hh in range(HEADS_PER_STEP):
            lanes = _head_lanes(hh)
            q, v_ctx, s_ctx = context_scores(hh)
            s_loc = _dot_nt(q, k_ref[rows, lanes].astype(BF16)) * QK_SCALE + bias_ref[0, hh, 0]
            m = jnp.maximum(jnp.max(s_loc, axis=-1, keepdims=True), jnp.max(s_ctx, axis=-1, keepdims=True))
            p_loc = jnp.exp(s_loc - m)
            p_ctx = jnp.exp(s_ctx - m)
            denom = jnp.sum(p_loc, axis=-1, keepdims=True) + jnp.sum(p_ctx, axis=-1, keepdims=True)
            o = _dot(p_loc.astype(BF16), v_ref[rows, lanes].astype(BF16)) + _dot(p_ctx.astype(BF16), v_ctx)
            o_ref[:, lanes] = o / denom


def _na_row_case(case, a):
    t = np.arange(NA_K_ROWS)
    last_start = NA_K_ROWS - NA_WIN_ROWS
    return [
        (t < NA_WIN_ROWS, NA_WIN_ROWS - 1 - a),
        ((t >= a) & (t < a + NA_WIN_ROWS), NA_WIN_ROWS // 2 - 1 - a),
        (t >= last_start, NA_Q_ROWS - NA_K_ROWS + NA_WIN_ROWS - 1 - a),
    ][case]


def _na_expand_kernel(slab_ref, o_ref):
    masked = jnp.full((GRID_W, GRID_W), NEG_INF, F32)
    for case in range(3):
        for a in range(NA_Q_ROWS):
            row_ok, first = _na_row_case(case, a)
            for pair in range(NA_K_ROWS // 2):
                tiles = [slab_ref[0, first + t] if row_ok[t] else masked for t in (2 * pair, 2 * pair + 1)]
                o_ref[0, case, a * GRID_W:(a + 1) * GRID_W, pair * LANES:(pair + 1) * LANES] = (
                    jnp.concatenate(tiles, axis=1))


def _na_bias_tables(rpb):
    lead = rpb.shape[:-2]
    n_row_off, n_col_off = rpb.shape[-2:]
    qc = np.arange(GRID_W)[:, None]
    kc = np.arange(GRID_W)[None, :]
    col_start = np.clip(qc - NA_WIN_COLS // 2, 0, GRID_W - NA_WIN_COLS)
    col_ok = (kc >= col_start) & (kc < col_start + NA_WIN_COLS)
    col_idx = np.clip(kc - qc + NA_WIN_COLS - 1, 0, n_col_off - 1)
    onehot = ((col_idx[None] == np.arange(n_col_off)[:, None, None]) & col_ok[None]).astype(np.float32)
    slabs = jnp.einsum('...rc,cqk->...rqk', rpb.astype(F32), onehot, precision=lax.Precision.HIGHEST)
    slabs = jnp.where(col_ok, slabs, NEG_INF).reshape(-1, n_row_off, GRID_W, GRID_W)
    n_tables = slabs.shape[0]
    tables = pl.pallas_call(
        _na_expand_kernel,
        grid=(n_tables,),
        in_specs=[pl.BlockSpec((1, n_row_off, GRID_W, GRID_W), lambda i: (i, 0, 0, 0))],
        out_specs=pl.BlockSpec((1, 3, NA_Q, NA_K), lambda i: (i, 0, 0, 0)),
        out_shape=jax.ShapeDtypeStruct((n_tables, 3, NA_Q, NA_K), F32),
        compiler_params=_params(("parallel",)),
        name="na_bias_tables",
    )(slabs)
    return tables.reshape(*lead, 3, NA_Q, NA_K)


def _neighbourhood_attention(na, bias, layer, n_batch):
    n = na.shape[0]
    steps = 1 + NA_STEPS
    tiles = TOK // NA_Q

    def table(b, h, j):
        return (layer, h, jnp.where(j <= 1, 0, jnp.where(j == NA_STEPS, 2, 1)), 0, 0)

    return pl.pallas_call(
        _na_kernel,
        grid=(n_batch, HEAD_STEPS, steps),
        in_specs=[
            pl.BlockSpec((NA_Q, STEP_WIDTH), lambda b, h, j: (b * tiles + j, h)),
            pl.BlockSpec((TOK, STEP_WIDTH), lambda b, h, j: (b, HEAD_STEPS + h)),
            pl.BlockSpec((TOK, STEP_WIDTH), lambda b, h, j: (b, 2 * HEAD_STEPS + h)),
            pl.BlockSpec((1, HEADS_PER_STEP, 1, NA_Q, NA_K), table),
        ],
        out_specs=pl.BlockSpec((NA_Q, STEP_WIDTH), lambda b, h, j: (b * tiles + j, h)),
        out_shape=jax.ShapeDtypeStruct((n, WIDTH), F32),
        compiler_params=_params(("parallel", "parallel", "arbitrary")),
        name="neighbourhood_attention",
    )(na, na, na, bias)


def _merge_kernel(r_ref, m_ref, a_ref, bg_ref, x_ref, mod_ref, wb_ref, wo_ref, g2_ref, wr_ref,
                  x_out_ref, h2_ref, logit_ref):
    gate = jax.nn.sigmoid(bg_ref[...])
    mix = (gate[:, 0:D_MODEL] * _dot(r_ref[...].astype(BF16), wb_ref[0, 0])
           + gate[:, D_MODEL:2 * D_MODEL] * _dot(m_ref[...].astype(BF16), wb_ref[0, 1])
           + gate[:, 2 * D_MODEL:] * _dot(a_ref[...].astype(BF16), wb_ref[0, 2]))
    y = _dot(mix.astype(BF16), wo_ref[0])
    x_new = x_ref[...] + mod_ref[0, 2:3, :] * y
    x_out_ref[...] = x_new
    h2 = _rms_modulate(x_new, g2_ref[0], mod_ref[0, 3:4, :], mod_ref[0, 4:5, :])
    _store_token_tiles(h2_ref, h2, ROW_TILE)
    hi = h2.astype(BF16)
    lo = (h2 - hi.astype(F32)).astype(BF16)
    hi_terms = _dot(hi, wr_ref[0])
    logit_ref[...] = hi_terms[:, :LANES] + (_dot(lo, wr_ref[0, :, :LANES]) + hi_terms[:, LANES:])


def _merge(r, m, a, bg, x_all, mod, w_branch, w_out, norm2_g, w_route, layer, n_batch):
    n = x_all.shape[0]
    row = lambda i: (i, 0)

    def layer_block(w):
        return pl.BlockSpec((1,) + w.shape[1:], lambda i: (layer,) + (0,) * (w.ndim - 1))

    return pl.pallas_call(
        _merge_kernel,
        grid=(n // TM,),
        in_specs=[
            pl.BlockSpec((TM, WIDTH), row), pl.BlockSpec((TM, WIDTH), row), pl.BlockSpec((TM, WIDTH), row),
            pl.BlockSpec((TM, 3 * D_MODEL), row),
            pl.BlockSpec((TM, D_MODEL), row),
            pl.BlockSpec((1, 6, D_MODEL), _mod_row(n_batch)),
            layer_block(w_branch), layer_block(w_out),
            pl.BlockSpec((1, 1, D_MODEL), lambda i: (layer, 0, 0)),
            layer_block(w_route),
        ],
        out_specs=[pl.BlockSpec((TM, D_MODEL), row), pl.BlockSpec((TM * ROW_TILE, LANES), row),
                   pl.BlockSpec((TM, LANES), row)],
        out_shape=[jax.ShapeDtypeStruct((n, D_MODEL), F32), jax.ShapeDtypeStruct((n * ROW_TILE, LANES), F32),
                   jax.ShapeDtypeStruct((n, LANES), F32)],
        compiler_params=_params(("parallel",), VMEM_LIMIT),
        name="merge",
    )(r, m, a, bg, x_all, mod, w_branch, w_out, norm2_g.reshape(-1, 1, D_MODEL), w_route)


ROUTE_E1, ROUTE_E2, ROUTE_RANK1, ROUTE_RANK2, ROUTE_W1, ROUTE_W2 = range(6)


def _route_kernel(logit_ref, route_ref, count_ref, cnt_ref):
    @pl.when(pl.program_id(0) == 0)
    def _():
        cnt_ref[...] = jnp.zeros_like(cnt_ref)

    lg = logit_ref[...]
    lane = lax.broadcasted_iota(jnp.int32, lg.shape, 1)
    lane_f = lane.astype(F32)

    def first_argmax(vals):
        top = jnp.max(vals, axis=-1, keepdims=True)
        idx = jnp.min(jnp.where(vals == top, lane_f, float(LANES)), axis=-1, keepdims=True)
        return top, idx

    group_logits = jnp.where(lane < N_GROUPS, lg, -jnp.inf)
    g_top, g_idx = first_argmax(group_logits)
    group_w = 1.0 / jnp.sum(jnp.exp(group_logits - g_top), axis=-1, keepdims=True)

    first = N_GROUPS + EXPERTS_PER_GROUP * g_idx
    in_group = (lane_f >= first) & (lane_f < first + EXPERTS_PER_GROUP)
    expert_logits = jnp.where(in_group, lg, -jnp.inf)
    v1, i1 = first_argmax(expert_logits)
    v2, i2 = first_argmax(jnp.where(lane_f == i1, -jnp.inf, expert_logits))
    t = jnp.exp(v2 - v1)
    w1 = group_w / (1.0 + t)
    w2 = group_w * t / (1.0 + t)

    oh1 = (lane_f == i1).astype(F32)
    oh2 = (lane_f == i2).astype(F32)
    both = oh1 + oh2
    rows = lg.shape[0]
    earlier = (lax.broadcasted_iota(jnp.int32, (rows, rows), 1)
               < lax.broadcasted_iota(jnp.int32, (rows, rows), 0)).astype(BF16)
    before = _dot(earlier, both.astype(BF16)) + cnt_ref[...]
    rank1 = jnp.sum(oh1 * before, axis=-1, keepdims=True)
    rank2 = jnp.sum(oh2 * before, axis=-1, keepdims=True)
    cnt_ref[...] += jnp.sum(both, axis=0, keepdims=True)
    count_ref[...] = jnp.broadcast_to(cnt_ref[...], count_ref.shape)

    out = jnp.zeros_like(lg)
    for slot, val in ((ROUTE_E1, i1 - N_GROUPS), (ROUTE_E2, i2 - N_GROUPS), (ROUTE_RANK1, rank1),
                      (ROUTE_RANK2, rank2), (ROUTE_W1, w1), (ROUTE_W2, w2)):
        out = jnp.where(lane == slot, val, out)
    route_ref[...] = out


def _route(logits):
    n = logits.shape[0]
    return pl.pallas_call(
        _route_kernel,
        grid=(n // ROUTE_TM,),
        in_specs=[pl.BlockSpec((ROUTE_TM, LANES), lambda i: (i, 0))],
        out_specs=[pl.BlockSpec((ROUTE_TM, LANES), lambda i: (i, 0)), pl.BlockSpec((8, LANES), lambda i: (0, 0))],
        out_shape=[jax.ShapeDtypeStruct((n, LANES), F32), jax.ShapeDtypeStruct((8, LANES), F32)],
        scratch_shapes=[pltpu.VMEM((1, LANES), F32)],
        compiler_params=_params(("arbitrary",)),
        name="route",
    )(logits)


GROUP = 8


INVERT_TILE = 1024


def _invert_kernel(valid_ref, dest_ref, assign_ref):
    i = pl.program_id(0)

    @pl.when(i == 0)
    def _():
        def mark_block(b, carry):
            def mark(g, inner):
                for j in range(GROUP):
                    assign_ref[b * EXPERT_BLOCK + g * GROUP + j] = -1
                return inner
            lax.fori_loop(valid_ref[b] // GROUP, EXPERT_BLOCK // GROUP, mark, 0)
            return carry
        lax.fori_loop(0, valid_ref.shape[0], mark_block, 0)

    base = i * INVERT_TILE

    def scatter(g, carry):
        for j in range(GROUP):
            k = g * GROUP + j
            assign_ref[dest_ref[0, 0, k]] = base + k
        return carry

    lax.fori_loop(0, INVERT_TILE // GROUP, scatter, 0)


def _invert(dest, valid, n_rows):
    n_assign = dest.shape[0]
    grid_spec = pltpu.PrefetchScalarGridSpec(
        num_scalar_prefetch=1,
        grid=(n_assign // INVERT_TILE,),
        in_specs=[pl.BlockSpec((1, 1, INVERT_TILE), lambda i, valid: (i, 0, 0), memory_space=pltpu.SMEM)],
        out_specs=pl.BlockSpec(memory_space=pltpu.SMEM),
    )
    return pl.pallas_call(
        _invert_kernel,
        grid_spec=grid_spec,
        out_shape=jax.ShapeDtypeStruct((n_rows,), jnp.int32),
        compiler_params=_params(("arbitrary",)),
        name="invert_assignment",
    )(valid, dest.reshape(n_assign // INVERT_TILE, 1, INVERT_TILE))


def _dispatch_plan(route, counts):
    n = route.shape[0]
    counts = counts[0, N_GROUPS:N_GROUPS + N_EXPERTS].astype(jnp.int32)
    padded = (counts + EXPERT_BLOCK - 1) // EXPERT_BLOCK * EXPERT_BLOCK
    pad_end = jnp.cumsum(padded)
    pad_start = pad_end - padded
    e = route[:, ROUTE_E1:ROUTE_E2 + 1].astype(jnp.int32)
    rank = route[:, ROUTE_RANK1:ROUTE_RANK2 + 1].astype(jnp.int32)
    start_of = jnp.sum(jnp.where(e[..., None] == jnp.arange(N_EXPERTS), pad_start, 0), axis=-1)
    dest = (start_of + rank).reshape(-1)
    n_blocks = (2 * n + N_EXPERTS * (EXPERT_BLOCK - 1) + EXPERT_BLOCK - 1) // EXPERT_BLOCK
    n_rows = n_blocks * EXPERT_BLOCK
    block = jnp.arange(n_blocks, dtype=jnp.int32)
    block_expert = jnp.minimum(jnp.sum(block[:, None] * EXPERT_BLOCK >= pad_end[None, :], axis=1), N_EXPERTS - 1)
    valid = jnp.clip(jnp.sum(jnp.where(block_expert[:, None] == jnp.arange(N_EXPERTS), pad_start + counts, 0), axis=1)
                     - block * EXPERT_BLOCK, 0, EXPERT_BLOCK)
    copies = (valid + GROUP - 1) // GROUP * GROUP
    assign = _invert(dest, valid.astype(jnp.int32), n_rows)
    used_blocks = (pad_end[-1] // EXPERT_BLOCK).reshape(1)
    row = jnp.arange(n_rows, dtype=jnp.int32)
    spare = 2 * n + ((row // EXPERT_BLOCK) % 2) * GROUP + row % GROUP
    src_token = jnp.where(assign < 0, 0, assign // 2).reshape(n_blocks, 1, EXPERT_BLOCK)
    dst_slot = jnp.where(assign < 0, spare, assign).reshape(n_blocks, 1, EXPERT_BLOCK)
    return (src_token, dst_slot, block_expert.astype(jnp.int32), copies.astype(jnp.int32),
            used_blocks.astype(jnp.int32))


def _tile_rows(i):
    return pl.ds(pl.multiple_of(i * ROW_TILE, ROW_TILE), ROW_TILE)


def _expert_kernel(be_ref, copies_ref, used_ref, src_ref, src_next_ref, dst_ref, dst_prev_ref, h_ref,
                   wg_ref, wu_ref, wd_ref, slots_ref, x_buf, y_buf, wg_s, wu_s, wd_s, gather_sem, scatter_sem):
    i = pl.program_id(0)
    used = used_ref[0]
    cur = i % 2

    def gather_copy(idx_ref, r, buf):
        return pltpu.make_async_copy(h_ref.at[_tile_rows(idx_ref[0, 0, r])], x_buf.at[buf, _tile_rows(r)],
                                     gather_sem.at[buf])

    def scatter_copy(idx_ref, r, buf):
        return pltpu.make_async_copy(y_buf.at[buf, _tile_rows(r)], slots_ref.at[_tile_rows(idx_ref[0, 0, r])],
                                     scatter_sem.at[buf])

    def for_each_group(block, fn):
        def body(g, carry):
            for j in range(GROUP):
                fn(g * GROUP + j, j % 2)
            return carry
        lax.fori_loop(0, copies_ref[block] // GROUP, body, 0)

    @pl.when(i == 0)
    def _():
        x_buf[...] = jnp.zeros_like(x_buf)
        spare_rows = 2 * GROUP * ROW_TILE
        zero_spare = pltpu.make_async_copy(x_buf.at[0, pl.ds(0, spare_rows)],
                                           slots_ref.at[pl.ds(slots_ref.shape[0] - spare_rows, spare_rows)],
                                           scatter_sem.at[0])
        zero_spare.start()
        zero_spare.wait()
        for_each_group(0, lambda r, p: gather_copy(src_ref, r, 0).start(priority=1))

    @pl.when(i + 1 < used)
    def _():
        for_each_group(i + 1, lambda r, p: gather_copy(src_next_ref, r, 1 - cur).start(priority=1))

    has_prev = jnp.logical_and(i > 0, i < used)

    @pl.when(has_prev)
    def _():
        for_each_group(i - 1, lambda r, p: scatter_copy(dst_prev_ref, r, 1 - cur).start(priority=p))

    @pl.when(jnp.logical_or(i == 0, be_ref[i] != be_ref[jnp.maximum(i - 1, 0)]))
    def _():
        wg_s[...] = wg_ref[0, 0].astype(BF16)
        wu_s[...] = wu_ref[0, 0].astype(BF16)
        wd_s[...] = wd_ref[0, 0].astype(BF16)

    @pl.when(i < used)
    def _():
        for_each_group(i, lambda r, p: gather_copy(src_ref, 0, cur).wait())
        x = jnp.concatenate([_load_token_slab(x_buf.at[cur], EXPERT_BLOCK, ROW_TILE, c) for c in range(ROW_TILE)],
                            axis=1).astype(BF16)
        hidden = _silu(_dot(x, wg_s[...])) * _dot(x, wu_s[...])
        _store_token_tiles(y_buf.at[cur], _dot(hidden.astype(BF16), wd_s[...]), ROW_TILE)

    @pl.when(has_prev)
    def _():
        for_each_group(i - 1, lambda r, p: scatter_copy(dst_prev_ref, 0, 1 - cur).wait())

    @pl.when(i == used - 1)
    def _():
        for_each_group(i, lambda r, p: scatter_copy(dst_ref, r, cur).start(priority=p))
        for_each_group(i, lambda r, p: scatter_copy(dst_ref, 0, cur).wait())


def _experts(h2_tiles, plan, w_gate, w_up, w_down, layer):
    src_token, dst_slot, block_expert, copies, used_blocks = plan
    n_blocks = src_token.shape[0]
    n_slots = 2 * (h2_tiles.shape[0] // ROW_TILE) + 2 * GROUP
    last = n_blocks - 1

    def smem_block(index):
        return pl.BlockSpec((1, 1, EXPERT_BLOCK), index, memory_space=pltpu.SMEM)

    def weight(shape):
        return pl.BlockSpec((1, 1) + shape, lambda i, be, copies, used: (layer, be[i], 0, 0))

    grid_spec = pltpu.PrefetchScalarGridSpec(
        num_scalar_prefetch=3,
        grid=(n_blocks,),
        in_specs=[
            smem_block(lambda i, be, copies, used: (i, 0, 0)),
            smem_block(lambda i, be, copies, used: (jnp.minimum(i + 1, last), 0, 0)),
            smem_block(lambda i, be, copies, used: (i, 0, 0)),
            smem_block(lambda i, be, copies, used: (jnp.maximum(i - 1, 0), 0, 0)),
            pl.BlockSpec(memory_space=pl.ANY),
            weight((D_MODEL, EXPERT_FF)), weight((D_MODEL, EXPERT_FF)), weight((EXPERT_FF, D_MODEL)),
        ],
        out_specs=pl.BlockSpec(memory_space=pl.ANY),
        scratch_shapes=[
            pltpu.VMEM((2, EXPERT_BLOCK * ROW_TILE, LANES), F32),
            pltpu.VMEM((2, EXPERT_BLOCK * ROW_TILE, LANES), F32),
            pltpu.VMEM((D_MODEL, EXPERT_FF), BF16),
            pltpu.VMEM((D_MODEL, EXPERT_FF), BF16),
            pltpu.VMEM((EXPERT_FF, D_MODEL), BF16),
            pltpu.SemaphoreType.DMA((2,)),
            pltpu.SemaphoreType.DMA((2,)),
        ],
    )
    return pl.pallas_call(
        _expert_kernel,
        grid_spec=grid_spec,
        out_shape=jax.ShapeDtypeStruct((n_slots * ROW_TILE, LANES), F32),
        compiler_params=_params(("arbitrary",), VMEM_LIMIT),
        name="experts",
    )(block_expert, copies, used_blocks, src_token, src_token, dst_slot, dst_slot, h2_tiles, w_gate, w_up, w_down)


def _final_kernel(slots_ref, x_ref, route_ref, mod_ref, fg_ref, o_ref):
    sq = _moe_residual(slots_ref, x_ref, route_ref, mod_ref, o_ref)
    o_ref[...] = o_ref[...] * lax.rsqrt(sq / D_MODEL + EPS) * fg_ref[...]


def _final_combine(slots, x_all, route, mod, final_g, n_batch):
    tiles = TOK // TM
    lat_tiles = SEQ // TM
    tok = lambda b, j: (b * tiles + CTX_LEN // TM + j, 0)
    return pl.pallas_call(
        _final_kernel,
        grid=(n_batch, lat_tiles),
        in_specs=[
            pl.BlockSpec((TM * 2 * ROW_TILE, LANES), tok),
            pl.BlockSpec((TM, D_MODEL), tok),
            pl.BlockSpec((TM, LANES), tok),
            pl.BlockSpec((1, 6, D_MODEL), lambda b, j: (b, 0, 0)),
            pl.BlockSpec((1, D_MODEL), lambda b, j: (0, 0)),
        ],
        out_specs=pl.BlockSpec((TM, D_MODEL), lambda b, j: (b * lat_tiles + j, 0)),
        out_shape=jax.ShapeDtypeStruct((n_batch * SEQ, D_MODEL), F32),
        compiler_params=_params(("parallel", "parallel")),
        name="final_combine",
    )(slots, x_all, route, mod, final_g.reshape(1, D_MODEL))


def _rope_tables():
    t = jnp.arange(SEQ)
    rows = (t // GRID_W).astype(F32)
    cols = (t % GRID_W).astype(F32)
    n_freq = HEAD_DIM // 4
    inv_freq = ROPE_BASE ** (-jnp.arange(n_freq, dtype=F32) / n_freq)
    ang_r = rows[:, None] * inv_freq
    ang_c = cols[:, None] * inv_freq
    cos = jnp.concatenate([jnp.cos(ang_r)] * 2 + [jnp.cos(ang_c)] * 2, axis=1)
    sin = jnp.concatenate([-jnp.sin(ang_r), jnp.sin(ang_r), -jnp.sin(ang_c), jnp.sin(ang_c)], axis=1)
    return cos, sin


def kernel(x, c, ctx, c_ctx, w_mod, b_mod, norm1_g, norm2_g, w_in, ret_decay, ret_norm_g, conv_w, conv_b,
           mlstm_gate_b, mlstm_norm_g, na_rpb, w_branch, w_out, w_group, w_router, w_expert_gate,
           w_expert_up, w_expert_down, final_norm_g):
    n_batch, seq, d = x.shape
    depth = w_mod.shape[0]
    assert (seq, d, ctx.shape[1]) == (SEQ, D_MODEL, CTX_LEN)
    n = n_batch * TOK

    cond_rows = -(-(n_batch + 1) // 8) * 8
    cond = jnp.zeros((cond_rows, d), F32).at[:n_batch].set(c).at[n_batch].set(c_ctx)
    mod_all = _modulation(cond, w_mod, b_mod).reshape(depth, cond_rows, 6, d)

    cos, sin = _rope_tables()
    x_all = jnp.concatenate([ctx, x], axis=1).reshape(n, d)

    w_packed = _pack_in_weights(w_in)
    w_branch_b = w_branch.astype(BF16)
    w_out_b = w_out.astype(BF16)
    w_route = jnp.concatenate([w_group, w_router], axis=-1)
    w_route = jnp.pad(w_route, ((0, 0), (0, 0), (0, LANES - w_route.shape[-1])))
    wr_hi = w_route.astype(BF16)
    wr_lo = (w_route - wr_hi.astype(F32)).astype(BF16)
    w_route_split = jnp.concatenate([wr_hi, wr_lo], axis=-1)
    gate_b = jnp.pad(mlstm_gate_b.reshape(depth, 1, 4 * N_HEADS), ((0, 0), (0, 0), (0, LANES - 4 * N_HEADS)))
    na_bias = _na_bias_tables(na_rpb)

    moe = None
    for layer in range(depth):
        mod = mod_all[layer]
        if moe is None:
            ret, ml, gates, na, bg = _in_projection(x_all, mod, norm1_g, w_packed, layer, n_batch)
        else:
            x_all, ret, ml, gates, na, bg = _in_projection(x_all, mod, norm1_g, w_packed, layer, n_batch, moe)
        r_out = _retention(ret, ret_decay[layer], ret_norm_g[layer], cos, sin, n_batch)
        m_out = _mlstm(ml, gates, gate_b[layer], conv_w[layer], conv_b[layer].reshape(1, 2 * WIDTH),
                       mlstm_norm_g[layer], n_batch)
        a_out = _neighbourhood_attention(na, na_bias, layer, n_batch)
        x_all, h2, logits = _merge(r_out, m_out, a_out, bg, x_all, mod, w_branch_b, w_out_b, norm2_g,
                                   w_route_split, layer, n_batch)
        route, counts = _route(logits)
        plan = _dispatch_plan(route, counts)
        slots = _experts(h2, plan, w_expert_gate, w_expert_up, w_expert_down, layer)
        moe = (slots, route, mod)

    return _final_combine(slots, x_all, route, mod, final_norm_g, n_batch).reshape(n_batch, SEQ, d)
```

```python
import functools

import numpy as np
import jax
import jax.numpy as jnp
from jax import lax
from jax.experimental import pallas as pl
from jax.experimental.pallas import tpu as pltpu

F32 = jnp.float32
BF16 = jnp.bfloat16

D_MODEL = 1024
SEQ = 2048
CTX_LEN = 256
TOK = CTX_LEN + SEQ
GRID_W = 64
GRID_ROWS = SEQ // GRID_W
HEAD_DIM = 128
N_HEADS = 4
WIDTH = N_HEADS * HEAD_DIM
CHUNK = 256
N_CHUNKS = TOK // CHUNK
CTX_CHUNKS = CTX_LEN // CHUNK
CONV_WIDTH = 5
NA_WIN_ROWS = 8
NA_WIN_COLS = 16
NA_Q_ROWS = 4
NA_Q = NA_Q_ROWS * GRID_W
NA_K_ROWS = NA_Q_ROWS + NA_WIN_ROWS
NA_K = NA_K_ROWS * GRID_W
NA_STEPS = SEQ // NA_Q
ROPE_BASE = 10000.0
N_GROUPS = 4
EXPERTS_PER_GROUP = 8
N_EXPERTS = N_GROUPS * EXPERTS_PER_GROUP
EXPERT_FF = 512
EXPERT_BLOCK = 256
EPS = 1e-6
NEG_INF = -1e30
QK_SCALE = HEAD_DIM ** -0.5

TM = 256
ROUTE_TM = 512
LANES = 128
VMEM_LIMIT = 56 * 1024 * 1024


def _dot(a, b):
    return jnp.dot(a, b, preferred_element_type=F32)


def _dot_nt(a, b):
    return lax.dot_general(a, b, (((1,), (1,)), ((), ())), preferred_element_type=F32)


def _bdot(a, b):
    return _dot(a.astype(BF16), b.astype(BF16))


def _bdot_nt(a, b):
    return _dot_nt(a.astype(BF16), b.astype(BF16))


def _bdot_tn(a, b):
    return _dot(a.T.astype(BF16), b.astype(BF16))


def _log_sigmoid(x):
    return jnp.minimum(x, 0.0) - jnp.log1p(jnp.exp(-jnp.abs(x)))


def _silu(x):
    return x * jax.nn.sigmoid(x)


def _params(sem, vmem=None):
    return pltpu.CompilerParams(dimension_semantics=sem, vmem_limit_bytes=vmem)


def _mod_row(n_batch):
    tiles = TOK // TM

    def index(i):
        return (jnp.where(i % tiles == 0, n_batch, i // tiles), 0, 0)

    return index


def _mod_kernel(c_ref, w_ref, b_ref, o_ref):
    cond = _silu(c_ref[...])
    o_ref[0] = _bdot(cond, w_ref[0]) + b_ref[0]


def _modulation(cond, w_mod, b_mod):
    depth, d, cols = w_mod.shape
    rows = cond.shape[0]
    tn = 1536
    return pl.pallas_call(
        _mod_kernel,
        grid=(depth, cols // tn),
        in_specs=[
            pl.BlockSpec((rows, d), lambda l, j: (0, 0)),
            pl.BlockSpec((1, d, tn), lambda l, j: (l, 0, j)),
            pl.BlockSpec((1, 1, tn), lambda l, j: (l, 0, j)),
        ],
        out_specs=pl.BlockSpec((1, rows, tn), lambda l, j: (l, 0, j)),
        out_shape=jax.ShapeDtypeStruct((depth, rows, cols), F32),
        compiler_params=_params(("parallel", "parallel")),
        name="modulation",
    )(cond, w_mod, b_mod.reshape(depth, 1, cols))


def _rms_modulate(x, g, shift, scale):
    y = x * lax.rsqrt(jnp.mean(x * x, axis=-1, keepdims=True) + EPS) * g
    return y * (1.0 + scale) + shift


IN_WIDTHS = (4 * WIDTH, 4 * WIDTH, LANES, 3 * WIDTH, 3 * D_MODEL)
IN_OFFSETS = tuple(int(v) for v in np.cumsum((0,) + IN_WIDTHS))


def _pack_kernel(w_ref, o_ref):
    gate0 = IN_OFFSETS[2]
    n_gate = 4 * N_HEADS
    o_ref[0, :, :gate0] = w_ref[0, :, :gate0].astype(BF16)
    window = w_ref[0, :, gate0:gate0 + LANES]
    lane = lax.broadcasted_iota(jnp.int32, window.shape, 1)
    o_ref[0, :, gate0:gate0 + LANES] = jnp.where(lane < n_gate, window, 0.0).astype(BF16)
    o_ref[0, :, gate0 + LANES:] = w_ref[0, :, gate0 + n_gate:].astype(BF16)


def _pack_in_weights(w_in):
    depth, d, cols = w_in.shape
    rows = 256
    return pl.pallas_call(
        _pack_kernel,
        grid=(depth, d // rows),
        in_specs=[pl.BlockSpec((1, rows, cols), lambda l, r: (l, r, 0))],
        out_specs=pl.BlockSpec((1, rows, IN_OFFSETS[-1]), lambda l, r: (l, r, 0)),
        out_shape=jax.ShapeDtypeStruct((depth, d, IN_OFFSETS[-1]), BF16),
        compiler_params=_params(("parallel", "parallel"), VMEM_LIMIT),
        name="pack_in_weights",
    )(w_in)


def _project(x, mod_ref, g_ref, w_ref, out_refs):
    h = _rms_modulate(x, g_ref[0], mod_ref[0, 0:1, :], mod_ref[0, 1:2, :]).astype(BF16)
    for o_ref, lo, hi in zip(out_refs, IN_OFFSETS[:-1], IN_OFFSETS[1:]):
        o_ref[...] = _dot(h, w_ref[0, :, lo:hi])


def _inproj_kernel(x_ref, mod_ref, g_ref, w_ref, *out_refs):
    _project(x_ref[...], mod_ref, g_ref, w_ref, out_refs)


def _combine_inproj_kernel(dest_ref, dest_next_ref, ys_ref, x_ref, route_ref, prev_mod_ref, mod_ref, g_ref, w_ref,
                           x_out_ref, *rest):
    out_refs, (y_buf, sem) = rest[:-2], rest[-2:]
    rows_ref = _expert_rows(ys_ref, dest_ref, dest_next_ref, y_buf, sem, pl.program_id(0), pl.num_programs(0))
    _moe_residual(rows_ref, x_ref, route_ref, prev_mod_ref, x_out_ref)
    _project(x_out_ref[...], mod_ref, g_ref, w_ref, out_refs)


def _in_projection(x_all, mod, norm_g, w_packed, layer, n_batch, moe=None):
    n = x_all.shape[0]
    row = lambda i: (i, 0)
    in_specs = [
        pl.BlockSpec((TM, D_MODEL), row),
        pl.BlockSpec((1, 6, D_MODEL), _mod_row(n_batch)),
        pl.BlockSpec((1, 1, D_MODEL), lambda i: (layer, 0, 0)),
        pl.BlockSpec((1,) + w_packed.shape[1:], lambda i: (layer, 0, 0), pipeline_mode=pl.Buffered(1)),
    ]
    out_specs = [pl.BlockSpec((TM, w), row) for w in IN_WIDTHS]
    out_shape = [jax.ShapeDtypeStruct((n, w), F32) for w in IN_WIDTHS]
    args = (x_all, mod, norm_g.reshape(-1, 1, D_MODEL), w_packed)
    scratch = []
    if moe is None:
        body = _inproj_kernel
    else:
        ys, dest_tiles, route, prev_mod = moe
        body = _combine_inproj_kernel
        last = n // TM - 1
        dest_block = lambda index: pl.BlockSpec((1, 1, 2 * TM), index, memory_space=pltpu.SMEM)
        in_specs = ([dest_block(lambda i: (i, 0, 0)), dest_block(lambda i: (jnp.minimum(i + 1, last), 0, 0)),
                     pl.BlockSpec(memory_space=pl.ANY), in_specs[0], pl.BlockSpec((TM, LANES), row),
                     pl.BlockSpec((1, 6, D_MODEL), _mod_row(n_batch))] + in_specs[1:])
        out_specs = [pl.BlockSpec((TM, D_MODEL), row)] + out_specs
        out_shape = [jax.ShapeDtypeStruct((n, D_MODEL), F32)] + out_shape
        scratch = [pltpu.VMEM((2, TM * 2 * ROW_TILE, LANES), F32), pltpu.SemaphoreType.DMA((2,))]
        args = (dest_tiles, dest_tiles, ys, x_all, route, prev_mod) + args[1:]
    return pl.pallas_call(
        body,
        grid=(n // TM,),
        in_specs=in_specs,
        out_specs=out_specs,
        out_shape=out_shape,
        scratch_shapes=scratch,
        compiler_params=_params(("arbitrary",), VMEM_LIMIT),
        name="in_projection",
    )(*args)


def _chunk_order(t):
    fwd = t
    bwd = jnp.where(t < CTX_CHUNKS, CTX_CHUNKS - 1 - t, N_CHUNKS + CTX_CHUNKS - 1 - t)
    return fwd, bwd


def _chunk_slice(c):
    return pl.ds(pl.multiple_of(c * CHUNK, CHUNK), CHUNK)


def _head_norm(y, gain):
    mu = jnp.mean(y, axis=-1, keepdims=True)
    yc = y - mu
    var = jnp.mean(yc * yc, axis=-1, keepdims=True)
    return yc * lax.rsqrt(var + EPS) * gain


ROW_TILE = D_MODEL // LANES


def _store_token_tiles(ref, x, pitch, offset=0):
    rows = x.shape[0]
    for c in range(ROW_TILE):
        ref[pl.ds(offset + c, rows, stride=pitch), :] = x[:, c * LANES:(c + 1) * LANES]


def _load_token_slab(ref, rows, pitch, c, offset=0):
    return ref[pl.ds(offset + c, rows, stride=pitch), :]


def _expert_rows(ys_ref, dest_ref, dest_next_ref, y_buf, sem, step, n_steps):
    cur = step % 2

    def copy(idx_ref, t, k, buf):
        return pltpu.make_async_copy(ys_ref.at[_tile_rows(idx_ref[0, 0, k * TM + t])],
                                     y_buf.at[buf, _tile_rows(2 * t + k)], sem.at[buf])

    def for_each_group(fn):
        def body(g, carry):
            for j in range(GROUP):
                for k in range(2):
                    fn(g * GROUP + j, k)
            return carry
        lax.fori_loop(0, TM // GROUP, body, 0)

    @pl.when(step == 0)
    def _():
        for_each_group(lambda t, k: copy(dest_ref, t, k, 0).start(priority=k))

    @pl.when(step + 1 < n_steps)
    def _():
        for_each_group(lambda t, k: copy(dest_next_ref, t, k, 1 - cur).start(priority=k))

    for_each_group(lambda t, k: copy(dest_ref, 0, 0, cur).wait())
    return y_buf.at[cur]


def _moe_residual(slots_ref, x_ref, route_ref, mod_ref, o_ref):
    route = route_ref[...]
    w1 = route[:, ROUTE_W1:ROUTE_W1 + 1]
    w2 = route[:, ROUTE_W2:ROUTE_W2 + 1]
    rows = x_ref.shape[0]
    sq = jnp.zeros((rows, 1), F32)
    for c in range(ROW_TILE):
        lanes = slice(c * LANES, (c + 1) * LANES)
        y1 = _load_token_slab(slots_ref, rows, 2 * ROW_TILE, c)
        y2 = _load_token_slab(slots_ref, rows, 2 * ROW_TILE, c, offset=ROW_TILE)
        x_new = x_ref[:, lanes] + mod_ref[0, 5:6, lanes] * (y1 * w1 + y2 * w2)
        o_ref[:, lanes] = x_new
        sq = sq + jnp.sum(x_new * x_new, axis=-1, keepdims=True)
    return sq


def _select_lane(x, lane, idx):
    return jnp.sum(jnp.where(lane == idx, x, 0.0), axis=-1, keepdims=True)


HEADS_PER_STEP = 2
HEAD_STEPS = N_HEADS // HEADS_PER_STEP
STEP_WIDTH = HEADS_PER_STEP * HEAD_DIM
SCAN_VMEM = 48 * 1024 * 1024


def _head_lanes(hh):
    return slice(hh * HEAD_DIM, (hh + 1) * HEAD_DIM)


def _retention_kernel(dec_ref, q_ref, k_ref, v_ref, g_ref, cos_ref, sin_ref, gn_ref, o_ref,
                      qs_ref, ks_ref, ob_ref, intra_ref, st_ref):
    first_head = pl.program_id(1) * HEADS_PER_STEP

    lane = lax.broadcasted_iota(jnp.int32, (CHUNK, HEAD_DIM), 1)
    first_half = (lane % (HEAD_DIM // 2)) < (HEAD_DIM // 4)

    def rope(x, cos, sin):
        rot = jnp.where(first_half, pltpu.roll(x, HEAD_DIM - HEAD_DIM // 4, 1), pltpu.roll(x, HEAD_DIM // 4, 1))
        return x * cos + rot * sin

    qs_ref[0:CTX_LEN, :] = q_ref[0:CTX_LEN, :]
    ks_ref[0:CTX_LEN, :] = k_ref[0:CTX_LEN, :] * QK_SCALE

    def rotate_chunk(c, carry):
        rows = _chunk_slice(c)
        pos = _chunk_slice(c - CTX_CHUNKS)
        cos = cos_ref[pos, :]
        sin = sin_ref[pos, :]
        for hh in range(HEADS_PER_STEP):
            qs_ref[rows, _head_lanes(hh)] = rope(q_ref[rows, _head_lanes(hh)], cos, sin)
            ks_ref[rows, _head_lanes(hh)] = rope(k_ref[rows, _head_lanes(hh)], cos, sin) * QK_SCALE
        return carry

    lax.fori_loop(CTX_CHUNKS, N_CHUNKS, rotate_chunk, 0)

    ii = lax.broadcasted_iota(jnp.int32, (CHUNK, CHUNK), 0).astype(F32)
    jj = lax.broadcasted_iota(jnp.int32, (CHUNK, CHUNK), 1).astype(F32)
    col = lax.broadcasted_iota(jnp.int32, (CHUNK, 1), 0).astype(F32)

    def decay_mat(dist, lg_dir):
        ok = dist >= 0
        return jnp.where(ok, jnp.exp(jnp.where(ok, dist, 0.0) * lg_dir), 0.0)

    dec = _log_sigmoid(dec_ref[...])
    hl = lax.broadcasted_iota(jnp.int32, dec.shape, 1)
    consts = []
    for hh in range(HEADS_PER_STEP):
        lg = jnp.sum(jnp.where(hl == first_head + hh, dec, 0.0), axis=-1, keepdims=True)
        lg_f, lg_b = lg[0:1, :], lg[1:2, :]
        intra_ref[2 * hh] = decay_mat(ii - jj, lg_f)
        intra_ref[2 * hh + 1] = decay_mat(jj - ii, lg_b)
        consts.append(dict(
            q_decay=(jnp.exp((col + 1.0) * lg_f), jnp.exp((CHUNK - col) * lg_b)),
            k_decay=(jnp.exp((CHUNK - 1.0 - col) * lg_f), jnp.exp(col * lg_b)),
            chunk_decay=(jnp.exp(CHUNK * lg_f), jnp.exp(CHUNK * lg_b))))

    st_ref[...] = jnp.zeros_like(st_ref)
    out_refs = (o_ref, ob_ref)

    def step(t, carry):
        for hh in range(HEADS_PER_STEP):
            cst = consts[hh]
            for d, c in enumerate(_chunk_order(t)):
                rows = _chunk_slice(c)
                q = qs_ref[rows, _head_lanes(hh)]
                k = ks_ref[rows, _head_lanes(hh)]
                v = v_ref[rows, _head_lanes(hh)]
                s_prev = st_ref[2 * hh + d]
                scores = _bdot_nt(q, k) * intra_ref[2 * hh + d]
                out_refs[d][rows, _head_lanes(hh)] = _bdot(scores, v) + _bdot(q * cst["q_decay"][d], s_prev)
                st_ref[2 * hh + d] = s_prev * cst["chunk_decay"][d] + _bdot_tn(k * cst["k_decay"][d], v)
        return carry

    lax.fori_loop(0, N_CHUNKS, step, 0)

    def finish_chunk(c, carry):
        rows = _chunk_slice(c)
        for hh in range(HEADS_PER_STEP):
            lanes = _head_lanes(hh)
            y = o_ref[rows, lanes] + ob_ref[rows, lanes]
            o_ref[rows, lanes] = _head_norm(y, gn_ref[:, lanes]) * _silu(g_ref[rows, lanes])
        return carry

    lax.fori_loop(0, N_CHUNKS, finish_chunk, 0)


def _retention(ret, ret_decay, norm_g, cos, sin, n_batch):
    n = ret.shape[0]

    def head_block(offset):
        return pl.BlockSpec((TOK, STEP_WIDTH), lambda b, h: (b, offset * HEAD_STEPS + h))

    return pl.pallas_call(
        _retention_kernel,
        grid=(n_batch, HEAD_STEPS),
        in_specs=[
            pl.BlockSpec(ret_decay.shape, lambda b, h: (0, 0)),
            head_block(0), head_block(1), head_block(2), head_block(3),
            pl.BlockSpec((SEQ, HEAD_DIM), lambda b, h: (0, 0)),
            pl.BlockSpec((SEQ, HEAD_DIM), lambda b, h: (0, 0)),
            pl.BlockSpec((1, STEP_WIDTH), lambda b, h: (0, h)),
        ],
        out_specs=pl.BlockSpec((TOK, STEP_WIDTH), lambda b, h: (b, h)),
        out_shape=jax.ShapeDtypeStruct((n, WIDTH), F32),
        scratch_shapes=[
            pltpu.VMEM((TOK, STEP_WIDTH), F32),
            pltpu.VMEM((TOK, STEP_WIDTH), F32),
            pltpu.VMEM((TOK, STEP_WIDTH), F32),
            pltpu.VMEM((2 * HEADS_PER_STEP, CHUNK, CHUNK), F32),
            pltpu.VMEM((2 * HEADS_PER_STEP, HEAD_DIM, HEAD_DIM), F32),
        ],
        compiler_params=_params(("parallel", "parallel"), SCAN_VMEM),
        name="retention",
    )(ret_decay, ret, ret, ret, ret, cos, sin, norm_g.reshape(1, WIDTH))


CONV_PAD = 8
CONV_ROWS = 128


def _mlstm_kernel(q_ref, k_ref, v_ref, og_ref, gate_ref, gb_ref, cwq_ref, cwk_ref, cbq_ref, cbk_ref,
                  gn_ref, o_ref, qs_ref, ks_ref, ob_ref, pad_ref, gx_ref, gxt_ref, cst_ref):
    first_head = pl.program_id(1) * HEADS_PER_STEP

    pad_ref[0:CONV_PAD, :] = jnp.zeros((CONV_PAD, STEP_WIDTH), F32)
    pad_ref[CONV_PAD + TOK:, :] = jnp.zeros((CONV_PAD, STEP_WIDTH), F32)
    crow = lax.broadcasted_iota(jnp.int32, (CONV_ROWS, 1), 0)

    def conv(u_ref, w_ref, b_ref, dst_ref, scale):
        def fill(c, carry):
            rows = _chunk_slice(c)
            pad_ref[pl.ds(pl.multiple_of(c * CHUNK + CONV_PAD, CONV_PAD), CHUNK), :] = u_ref[rows, :]
            return carry

        lax.fori_loop(0, N_CHUNKS, fill, 0)

        for blk in range(TOK // CONV_ROWS):
            first = blk * CONV_ROWS
            acc = jnp.zeros((CONV_ROWS, STEP_WIDTH), F32)
            for j in range(CONV_WIDTH):
                shift = j - CONV_WIDTH // 2
                tap = pad_ref[first + CONV_PAD + shift:first + CONV_PAD + shift + CONV_ROWS, :]
                if (first + CONV_ROWS == CTX_LEN and shift > 0) or (first == CTX_LEN and shift < 0):
                    trow = crow + first
                    tap = jnp.where((trow < CTX_LEN) == (trow + shift < CTX_LEN), tap, 0.0)
                acc = acc + tap * w_ref[j:j + 1, :]
            dst_ref[first:first + CONV_ROWS, :] = _silu(acc + b_ref[...]) * scale

    conv(q_ref, cwq_ref, cbq_ref, qs_ref, 1.0)
    conv(k_ref, cwk_ref, cbk_ref, ks_ref, QK_SCALE)

    clane = lax.broadcasted_iota(jnp.int32, (CHUNK, LANES), 1)
    is_forget = ((clane // N_HEADS) % 2) == 1

    def gate_chunk(c, carry):
        rows = _chunk_slice(c)
        g = gate_ref[rows, :] + gb_ref[...]
        gx = jnp.where(is_forget, _log_sigmoid(g), g)
        gx_ref[rows, :] = gx
        gxt_ref[c] = gx.T
        return carry

    lax.fori_loop(0, N_CHUNKS, gate_chunk, 0)

    ii = lax.broadcasted_iota(jnp.int32, (CHUNK, CHUNK), 0)
    jj = lax.broadcasted_iota(jnp.int32, (CHUNK, CHUNK), 1)
    causal = (jj <= ii, jj >= ii)

    cst_ref[...] = jnp.zeros_like(cst_ref)
    out_refs = (o_ref, ob_ref)

    def step(t, carry):
        new_carry = []
        for hh in range(HEADS_PER_STEP):
            head = first_head + hh
            lanes = _head_lanes(hh)
            for d, c in enumerate(_chunk_order(t)):
                slot = 2 * hh + d
                n_prev, m_prev = carry[2 * slot], carry[2 * slot + 1]
                rows = _chunk_slice(c)
                q = qs_ref[rows, lanes]
                k = ks_ref[rows, lanes]
                v = v_ref[rows, lanes]
                gc = gx_ref[rows, :]
                i_col = _select_lane(gc, clane, 2 * d * N_HEADS + head)
                f_col = _select_lane(gc, clane, (2 * d + 1) * N_HEADS + head)
                i_row = gxt_ref[c, pl.ds(2 * d * N_HEADS + head, 1), :]
                f_row = gxt_ref[c, pl.ds((2 * d + 1) * N_HEADS + head, 1), :]
                vis = causal[d]
                cum_col = jnp.sum(jnp.where(vis, f_row, 0.0), axis=1, keepdims=True)
                cum_row = jnp.sum(jnp.where(causal[1 - d], f_col, 0.0), axis=0, keepdims=True)
                total = jnp.sum(f_row, axis=1, keepdims=True)
                c_prev = cst_ref[slot]

                log_kw = total - cum_col + i_col
                m_new = jnp.maximum(total + m_prev, jnp.max(log_kw, axis=0, keepdims=True))
                kw = jnp.exp(log_kw - m_new)
                pw = jnp.exp(total + m_prev - m_new)
                cst_ref[slot] = pw * c_prev + _bdot_tn(k * kw, v)
                n_new = pw * n_prev + jnp.sum(kw * k, axis=0, keepdims=True)

                log_w = jnp.where(vis, cum_col - cum_row + i_row, -jnp.inf)
                log_p = cum_col + m_prev
                m_t = jnp.maximum(log_p, jnp.max(log_w, axis=1, keepdims=True))
                w = jnp.exp(log_w - m_t)
                p = jnp.exp(log_p - m_t)
                qk = _bdot_nt(q, k) * w
                num = _bdot(qk, v) + p * _bdot(q, c_prev)
                den = jnp.sum(qk, axis=1, keepdims=True) + p * jnp.sum(q * n_prev, axis=1, keepdims=True)
                out_refs[d][rows, lanes] = num / jnp.maximum(jnp.abs(den), jnp.exp(-m_t))
                new_carry += [n_new, m_new]
        return tuple(new_carry)

    zero_n = jnp.zeros((1, HEAD_DIM), F32)
    zero_m = jnp.zeros((1, 1), F32)
    lax.fori_loop(0, N_CHUNKS, step, (zero_n, zero_m) * (2 * HEADS_PER_STEP))

    def finish_chunk(c, carry):
        rows = _chunk_slice(c)
        for hh in range(HEADS_PER_STEP):
            lanes = _head_lanes(hh)
            gated = (o_ref[rows, lanes] + ob_ref[rows, lanes]) * jax.nn.sigmoid(og_ref[rows, lanes])
            o_ref[rows, lanes] = _head_norm(gated, gn_ref[:, lanes])
        return carry

    lax.fori_loop(0, N_CHUNKS, finish_chunk, 0)


def _mlstm(ml, gates, gate_b, conv_w, conv_b, norm_g, n_batch):
    n = ml.shape[0]

    def head_block(offset):
        return pl.BlockSpec((TOK, STEP_WIDTH), lambda b, h: (b, offset * HEAD_STEPS + h))

    def head_cols(rows, offset):
        return pl.BlockSpec((rows, STEP_WIDTH), lambda b, h: (0, offset * HEAD_STEPS + h))

    return pl.pallas_call(
        _mlstm_kernel,
        grid=(n_batch, HEAD_STEPS),
        in_specs=[
            head_block(0), head_block(1), head_block(2), head_block(3),
            pl.BlockSpec((TOK, LANES), lambda b, h: (b, 0)),
            pl.BlockSpec((1, LANES), lambda b, h: (0, 0)),
            head_cols(CONV_WIDTH, 0), head_cols(CONV_WIDTH, 1),
            head_cols(1, 0), head_cols(1, 1),
            head_cols(1, 0),
        ],
        out_specs=pl.BlockSpec((TOK, STEP_WIDTH), lambda b, h: (b, h)),
        out_shape=jax.ShapeDtypeStruct((n, WIDTH), F32),
        scratch_shapes=[
            pltpu.VMEM((TOK, STEP_WIDTH), F32),
            pltpu.VMEM((TOK, STEP_WIDTH), F32),
            pltpu.VMEM((TOK, STEP_WIDTH), F32),
            pltpu.VMEM((TOK + 2 * CONV_PAD, STEP_WIDTH), F32),
            pltpu.VMEM((TOK, LANES), F32),
            pltpu.VMEM((N_CHUNKS, LANES, CHUNK), F32),
            pltpu.VMEM((2 * HEADS_PER_STEP, HEAD_DIM, HEAD_DIM), F32),
        ],
        compiler_params=_params(("parallel", "parallel"), SCAN_VMEM),
        name="mlstm",
    )(ml, ml, ml, ml, gates, gate_b, conv_w, conv_w, conv_b, conv_b, norm_g.reshape(1, WIDTH))


def _na_kernel(q_ref, k_ref, v_ref, bias_ref, o_ref):
    step = pl.program_id(2)

    def context_scores(hh):
        lanes = _head_lanes(hh)
        q = q_ref[:, lanes].astype(BF16)
        v_ctx = v_ref[0:CTX_LEN, lanes].astype(BF16)
        return q, v_ctx, _dot_nt(q, k_ref[0:CTX_LEN, lanes].astype(BF16)) * QK_SCALE

    @pl.when(step == 0)
    def _():
        for hh in range(HEADS_PER_STEP):
            _, v_ctx, s_ctx = context_scores(hh)
            m = jnp.max(s_ctx, axis=-1, keepdims=True)
            p = jnp.exp(s_ctx - m)
            o_ref[:, _head_lanes(hh)] = _dot(p.astype(BF16), v_ctx) / jnp.sum(p, axis=-1, keepdims=True)

    @pl.when(step > 0)
    def _():
        start = CTX_LEN + NA_Q * jnp.clip(step - 2, 0, NA_STEPS - NA_K_ROWS // NA_Q_ROWS)
        rows = pl.ds(pl.multiple_of(start, NA_Q), NA_K)
        for hh in range(HEADS_PER_STEP):
            lanes = _head_lanes(hh)
            q, v_ctx, s_ctx = context_scores(hh)
            s_loc = _dot_nt(q, k_ref[rows, lanes].astype(BF16)) * QK_SCALE + bias_ref[0, hh, 0]
            m = jnp.maximum(jnp.max(s_loc, axis=-1, keepdims=True), jnp.max(s_ctx, axis=-1, keepdims=True))
            p_loc = jnp.exp(s_loc - m)
            p_ctx = jnp.exp(s_ctx - m)
            denom = jnp.sum(p_loc, axis=-1, keepdims=True) + jnp.sum(p_ctx, axis=-1, keepdims=True)
            o = _dot(p_loc.astype(BF16), v_ref[rows, lanes].astype(BF16)) + _dot(p_ctx.astype(BF16), v_ctx)
            o_ref[:, lanes] = o / denom


def _na_row_case(case, a):
    t = np.arange(NA_K_ROWS)
    last_start = NA_K_ROWS - NA_WIN_ROWS
    return [
        (t < NA_WIN_ROWS, NA_WIN_ROWS - 1 - a),
        ((t >= a) & (t < a + NA_WIN_ROWS), NA_WIN_ROWS // 2 - 1 - a),
        (t >= last_start, NA_Q_ROWS - NA_K_ROWS + NA_WIN_ROWS - 1 - a),
    ][case]


def _na_expand_kernel(slab_ref, o_ref):
    masked = jnp.full((GRID_W, GRID_W), NEG_INF, F32)
    for case in range(3):
        for a in range(NA_Q_ROWS):
            row_ok, first = _na_row_case(case, a)
            for pair in range(NA_K_ROWS // 2):
                tiles = [slab_ref[0, first + t] if row_ok[t] else masked for t in (2 * pair, 2 * pair + 1)]
                o_ref[0, case, a * GRID_W:(a + 1) * GRID_W, pair * LANES:(pair + 1) * LANES] = (
                    jnp.concatenate(tiles, axis=1))


def _na_bias_tables(rpb):
    lead = rpb.shape[:-2]
    n_row_off, n_col_off = rpb.shape[-2:]
    qc = np.arange(GRID_W)[:, None]
    kc = np.arange(GRID_W)[None, :]
    col_start = np.clip(qc - NA_WIN_COLS // 2, 0, GRID_W - NA_WIN_COLS)
    col_ok = (kc >= col_start) & (kc < col_start + NA_WIN_COLS)
    col_idx = np.clip(kc - qc + NA_WIN_COLS - 1, 0, n_col_off - 1)
    onehot = ((col_idx[None] == np.arange(n_col_off)[:, None, None]) & col_ok[None]).astype(np.float32)
    slabs = jnp.einsum('...rc,cqk->...rqk', rpb.astype(F32), onehot, precision=lax.Precision.HIGHEST)
    slabs = jnp.where(col_ok, slabs, NEG_INF).reshape(-1, n_row_off, GRID_W, GRID_W)
    n_tables = slabs.shape[0]
    tables = pl.pallas_call(
        _na_expand_kernel,
        grid=(n_tables,),
        in_specs=[pl.BlockSpec((1, n_row_off, GRID_W, GRID_W), lambda i: (i, 0, 0, 0))],
        out_specs=pl.BlockSpec((1, 3, NA_Q, NA_K), lambda i: (i, 0, 0, 0)),
        out_shape=jax.ShapeDtypeStruct((n_tables, 3, NA_Q, NA_K), F32),
        compiler_params=_params(("parallel",)),
        name="na_bias_tables",
    )(slabs)
    return tables.reshape(*lead, 3, NA_Q, NA_K)


def _neighbourhood_attention(na, bias, layer, n_batch):
    n = na.shape[0]
    steps = 1 + NA_STEPS
    tiles = TOK // NA_Q

    def table(b, h, j):
        return (layer, h, jnp.where(j <= 1, 0, jnp.where(j == NA_STEPS, 2, 1)), 0, 0)

    return pl.pallas_call(
        _na_kernel,
        grid=(n_batch, HEAD_STEPS, steps),
        in_specs=[
            pl.BlockSpec((NA_Q, STEP_WIDTH), lambda b, h, j: (b * tiles + j, h)),
            pl.BlockSpec((TOK, STEP_WIDTH), lambda b, h, j: (b, HEAD_STEPS + h)),
            pl.BlockSpec((TOK, STEP_WIDTH), lambda b, h, j: (b, 2 * HEAD_STEPS + h)),
            pl.BlockSpec((1, HEADS_PER_STEP, 1, NA_Q, NA_K), table),
        ],
        out_specs=pl.BlockSpec((NA_Q, STEP_WIDTH), lambda b, h, j: (b * tiles + j, h)),
        out_shape=jax.ShapeDtypeStruct((n, WIDTH), F32),
        compiler_params=_params(("parallel", "parallel", "arbitrary")),
        name="neighbourhood_attention",
    )(na, na, na, bias)


def _merge_kernel(r_ref, m_ref, a_ref, bg_ref, x_ref, mod_ref, wb_ref, wo_ref, g2_ref, wr_ref,
                  x_out_ref, h2_ref, logit_ref):
    gate = jax.nn.sigmoid(bg_ref[...])
    mix = (gate[:, 0:D_MODEL] * _dot(r_ref[...].astype(BF16), wb_ref[0, 0])
           + gate[:, D_MODEL:2 * D_MODEL] * _dot(m_ref[...].astype(BF16), wb_ref[0, 1])
           + gate[:, 2 * D_MODEL:] * _dot(a_ref[...].astype(BF16), wb_ref[0, 2]))
    y = _dot(mix.astype(BF16), wo_ref[0])
    x_new = x_ref[...] + mod_ref[0, 2:3, :] * y
    x_out_ref[...] = x_new
    h2 = _rms_modulate(x_new, g2_ref[0], mod_ref[0, 3:4, :], mod_ref[0, 4:5, :])
    _store_token_tiles(h2_ref, h2, ROW_TILE)
    hi = h2.astype(BF16)
    lo = (h2 - hi.astype(F32)).astype(BF16)
    hi_terms = _dot(hi, wr_ref[0])
    logit_ref[...] = hi_terms[:, :LANES] + (_dot(lo, wr_ref[0, :, :LANES]) + hi_terms[:, LANES:])


def _merge(r, m, a, bg, x_all, mod, w_branch, w_out, norm2_g, w_route, layer, n_batch):
    n = x_all.shape[0]
    row = lambda i: (i, 0)

    def layer_block(w):
        return pl.BlockSpec((1,) + w.shape[1:], lambda i: (layer,) + (0,) * (w.ndim - 1))

    return pl.pallas_call(
        _merge_kernel,
        grid=(n // TM,),
        in_specs=[
            pl.BlockSpec((TM, WIDTH), row), pl.BlockSpec((TM, WIDTH), row), pl.BlockSpec((TM, WIDTH), row),
            pl.BlockSpec((TM, 3 * D_MODEL), row),
            pl.BlockSpec((TM, D_MODEL), row),
            pl.BlockSpec((1, 6, D_MODEL), _mod_row(n_batch)),
            layer_block(w_branch), layer_block(w_out),
            pl.BlockSpec((1, 1, D_MODEL), lambda i: (layer, 0, 0)),
            layer_block(w_route),
        ],
        out_specs=[pl.BlockSpec((TM, D_MODEL), row), pl.BlockSpec((TM * ROW_TILE, LANES), row),
                   pl.BlockSpec((TM, LANES), row)],
        out_shape=[jax.ShapeDtypeStruct((n, D_MODEL), F32), jax.ShapeDtypeStruct((n * ROW_TILE, LANES), F32),
                   jax.ShapeDtypeStruct((n, LANES), F32)],
        compiler_params=_params(("parallel",), VMEM_LIMIT),
        name="merge",
    )(r, m, a, bg, x_all, mod, w_branch, w_out, norm2_g.reshape(-1, 1, D_MODEL), w_route)


ROUTE_E1, ROUTE_E2, ROUTE_RANK1, ROUTE_RANK2, ROUTE_W1, ROUTE_W2 = range(6)


def _route_kernel(logit_ref, route_ref, count_ref, cnt_ref):
    @pl.when(pl.program_id(0) == 0)
    def _():
        cnt_ref[...] = jnp.zeros_like(cnt_ref)

    lg = logit_ref[...]
    lane = lax.broadcasted_iota(jnp.int32, lg.shape, 1)
    lane_f = lane.astype(F32)

    def first_argmax(vals):
        top = jnp.max(vals, axis=-1, keepdims=True)
        idx = jnp.min(jnp.where(vals == top, lane_f, float(LANES)), axis=-1, keepdims=True)
        return top, idx

    group_logits = jnp.where(lane < N_GROUPS, lg, -jnp.inf)
    g_top, g_idx = first_argmax(group_logits)
    group_w = 1.0 / jnp.sum(jnp.exp(group_logits - g_top), axis=-1, keepdims=True)

    first = N_GROUPS + EXPERTS_PER_GROUP * g_idx
    in_group = (lane_f >= first) & (lane_f < first + EXPERTS_PER_GROUP)
    expert_logits = jnp.where(in_group, lg, -jnp.inf)
    v1, i1 = first_argmax(expert_logits)
    v2, i2 = first_argmax(jnp.where(lane_f == i1, -jnp.inf, expert_logits))
    t = jnp.exp(v2 - v1)
    w1 = group_w / (1.0 + t)
    w2 = group_w * t / (1.0 + t)

    oh1 = (lane_f == i1).astype(F32)
    oh2 = (lane_f == i2).astype(F32)
    both = oh1 + oh2
    rows = lg.shape[0]
    earlier = (lax.broadcasted_iota(jnp.int32, (rows, rows), 1)
               < lax.broadcasted_iota(jnp.int32, (rows, rows), 0)).astype(BF16)
    before = _dot(earlier, both.astype(BF16)) + cnt_ref[...]
    rank1 = jnp.sum(oh1 * before, axis=-1, keepdims=True)
    rank2 = jnp.sum(oh2 * before, axis=-1, keepdims=True)
    cnt_ref[...] += jnp.sum(both, axis=0, keepdims=True)
    count_ref[...] = jnp.broadcast_to(cnt_ref[...], count_ref.shape)

    out = jnp.zeros_like(lg)
    for slot, val in ((ROUTE_E1, i1 - N_GROUPS), (ROUTE_E2, i2 - N_GROUPS), (ROUTE_RANK1, rank1),
                      (ROUTE_RANK2, rank2), (ROUTE_W1, w1), (ROUTE_W2, w2)):
        out = jnp.where(lane == slot, val, out)
    route_ref[...] = out


def _route(logits):
    n = logits.shape[0]
    return pl.pallas_call(
        _route_kernel,
        grid=(n // ROUTE_TM,),
        in_specs=[pl.BlockSpec((ROUTE_TM, LANES), lambda i: (i, 0))],
        out_specs=[pl.BlockSpec((ROUTE_TM, LANES), lambda i: (i, 0)), pl.BlockSpec((8, LANES), lambda i: (0, 0))],
        out_shape=[jax.ShapeDtypeStruct((n, LANES), F32), jax.ShapeDtypeStruct((8, LANES), F32)],
        scratch_shapes=[pltpu.VMEM((1, LANES), F32)],
        compiler_params=_params(("arbitrary",)),
        name="route",
    )(logits)


GROUP = 8


INVERT_TILE = 1024


def _invert_kernel(valid_ref, dest_ref, assign_ref):
    i = pl.program_id(0)

    @pl.when(i == 0)
    def _():
        def mark_block(b, carry):
            def mark(g, inner):
                for j in range(GROUP):
                    assign_ref[b * EXPERT_BLOCK + g * GROUP + j] = -1
                return inner
            lax.fori_loop(valid_ref[b] // GROUP, EXPERT_BLOCK // GROUP, mark, 0)
            return carry
        lax.fori_loop(0, valid_ref.shape[0], mark_block, 0)

    base = i * INVERT_TILE

    def scatter(g, carry):
        for j in range(GROUP):
            k = g * GROUP + j
            assign_ref[dest_ref[0, 0, k]] = base + k
        return carry

    lax.fori_loop(0, INVERT_TILE // GROUP, scatter, 0)


def _invert(dest, valid, n_rows):
    n_assign = dest.shape[0]
    grid_spec = pltpu.PrefetchScalarGridSpec(
        num_scalar_prefetch=1,
        grid=(n_assign // INVERT_TILE,),
        in_specs=[pl.BlockSpec((1, 1, INVERT_TILE), lambda i, valid: (i, 0, 0), memory_space=pltpu.SMEM)],
        out_specs=pl.BlockSpec(memory_space=pltpu.SMEM),
    )
    return pl.pallas_call(
        _invert_kernel,
        grid_spec=grid_spec,
        out_shape=jax.ShapeDtypeStruct((n_rows,), jnp.int32),
        compiler_params=_params(("arbitrary",)),
        name="invert_assignment",
    )(valid, dest.reshape(n_assign // INVERT_TILE, 1, INVERT_TILE))


def _dispatch_plan(route, counts):
    n = route.shape[0]
    counts = counts[0, N_GROUPS:N_GROUPS + N_EXPERTS].astype(jnp.int32)
    padded = (counts + EXPERT_BLOCK - 1) // EXPERT_BLOCK * EXPERT_BLOCK
    pad_end = jnp.cumsum(padded)
    pad_start = pad_end - padded
    e = route[:, ROUTE_E1:ROUTE_E2 + 1].astype(jnp.int32)
    rank = route[:, ROUTE_RANK1:ROUTE_RANK2 + 1].astype(jnp.int32)
    start_of = jnp.sum(jnp.where(e[..., None] == jnp.arange(N_EXPERTS), pad_start, 0), axis=-1)
    dest = (start_of + rank).reshape(-1)
    n_blocks = (2 * n + N_EXPERTS * (EXPERT_BLOCK - 1) + EXPERT_BLOCK - 1) // EXPERT_BLOCK
    n_rows = n_blocks * EXPERT_BLOCK
    block = jnp.arange(n_blocks, dtype=jnp.int32)
    block_expert = jnp.minimum(jnp.sum(block[:, None] * EXPERT_BLOCK >= pad_end[None, :], axis=1), N_EXPERTS - 1)
    valid = jnp.clip(jnp.sum(jnp.where(block_expert[:, None] == jnp.arange(N_EXPERTS), pad_start + counts, 0), axis=1)
                     - block * EXPERT_BLOCK, 0, EXPERT_BLOCK)
    copies = (valid + GROUP - 1) // GROUP * GROUP
    assign = _invert(dest, valid.astype(jnp.int32), n_rows)
    used_blocks = (pad_end[-1] // EXPERT_BLOCK).reshape(1)
    src_token = jnp.where(assign < 0, 0, assign // 2).reshape(n_blocks, 1, EXPERT_BLOCK)
    dest_tiles = dest.reshape(n // TM, TM, 2).transpose(0, 2, 1).reshape(n // TM, 1, 2 * TM)
    return (dest_tiles, src_token, block_expert.astype(jnp.int32), copies.astype(jnp.int32),
            used_blocks.astype(jnp.int32))


def _tile_rows(i):
    return pl.ds(pl.multiple_of(i * ROW_TILE, ROW_TILE), ROW_TILE)


def _expert_kernel(be_ref, copies_ref, used_ref, src_ref, src_next_ref, h_ref, wg_ref, wu_ref, wd_ref,
                   y_ref, x_buf, wg_s, wu_s, wd_s, gather_sem):
    i = pl.program_id(0)
    used = used_ref[0]
    cur = i % 2

    def gather_copy(idx_ref, r, buf):
        return pltpu.make_async_copy(h_ref.at[_tile_rows(idx_ref[0, 0, r])], x_buf.at[buf, _tile_rows(r)],
                                     gather_sem.at[buf])

    def for_each_group(block, fn):
        def body(g, carry):
            for j in range(GROUP):
                fn(g * GROUP + j)
            return carry
        lax.fori_loop(0, copies_ref[block] // GROUP, body, 0)

    @pl.when(i == 0)
    def _():
        x_buf[...] = jnp.zeros_like(x_buf)
        for_each_group(0, lambda r: gather_copy(src_ref, r, 0).start(priority=1))

    @pl.when(i + 1 < used)
    def _():
        for_each_group(i + 1, lambda r: gather_copy(src_next_ref, r, 1 - cur).start(priority=1))

    @pl.when(jnp.logical_or(i == 0, be_ref[i] != be_ref[jnp.maximum(i - 1, 0)]))
    def _():
        wg_s[...] = wg_ref[0, 0].astype(BF16)
        wu_s[...] = wu_ref[0, 0].astype(BF16)
        wd_s[...] = wd_ref[0, 0].astype(BF16)

    @pl.when(i < used)
    def _():
        for_each_group(i, lambda r: gather_copy(src_ref, 0, cur).wait())
        x = jnp.concatenate([_load_token_slab(x_buf.at[cur], EXPERT_BLOCK, ROW_TILE, c) for c in range(ROW_TILE)],
                            axis=1).astype(BF16)
        hidden = _silu(_dot(x, wg_s[...])) * _dot(x, wu_s[...])
        _store_token_tiles(y_ref, _dot(hidden.astype(BF16), wd_s[...]), ROW_TILE)

    @pl.when(i >= used)
    def _():
        y_ref[...] = jnp.zeros_like(y_ref)


def _experts(h2_tiles, plan, w_gate, w_up, w_down, layer):
    _, src_token, block_expert, copies, used_blocks = plan
    n_blocks = src_token.shape[0]
    last = n_blocks - 1

    def smem_block(index):
        return pl.BlockSpec((1, 1, EXPERT_BLOCK), index, memory_space=pltpu.SMEM)

    def weight(shape):
        return pl.BlockSpec((1, 1) + shape, lambda i, be, copies, used: (layer, be[i], 0, 0))

    grid_spec = pltpu.PrefetchScalarGridSpec(
        num_scalar_prefetch=3,
        grid=(n_blocks,),
        in_specs=[
            smem_block(lambda i, be, copies, used: (i, 0, 0)),
            smem_block(lambda i, be, copies, used: (jnp.minimum(i + 1, last), 0, 0)),
            pl.BlockSpec(memory_space=pl.ANY),
            weight((D_MODEL, EXPERT_FF)), weight((D_MODEL, EXPERT_FF)), weight((EXPERT_FF, D_MODEL)),
        ],
        out_specs=pl.BlockSpec((EXPERT_BLOCK * ROW_TILE, LANES), lambda i, be, copies, used: (i, 0)),
        scratch_shapes=[
            pltpu.VMEM((2, EXPERT_BLOCK * ROW_TILE, LANES), F32),
            pltpu.VMEM((D_MODEL, EXPERT_FF), BF16),
            pltpu.VMEM((D_MODEL, EXPERT_FF), BF16),
            pltpu.VMEM((EXPERT_FF, D_MODEL), BF16),
            pltpu.SemaphoreType.DMA((2,)),
        ],
    )
    return pl.pallas_call(
        _expert_kernel,
        grid_spec=grid_spec,
        out_shape=jax.ShapeDtypeStruct((n_blocks * EXPERT_BLOCK * ROW_TILE, LANES), F32),
        compiler_params=_params(("arbitrary",), VMEM_LIMIT),
        name="experts",
    )(block_expert, copies, used_blocks, src_token, src_token, h2_tiles, w_gate, w_up, w_down)


def _final_kernel(dest_ref, dest_next_ref, ys_ref, x_ref, route_ref, mod_ref, fg_ref, o_ref, y_buf, sem):
    rows_ref = _expert_rows(ys_ref, dest_ref, dest_next_ref, y_buf, sem, pl.program_id(0), pl.num_programs(0))
    sq = _moe_residual(rows_ref, x_ref, route_ref, mod_ref, o_ref)
    o_ref[...] = o_ref[...] * lax.rsqrt(sq / D_MODEL + EPS) * fg_ref[...]


def _final_combine(ys, dest_tiles, x_all, route, mod, final_g, n_batch):
    tiles = TOK // TM
    lat_tiles = SEQ // TM
    n_steps = n_batch * lat_tiles

    def tile(g):
        return (g // lat_tiles) * tiles + CTX_LEN // TM + g % lat_tiles

    tok = lambda g: (tile(g), 0)
    dest_block = lambda index: pl.BlockSpec((1, 1, 2 * TM), index, memory_space=pltpu.SMEM)
    return pl.pallas_call(
        _final_kernel,
        grid=(n_steps,),
        in_specs=[
            dest_block(lambda g: (tile(g), 0, 0)),
            dest_block(lambda g: (tile(jnp.minimum(g + 1, n_steps - 1)), 0, 0)),
            pl.BlockSpec(memory_space=pl.ANY),
            pl.BlockSpec((TM, D_MODEL), tok),
            pl.BlockSpec((TM, LANES), tok),
            pl.BlockSpec((1, 6, D_MODEL), lambda g: (g // lat_tiles, 0, 0)),
            pl.BlockSpec((1, D_MODEL), lambda g: (0, 0)),
        ],
        out_specs=pl.BlockSpec((TM, D_MODEL), lambda g: (g, 0)),
        out_shape=jax.ShapeDtypeStruct((n_batch * SEQ, D_MODEL), F32),
        scratch_shapes=[pltpu.VMEM((2, TM * 2 * ROW_TILE, LANES), F32), pltpu.SemaphoreType.DMA((2,))],
        compiler_params=_params(("arbitrary",)),
        name="final_combine",
    )(dest_tiles, dest_tiles, ys, x_all, route, mod, final_g.reshape(1, D_MODEL))


def _rope_tables():
    t = jnp.arange(SEQ)
    rows = (t // GRID_W).astype(F32)
    cols = (t % GRID_W).astype(F32)
    n_freq = HEAD_DIM // 4
    inv_freq = ROPE_BASE ** (-jnp.arange(n_freq, dtype=F32) / n_freq)
    ang_r = rows[:, None] * inv_freq
    ang_c = cols[:, None] * inv_freq
    cos = jnp.concatenate([jnp.cos(ang_r)] * 2 + [jnp.cos(ang_c)] * 2, axis=1)
    sin = jnp.concatenate([-jnp.sin(ang_r), jnp.sin(ang_r), -jnp.sin(ang_c), jnp.sin(ang_c)], axis=1)
    return cos, sin


def kernel(x, c, ctx, c_ctx, w_mod, b_mod, norm1_g, norm2_g, w_in, ret_decay, ret_norm_g, conv_w, conv_b,
           mlstm_gate_b, mlstm_norm_g, na_rpb, w_branch, w_out, w_group, w_router, w_expert_gate,
           w_expert_up, w_expert_down, final_norm_g):
    n_batch, seq, d = x.shape
    depth = w_mod.shape[0]
    assert (seq, d, ctx.shape[1]) == (SEQ, D_MODEL, CTX_LEN)
    n = n_batch * TOK

    cond_rows = -(-(n_batch + 1) // 8) * 8
    cond = jnp.zeros((cond_rows, d), F32).at[:n_batch].set(c).at[n_batch].set(c_ctx)
    mod_all = _modulation(cond, w_mod, b_mod).reshape(depth, cond_rows, 6, d)

    cos, sin = _rope_tables()
    x_all = jnp.concatenate([ctx, x], axis=1).reshape(n, d)

    w_packed = _pack_in_weights(w_in)
    w_branch_b = w_branch.astype(BF16)
    w_out_b = w_out.astype(BF16)
    w_route = jnp.concatenate([w_group, w_router], axis=-1)
    w_route = jnp.pad(w_route, ((0, 0), (0, 0), (0, LANES - w_route.shape[-1])))
    wr_hi = w_route.astype(BF16)
    wr_lo = (w_route - wr_hi.astype(F32)).astype(BF16)
    w_route_split = jnp.concatenate([wr_hi, wr_lo], axis=-1)
    gate_b = jnp.pad(mlstm_gate_b.reshape(depth, 1, 4 * N_HEADS), ((0, 0), (0, 0), (0, LANES - 4 * N_HEADS)))
    na_bias = _na_bias_tables(na_rpb)

    moe = None
    for layer in range(depth):
        mod = mod_all[layer]
        if moe is None:
            ret, ml, gates, na, bg = _in_projection(x_all, mod, norm1_g, w_packed, layer, n_batch)
        else:
            x_all, ret, ml, gates, na, bg = _in_projection(x_all, mod, norm1_g, w_packed, layer, n_batch, moe)
        r_out = _retention(ret, ret_decay[layer], ret_norm_g[layer], cos, sin, n_batch)
        m_out = _mlstm(ml, gates, gate_b[layer], conv_w[layer], conv_b[layer].reshape(1, 2 * WIDTH),
                       mlstm_norm_g[layer], n_batch)
        a_out = _neighbourhood_attention(na, na_bias, layer, n_batch)
        x_all, h2, logits = _merge(r_out, m_out, a_out, bg, x_all, mod, w_branch_b, w_out_b, norm2_g,
                                   w_route_split, layer, n_batch)
        route, counts = _route(logits)
        plan = _dispatch_plan(route, counts)
        ys = _experts(h2, plan, w_expert_gate, w_expert_up, w_expert_down, layer)
        moe = (ys, plan[0], route, mod)

    return _final_combine(ys, plan[0], x_all, route, mod, final_norm_g, n_batch).reshape(n_batch, SEQ, d)
```

```python
import functools

import numpy as np
import jax
import jax.numpy as jnp
from jax import lax
from jax.experimental import pallas as pl
from jax.experimental.pallas import tpu as pltpu

F32 = jnp.float32
BF16 = jnp.bfloat16

D_MODEL = 1024
SEQ = 2048
CTX_LEN = 256
TOK = CTX_LEN + SEQ
GRID_W = 64
GRID_ROWS = SEQ // GRID_W
HEAD_DIM = 128
N_HEADS = 4
WIDTH = N_HEADS * HEAD_DIM
CHUNK = 256
N_CHUNKS = TOK // CHUNK
CTX_CHUNKS = CTX_LEN // CHUNK
CONV_WIDTH = 5
NA_WIN_ROWS = 8
NA_WIN_COLS = 16
NA_Q_ROWS = 4
NA_Q = NA_Q_ROWS * GRID_W
NA_K_ROWS = NA_Q_ROWS + NA_WIN_ROWS
NA_K = NA_K_ROWS * GRID_W
NA_STEPS = SEQ // NA_Q
ROPE_BASE = 10000.0
N_GROUPS = 4
EXPERTS_PER_GROUP = 8
N_EXPERTS = N_GROUPS * EXPERTS_PER_GROUP
EXPERT_FF = 512
EXPERT_BLOCK = 256
EPS = 1e-6
NEG_INF = -1e30
QK_SCALE = HEAD_DIM ** -0.5

TM = 256
ROUTE_TM = 512
LANES = 128
VMEM_LIMIT = 56 * 1024 * 1024


def _dot(a, b):
    return jnp.dot(a, b, preferred_element_type=F32)


def _dot_nt(a, b):
    return lax.dot_general(a, b, (((1,), (1,)), ((), ())), preferred_element_type=F32)


def _bdot(a, b):
    return _dot(a.astype(BF16), b.astype(BF16))


def _bdot_nt(a, b):
    return _dot_nt(a.astype(BF16), b.astype(BF16))


def _bdot_tn(a, b):
    return _dot(a.T.astype(BF16), b.astype(BF16))


def _log_sigmoid(x):
    return jnp.minimum(x, 0.0) - jnp.log1p(jnp.exp(-jnp.abs(x)))


def _silu(x):
    return x * jax.nn.sigmoid(x)


def _params(sem, vmem=None):
    return pltpu.CompilerParams(dimension_semantics=sem, vmem_limit_bytes=vmem)


def _mod_row(n_batch):
    tiles = TOK // TM

    def index(i):
        return (jnp.where(i % tiles == 0, n_batch, i // tiles), 0, 0)

    return index


def _mod_kernel(c_ref, w_ref, b_ref, o_ref):
    cond = _silu(c_ref[...])
    o_ref[0] = _bdot(cond, w_ref[0]) + b_ref[0]


def _modulation(cond, w_mod, b_mod):
    depth, d, cols = w_mod.shape
    rows = cond.shape[0]
    tn = 1536
    return pl.pallas_call(
        _mod_kernel,
        grid=(depth, cols // tn),
        in_specs=[
            pl.BlockSpec((rows, d), lambda l, j: (0, 0)),
            pl.BlockSpec((1, d, tn), lambda l, j: (l, 0, j)),
            pl.BlockSpec((1, 1, tn), lambda l, j: (l, 0, j)),
        ],
        out_specs=pl.BlockSpec((1, rows, tn), lambda l, j: (l, 0, j)),
        out_shape=jax.ShapeDtypeStruct((depth, rows, cols), F32),
        compiler_params=_params(("parallel", "parallel")),
        name="modulation",
    )(cond, w_mod, b_mod.reshape(depth, 1, cols))


def _rms_modulate(x, g, shift, scale):
    y = x * lax.rsqrt(jnp.mean(x * x, axis=-1, keepdims=True) + EPS) * g
    return y * (1.0 + scale) + shift


IN_WIDTHS = (4 * WIDTH, 4 * WIDTH, LANES, 3 * WIDTH, 3 * D_MODEL)
IN_OFFSETS = tuple(int(v) for v in np.cumsum((0,) + IN_WIDTHS))


def _pack_kernel(w_ref, o_ref):
    gate0 = IN_OFFSETS[2]
    n_gate = 4 * N_HEADS
    o_ref[0, :, :gate0] = w_ref[0, :, :gate0].astype(BF16)
    window = w_ref[0, :, gate0:gate0 + LANES]
    lane = lax.broadcasted_iota(jnp.int32, window.shape, 1)
    o_ref[0, :, gate0:gate0 + LANES] = jnp.where(lane < n_gate, window, 0.0).astype(BF16)
    o_ref[0, :, gate0 + LANES:] = w_ref[0, :, gate0 + n_gate:].astype(BF16)


def _pack_in_weights(w_in):
    depth, d, cols = w_in.shape
    rows = 256
    return pl.pallas_call(
        _pack_kernel,
        grid=(depth, d // rows),
        in_specs=[pl.BlockSpec((1, rows, cols), lambda l, r: (l, r, 0))],
        out_specs=pl.BlockSpec((1, rows, IN_OFFSETS[-1]), lambda l, r: (l, r, 0)),
        out_shape=jax.ShapeDtypeStruct((depth, d, IN_OFFSETS[-1]), BF16),
        compiler_params=_params(("parallel", "parallel"), VMEM_LIMIT),
        name="pack_in_weights",
    )(w_in)


def _project(x, mod_ref, g_ref, w_ref, out_refs):
    h = _rms_modulate(x, g_ref[0], mod_ref[0, 0:1, :], mod_ref[0, 1:2, :]).astype(BF16)
    for o_ref, lo, hi in zip(out_refs, IN_OFFSETS[:-1], IN_OFFSETS[1:]):
        o_ref[...] = _dot(h, w_ref[0, :, lo:hi])


def _inproj_kernel(x_ref, mod_ref, g_ref, w_ref, *out_refs):
    _project(x_ref[...], mod_ref, g_ref, w_ref, out_refs)


def _combine_inproj_kernel(slots_ref, x_ref, route_ref, prev_mod_ref, mod_ref, g_ref, w_ref, x_out_ref, *out_refs):
    _moe_residual(slots_ref, x_ref, route_ref, prev_mod_ref, x_out_ref)
    _project(x_out_ref[...], mod_ref, g_ref, w_ref, out_refs)


def _in_projection(x_all, mod, norm_g, w_packed, layer, n_batch, moe=None):
    n = x_all.shape[0]
    row = lambda i: (i, 0)
    in_specs = [
        pl.BlockSpec((TM, D_MODEL), row),
        pl.BlockSpec((1, 6, D_MODEL), _mod_row(n_batch)),
        pl.BlockSpec((1, 1, D_MODEL), lambda i: (layer, 0, 0)),
        pl.BlockSpec((1,) + w_packed.shape[1:], lambda i: (layer, 0, 0), pipeline_mode=pl.Buffered(1)),
    ]
    out_specs = [pl.BlockSpec((TM, w), row) for w in IN_WIDTHS]
    out_shape = [jax.ShapeDtypeStruct((n, w), F32) for w in IN_WIDTHS]
    args = (x_all, mod, norm_g.reshape(-1, 1, D_MODEL), w_packed)
    if moe is None:
        body = _inproj_kernel
    else:
        slots, route, prev_mod = moe
        body = _combine_inproj_kernel
        in_specs = ([pl.BlockSpec((TM * 2 * ROW_TILE, LANES), row), in_specs[0], pl.BlockSpec((TM, LANES), row),
                     pl.BlockSpec((1, 6, D_MODEL), _mod_row(n_batch))] + in_specs[1:])
        out_specs = [pl.BlockSpec((TM, D_MODEL), row)] + out_specs
        out_shape = [jax.ShapeDtypeStruct((n, D_MODEL), F32)] + out_shape
        args = (slots, x_all, route, prev_mod) + args[1:]
    return pl.pallas_call(
        body,
        grid=(n // TM,),
        in_specs=in_specs,
        out_specs=out_specs,
        out_shape=out_shape,
        compiler_params=_params(("parallel",), VMEM_LIMIT),
        name="in_projection",
    )(*args)


def _chunk_order(t):
    fwd = t
    bwd = jnp.where(t < CTX_CHUNKS, CTX_CHUNKS - 1 - t, N_CHUNKS + CTX_CHUNKS - 1 - t)
    return fwd, bwd


def _chunk_slice(c):
    return pl.ds(pl.multiple_of(c * CHUNK, CHUNK), CHUNK)


def _head_norm(y, gain):
    mu = jnp.mean(y, axis=-1, keepdims=True)
    yc = y - mu
    var = jnp.mean(yc * yc, axis=-1, keepdims=True)
    return yc * lax.rsqrt(var + EPS) * gain


ROW_TILE = D_MODEL // LANES


def _store_token_tiles(ref, x, pitch, offset=0):
    rows = x.shape[0]
    for c in range(ROW_TILE):
        ref[pl.ds(offset + c, rows, stride=pitch), :] = x[:, c * LANES:(c + 1) * LANES]


def _load_token_slab(ref, rows, pitch, c, offset=0):
    return ref[pl.ds(offset + c, rows, stride=pitch), :]


def _moe_residual(slots_ref, x_ref, route_ref, mod_ref, o_ref):
    route = route_ref[...]
    w1 = route[:, ROUTE_W1:ROUTE_W1 + 1]
    w2 = route[:, ROUTE_W2:ROUTE_W2 + 1]
    rows = x_ref.shape[0]
    sq = jnp.zeros((rows, 1), F32)
    for c in range(ROW_TILE):
        lanes = slice(c * LANES, (c + 1) * LANES)
        y1 = _load_token_slab(slots_ref, rows, 2 * ROW_TILE, c)
        y2 = _load_token_slab(slots_ref, rows, 2 * ROW_TILE, c, offset=ROW_TILE)
        x_new = x_ref[:, lanes] + mod_ref[0, 5:6, lanes] * (y1 * w1 + y2 * w2)
        o_ref[:, lanes] = x_new
        sq = sq + jnp.sum(x_new * x_new, axis=-1, keepdims=True)
    return sq


def _select_lane(x, lane, idx):
    return jnp.sum(jnp.where(lane == idx, x, 0.0), axis=-1, keepdims=True)


HEADS_PER_STEP = 2
HEAD_STEPS = N_HEADS // HEADS_PER_STEP
STEP_WIDTH = HEADS_PER_STEP * HEAD_DIM
SCAN_VMEM = 48 * 1024 * 1024


def _head_lanes(hh):
    return slice(hh * HEAD_DIM, (hh + 1) * HEAD_DIM)


def _retention_kernel(dec_ref, q_ref, k_ref, v_ref, g_ref, cos_ref, sin_ref, gn_ref, o_ref,
                      qs_ref, ks_ref, ob_ref, intra_ref, st_ref):
    first_head = pl.program_id(1) * HEADS_PER_STEP

    lane = lax.broadcasted_iota(jnp.int32, (CHUNK, HEAD_DIM), 1)
    first_half = (lane % (HEAD_DIM // 2)) < (HEAD_DIM // 4)

    def rope(x, cos, sin):
        rot = jnp.where(first_half, pltpu.roll(x, HEAD_DIM - HEAD_DIM // 4, 1), pltpu.roll(x, HEAD_DIM // 4, 1))
        return x * cos + rot * sin

    qs_ref[0:CTX_LEN, :] = q_ref[0:CTX_LEN, :]
    ks_ref[0:CTX_LEN, :] = k_ref[0:CTX_LEN, :] * QK_SCALE

    def rotate_chunk(c, carry):
        rows = _chunk_slice(c)
        pos = _chunk_slice(c - CTX_CHUNKS)
        cos = cos_ref[pos, :]
        sin = sin_ref[pos, :]
        for hh in range(HEADS_PER_STEP):
            qs_ref[rows, _head_lanes(hh)] = rope(q_ref[rows, _head_lanes(hh)], cos, sin)
            ks_ref[rows, _head_lanes(hh)] = rope(k_ref[rows, _head_lanes(hh)], cos, sin) * QK_SCALE
        return carry

    lax.fori_loop(CTX_CHUNKS, N_CHUNKS, rotate_chunk, 0)

    ii = lax.broadcasted_iota(jnp.int32, (CHUNK, CHUNK), 0).astype(F32)
    jj = lax.broadcasted_iota(jnp.int32, (CHUNK, CHUNK), 1).astype(F32)
    col = lax.broadcasted_iota(jnp.int32, (CHUNK, 1), 0).astype(F32)

    def decay_mat(dist, lg_dir):
        ok = dist >= 0
        return jnp.where(ok, jnp.exp(jnp.where(ok, dist, 0.0) * lg_dir), 0.0)

    dec = _log_sigmoid(dec_ref[...])
    hl = lax.broadcasted_iota(jnp.int32, dec.shape, 1)
    consts = []
    for hh in range(HEADS_PER_STEP):
        lg = jnp.sum(jnp.where(hl == first_head + hh, dec, 0.0), axis=-1, keepdims=True)
        lg_f, lg_b = lg[0:1, :], lg[1:2, :]
        intra_ref[2 * hh] = decay_mat(ii - jj, lg_f)
        intra_ref[2 * hh + 1] = decay_mat(jj - ii, lg_b)
        consts.append(dict(
            q_decay=(jnp.exp((col + 1.0) * lg_f), jnp.exp((CHUNK - col) * lg_b)),
            k_decay=(jnp.exp((CHUNK - 1.0 - col) * lg_f), jnp.exp(col * lg_b)),
            chunk_decay=(jnp.exp(CHUNK * lg_f), jnp.exp(CHUNK * lg_b))))

    st_ref[...] = jnp.zeros_like(st_ref)
    out_refs = (o_ref, ob_ref)

    def step(t, carry):
        for hh in range(HEADS_PER_STEP):
            cst = consts[hh]
            for d, c in enumerate(_chunk_order(t)):
                rows = _chunk_slice(c)
                q = qs_ref[rows, _head_lanes(hh)]
                k = ks_ref[rows, _head_lanes(hh)]
                v = v_ref[rows, _head_lanes(hh)]
                s_prev = st_ref[2 * hh + d]
                scores = _bdot_nt(q, k) * intra_ref[2 * hh + d]
                out_refs[d][rows, _head_lanes(hh)] = _bdot(scores, v) + _bdot(q * cst["q_decay"][d], s_prev)
                st_ref[2 * hh + d] = s_prev * cst["chunk_decay"][d] + _bdot_tn(k * cst["k_decay"][d], v)
        return carry

    lax.fori_loop(0, N_CHUNKS, step, 0)

    def finish_chunk(c, carry):
        rows = _chunk_slice(c)
        for hh in range(HEADS_PER_STEP):
            lanes = _head_lanes(hh)
            y = o_ref[rows, lanes] + ob_ref[rows, lanes]
            o_ref[rows, lanes] = _head_norm(y, gn_ref[:, lanes]) * _silu(g_ref[rows, lanes])
        return carry

    lax.fori_loop(0, N_CHUNKS, finish_chunk, 0)


def _retention(ret, ret_decay, norm_g, cos, sin, n_batch):
    n = ret.shape[0]

    def head_block(offset):
        return pl.BlockSpec((TOK, STEP_WIDTH), lambda b, h: (b, offset * HEAD_STEPS + h))

    return pl.pallas_call(
        _retention_kernel,
        grid=(n_batch, HEAD_STEPS),
        in_specs=[
            pl.BlockSpec(ret_decay.shape, lambda b, h: (0, 0)),
            head_block(0), head_block(1), head_block(2), head_block(3),
            pl.BlockSpec((SEQ, HEAD_DIM), lambda b, h: (0, 0)),
            pl.BlockSpec((SEQ, HEAD_DIM), lambda b, h: (0, 0)),
            pl.BlockSpec((1, STEP_WIDTH), lambda b, h: (0, h)),
        ],
        out_specs=pl.BlockSpec((TOK, STEP_WIDTH), lambda b, h: (b, h)),
        out_shape=jax.ShapeDtypeStruct((n, WIDTH), F32),
        scratch_shapes=[
            pltpu.VMEM((TOK, STEP_WIDTH), F32),
            pltpu.VMEM((TOK, STEP_WIDTH), F32),
            pltpu.VMEM((TOK, STEP_WIDTH), F32),
            pltpu.VMEM((2 * HEADS_PER_STEP, CHUNK, CHUNK), F32),
            pltpu.VMEM((2 * HEADS_PER_STEP, HEAD_DIM, HEAD_DIM), F32),
        ],
        compiler_params=_params(("parallel", "parallel"), SCAN_VMEM),
        name="retention",
    )(ret_decay, ret, ret, ret, ret, cos, sin, norm_g.reshape(1, WIDTH))


CONV_PAD = 8
CONV_ROWS = 128


def _mlstm_kernel(q_ref, k_ref, v_ref, og_ref, gate_ref, gb_ref, cwq_ref, cwk_ref, cbq_ref, cbk_ref,
                  gn_ref, o_ref, qs_ref, ks_ref, ob_ref, pad_ref, gx_ref, gxt_ref, cst_ref):
    first_head = pl.program_id(1) * HEADS_PER_STEP

    pad_ref[0:CONV_PAD, :] = jnp.zeros((CONV_PAD, STEP_WIDTH), F32)
    pad_ref[CONV_PAD + TOK:, :] = jnp.zeros((CONV_PAD, STEP_WIDTH), F32)
    crow = lax.broadcasted_iota(jnp.int32, (CONV_ROWS, 1), 0)

    def conv(u_ref, w_ref, b_ref, dst_ref, scale):
        def fill(c, carry):
            rows = _chunk_slice(c)
            pad_ref[pl.ds(pl.multiple_of(c * CHUNK + CONV_PAD, CONV_PAD), CHUNK), :] = u_ref[rows, :]
            return carry

        lax.fori_loop(0, N_CHUNKS, fill, 0)

        for blk in range(TOK // CONV_ROWS):
            first = blk * CONV_ROWS
            acc = jnp.zeros((CONV_ROWS, STEP_WIDTH), F32)
            for j in range(CONV_WIDTH):
                shift = j - CONV_WIDTH // 2
                tap = pad_ref[first + CONV_PAD + shift:first + CONV_PAD + shift + CONV_ROWS, :]
                if (first + CONV_ROWS == CTX_LEN and shift > 0) or (first == CTX_LEN and shift < 0):
                    trow = crow + first
                    tap = jnp.where((trow < CTX_LEN) == (trow + shift < CTX_LEN), tap, 0.0)
                acc = acc + tap * w_ref[j:j + 1, :]
            dst_ref[first:first + CONV_ROWS, :] = _silu(acc + b_ref[...]) * scale

    conv(q_ref, cwq_ref, cbq_ref, qs_ref, 1.0)
    conv(k_ref, cwk_ref, cbk_ref, ks_ref, QK_SCALE)

    clane = lax.broadcasted_iota(jnp.int32, (CHUNK, LANES), 1)
    is_forget = ((clane // N_HEADS) % 2) == 1

    def gate_chunk(c, carry):
        rows = _chunk_slice(c)
        g = gate_ref[rows, :] + gb_ref[...]
        gx = jnp.where(is_forget, _log_sigmoid(g), g)
        gx_ref[rows, :] = gx
        gxt_ref[c] = gx.T
        return carry

    lax.fori_loop(0, N_CHUNKS, gate_chunk, 0)

    ii = lax.broadcasted_iota(jnp.int32, (CHUNK, CHUNK), 0)
    jj = lax.broadcasted_iota(jnp.int32, (CHUNK, CHUNK), 1)
    causal = (jj <= ii, jj >= ii)

    cst_ref[...] = jnp.zeros_like(cst_ref)
    out_refs = (o_ref, ob_ref)

    def step(t, carry):
        new_carry = []
        for hh in range(HEADS_PER_STEP):
            head = first_head + hh
            lanes = _head_lanes(hh)
            for d, c in enumerate(_chunk_order(t)):
                slot = 2 * hh + d
                n_prev, m_prev = carry[2 * slot], carry[2 * slot + 1]
                rows = _chunk_slice(c)
                q = qs_ref[rows, lanes]
                k = ks_ref[rows, lanes]
                v = v_ref[rows, lanes]
                gc = gx_ref[rows, :]
                i_col = _select_lane(gc, clane, 2 * d * N_HEADS + head)
                f_col = _select_lane(gc, clane, (2 * d + 1) * N_HEADS + head)
                i_row = gxt_ref[c, pl.ds(2 * d * N_HEADS + head, 1), :]
                f_row = gxt_ref[c, pl.ds((2 * d + 1) * N_HEADS + head, 1), :]
                vis = causal[d]
                cum_col = jnp.sum(jnp.where(vis, f_row, 0.0), axis=1, keepdims=True)
                cum_row = jnp.sum(jnp.where(causal[1 - d], f_col, 0.0), axis=0, keepdims=True)
                total = jnp.sum(f_row, axis=1, keepdims=True)
                c_prev = cst_ref[slot]

                log_kw = total - cum_col + i_col
                m_new = jnp.maximum(total + m_prev, jnp.max(log_kw, axis=0, keepdims=True))
                kw = jnp.exp(log_kw - m_new)
                pw = jnp.exp(total + m_prev - m_new)
                cst_ref[slot] = pw * c_prev + _bdot_tn(k * kw, v)
                n_new = pw * n_prev + jnp.sum(kw * k, axis=0, keepdims=True)

                log_w = jnp.where(vis, cum_col - cum_row + i_row, -jnp.inf)
                log_p = cum_col + m_prev
                m_t = jnp.maximum(log_p, jnp.max(log_w, axis=1, keepdims=True))
                w = jnp.exp(log_w - m_t)
                p = jnp.exp(log_p - m_t)
                qk = _bdot_nt(q, k) * w
                num = _bdot(qk, v) + p * _bdot(q, c_prev)
                den = jnp.sum(qk, axis=1, keepdims=True) + p * jnp.sum(q * n_prev, axis=1, keepdims=True)
                out_refs[d][rows, lanes] = num / jnp.maximum(jnp.abs(den), jnp.exp(-m_t))
                new_carry += [n_new, m_new]
        return tuple(new_carry)

    zero_n = jnp.zeros((1, HEAD_DIM), F32)
    zero_m = jnp.zeros((1, 1), F32)
    lax.fori_loop(0, N_CHUNKS, step, (zero_n, zero_m) * (2 * HEADS_PER_STEP))

    def finish_chunk(c, carry):
        rows = _chunk_slice(c)
        for hh in range(HEADS_PER_STEP):
            lanes = _head_lanes(hh)
            gated = (o_ref[rows, lanes] + ob_ref[rows, lanes]) * jax.nn.sigmoid(og_ref[rows, lanes])
            o_ref[rows, lanes] = _head_norm(gated, gn_ref[:, lanes])
        return carry

    lax.fori_loop(0, N_CHUNKS, finish_chunk, 0)


def _mlstm(ml, gates, gate_b, conv_w, conv_b, norm_g, n_batch):
    n = ml.shape[0]

    def head_block(offset):
        return pl.BlockSpec((TOK, STEP_WIDTH), lambda b, h: (b, offset * HEAD_STEPS + h))

    def head_cols(rows, offset):
        return pl.BlockSpec((rows, STEP_WIDTH), lambda b, h: (0, offset * HEAD_STEPS + h))

    return pl.pallas_call(
        _mlstm_kernel,
        grid=(n_batch, HEAD_STEPS),
        in_specs=[
            head_block(0), head_block(1), head_block(2), head_block(3),
            pl.BlockSpec((TOK, LANES), lambda b, h: (b, 0)),
            pl.BlockSpec((1, LANES), lambda b, h: (0, 0)),
            head_cols(CONV_WIDTH, 0), head_cols(CONV_WIDTH, 1),
            head_cols(1, 0), head_cols(1, 1),
            head_cols(1, 0),
        ],
        out_specs=pl.BlockSpec((TOK, STEP_WIDTH), lambda b, h: (b, h)),
        out_shape=jax.ShapeDtypeStruct((n, WIDTH), F32),
        scratch_shapes=[
            pltpu.VMEM((TOK, STEP_WIDTH), F32),
            pltpu.VMEM((TOK, STEP_WIDTH), F32),
            pltpu.VMEM((TOK, STEP_WIDTH), F32),
            pltpu.VMEM((TOK + 2 * CONV_PAD, STEP_WIDTH), F32),
            pltpu.VMEM((TOK, LANES), F32),
            pltpu.VMEM((N_CHUNKS, LANES, CHUNK), F32),
            pltpu.VMEM((2 * HEADS_PER_STEP, HEAD_DIM, HEAD_DIM), F32),
        ],
        compiler_params=_params(("parallel", "parallel"), SCAN_VMEM),
        name="mlstm",
    )(ml, ml, ml, ml, gates, gate_b, conv_w, conv_w, conv_b, conv_b, norm_g.reshape(1, WIDTH))


def _na_kernel(q_ref, k_ref, v_ref, bias_ref, o_ref):
    step = pl.program_id(2)

    def context_scores(hh):
        lanes = _head_lanes(hh)
        q = q_ref[:, lanes].astype(BF16)
        v_ctx = v_ref[0:CTX_LEN, lanes].astype(BF16)
        return q, v_ctx, _dot_nt(q, k_ref[0:CTX_LEN, lanes].astype(BF16)) * QK_SCALE

    @pl.when(step == 0)
    def _():
        for hh in range(HEADS_PER_STEP):
            _, v_ctx, s_ctx = context_scores(hh)
            m = jnp.max(s_ctx, axis=-1, keepdims=True)
            p = jnp.exp(s_ctx - m)
            o_ref[:, _head_lanes(hh)] = _dot(p.astype(BF16), v_ctx) / jnp.sum(p, axis=-1, keepdims=True)

    @pl.when(step > 0)
    def _():
        start = CTX_LEN + NA_Q * jnp.clip(step - 2, 0, NA_STEPS - NA_K_ROWS // NA_Q_ROWS)
        rows = pl.ds(pl.multiple_of(start, NA_Q), NA_K)
        for hh in range(HEADS_PER_STEP):
            lanes = _head_lanes(hh)
            q, v_ctx, s_ctx = context_scores(hh)
            s_loc = _dot_nt(q, k_ref[rows, lanes].astype(BF16)) * QK_SCALE + bias_ref[0, hh, 0]
            m = jnp.maximum(jnp.max(s_loc, axis=-1, keepdims=True), jnp.max(s_ctx, axis=-1, keepdims=True))
            p_loc = jnp.exp(s_loc - m)
            p_ctx = jnp.exp(s_ctx - m)
            denom = jnp.sum(p_loc, axis=-1, keepdims=True) + jnp.sum(p_ctx, axis=-1, keepdims=True)
            o = _dot(p_loc.astype(BF16), v_ref[rows, lanes].astype(BF16)) + _dot(p_ctx.astype(BF16), v_ctx)
            o_ref[:, lanes] = o / denom


def _na_row_case(case, a):
    t = np.arange(NA_K_ROWS)
    last_start = NA_K_ROWS - NA_WIN_ROWS
    return [
        (t < NA_WIN_ROWS, NA_WIN_ROWS - 1 - a),
        ((t >= a) & (t < a + NA_WIN_ROWS), NA_WIN_ROWS // 2 - 1 - a),
        (t >= last_start, NA_Q_ROWS - NA_K_ROWS + NA_WIN_ROWS - 1 - a),
    ][case]


def _na_expand_kernel(slab_ref, o_ref):
    masked = jnp.full((GRID_W, GRID_W), NEG_INF, F32)
    for case in range(3):
        for a in range(NA_Q_ROWS):
            row_ok, first = _na_row_case(case, a)
            for pair in range(NA_K_ROWS // 2):
                tiles = [slab_ref[0, first + t] if row_ok[t] else masked for t in (2 * pair, 2 * pair + 1)]
                o_ref[0, case, a * GRID_W:(a + 1) * GRID_W, pair * LANES:(pair + 1) * LANES] = (
                    jnp.concatenate(tiles, axis=1))


def _na_bias_tables(rpb):
    lead = rpb.shape[:-2]
    n_row_off, n_col_off = rpb.shape[-2:]
    qc = np.arange(GRID_W)[:, None]
    kc = np.arange(GRID_W)[None, :]
    col_start = np.clip(qc - NA_WIN_COLS // 2, 0, GRID_W - NA_WIN_COLS)
    col_ok = (kc >= col_start) & (kc < col_start + NA_WIN_COLS)
    col_idx = np.clip(kc - qc + NA_WIN_COLS - 1, 0, n_col_off - 1)
    onehot = ((col_idx[None] == np.arange(n_col_off)[:, None, None]) & col_ok[None]).astype(np.float32)
    slabs = jnp.einsum('...rc,cqk->...rqk', rpb.astype(F32), onehot, precision=lax.Precision.HIGHEST)
    slabs = jnp.where(col_ok, slabs, NEG_INF).reshape(-1, n_row_off, GRID_W, GRID_W)
    n_tables = slabs.shape[0]
    tables = pl.pallas_call(
        _na_expand_kernel,
        grid=(n_tables,),
        in_specs=[pl.BlockSpec((1, n_row_off, GRID_W, GRID_W), lambda i: (i, 0, 0, 0))],
        out_specs=pl.BlockSpec((1, 3, NA_Q, NA_K), lambda i: (i, 0, 0, 0)),
        out_shape=jax.ShapeDtypeStruct((n_tables, 3, NA_Q, NA_K), F32),
        compiler_params=_params(("parallel",)),
        name="na_bias_tables",
    )(slabs)
    return tables.reshape(*lead, 3, NA_Q, NA_K)


def _neighbourhood_attention(na, bias, layer, n_batch):
    n = na.shape[0]
    steps = 1 + NA_STEPS
    tiles = TOK // NA_Q

    def table(b, h, j):
        return (layer, h, jnp.where(j <= 1, 0, jnp.where(j == NA_STEPS, 2, 1)), 0, 0)

    return pl.pallas_call(
        _na_kernel,
        grid=(n_batch, HEAD_STEPS, steps),
        in_specs=[
            pl.BlockSpec((NA_Q, STEP_WIDTH), lambda b, h, j: (b * tiles + j, h)),
            pl.BlockSpec((TOK, STEP_WIDTH), lambda b, h, j: (b, HEAD_STEPS + h)),
            pl.BlockSpec((TOK, STEP_WIDTH), lambda b, h, j: (b, 2 * HEAD_STEPS + h)),
            pl.BlockSpec((1, HEADS_PER_STEP, 1, NA_Q, NA_K), table),
        ],
        out_specs=pl.BlockSpec((NA_Q, STEP_WIDTH), lambda b, h, j: (b * tiles + j, h)),
        out_shape=jax.ShapeDtypeStruct((n, WIDTH), F32),
        compiler_params=_params(("parallel", "parallel", "arbitrary")),
        name="neighbourhood_attention",
    )(na, na, na, bias)


def _merge_kernel(r_ref, m_ref, a_ref, bg_ref, x_ref, mod_ref, wb_ref, wo_ref, g2_ref, wr_ref,
                  x_out_ref, h2_ref, logit_ref):
    gate = jax.nn.sigmoid(bg_ref[...])
    mix = (gate[:, 0:D_MODEL] * _dot(r_ref[...].astype(BF16), wb_ref[0, 0])
           + gate[:, D_MODEL:2 * D_MODEL] * _dot(m_ref[...].astype(BF16), wb_ref[0, 1])
           + gate[:, 2 * D_MODEL:] * _dot(a_ref[...].astype(BF16), wb_ref[0, 2]))
    y = _dot(mix.astype(BF16), wo_ref[0])
    x_new = x_ref[...] + mod_ref[0, 2:3, :] * y
    x_out_ref[...] = x_new
    h2 = _rms_modulate(x_new, g2_ref[0], mod_ref[0, 3:4, :], mod_ref[0, 4:5, :])
    _store_token_tiles(h2_ref, h2, ROW_TILE)
    hi = h2.astype(BF16)
    lo = (h2 - hi.astype(F32)).astype(BF16)
    hi_terms = _dot(hi, wr_ref[0])
    logit_ref[...] = hi_terms[:, :LANES] + (_dot(lo, wr_ref[0, :, :LANES]) + hi_terms[:, LANES:])


def _merge(r, m, a, bg, x_all, mod, w_branch, w_out, norm2_g, w_route, layer, n_batch):
    n = x_all.shape[0]
    row = lambda i: (i, 0)

    def layer_block(w):
        return pl.BlockSpec((1,) + w.shape[1:], lambda i: (layer,) + (0,) * (w.ndim - 1))

    return pl.pallas_call(
        _merge_kernel,
        grid=(n // TM,),
        in_specs=[
            pl.BlockSpec((TM, WIDTH), row), pl.BlockSpec((TM, WIDTH), row), pl.BlockSpec((TM, WIDTH), row),
            pl.BlockSpec((TM, 3 * D_MODEL), row),
            pl.BlockSpec((TM, D_MODEL), row),
            pl.BlockSpec((1, 6, D_MODEL), _mod_row(n_batch)),
            layer_block(w_branch), layer_block(w_out),
            pl.BlockSpec((1, 1, D_MODEL), lambda i: (layer, 0, 0)),
            layer_block(w_route),
        ],
        out_specs=[pl.BlockSpec((TM, D_MODEL), row), pl.BlockSpec((TM * ROW_TILE, LANES), row),
                   pl.BlockSpec((TM, LANES), row)],
        out_shape=[jax.ShapeDtypeStruct((n, D_MODEL), F32), jax.ShapeDtypeStruct((n * ROW_TILE, LANES), F32),
                   jax.ShapeDtypeStruct((n, LANES), F32)],
        compiler_params=_params(("parallel",), VMEM_LIMIT),
        name="merge",
    )(r, m, a, bg, x_all, mod, w_branch, w_out, norm2_g.reshape(-1, 1, D_MODEL), w_route)


ROUTE_E1, ROUTE_E2, ROUTE_RANK1, ROUTE_RANK2, ROUTE_W1, ROUTE_W2 = range(6)


def _route_kernel(logit_ref, route_ref, count_ref, cnt_ref):
    @pl.when(pl.program_id(0) == 0)
    def _():
        cnt_ref[...] = jnp.zeros_like(cnt_ref)

    lg = logit_ref[...]
    lane = lax.broadcasted_iota(jnp.int32, lg.shape, 1)
    lane_f = lane.astype(F32)

    def first_argmax(vals):
        top = jnp.max(vals, axis=-1, keepdims=True)
        idx = jnp.min(jnp.where(vals == top, lane_f, float(LANES)), axis=-1, keepdims=True)
        return top, idx

    group_logits = jnp.where(lane < N_GROUPS, lg, -jnp.inf)
    g_top, g_idx = first_argmax(group_logits)
    group_w = 1.0 / jnp.sum(jnp.exp(group_logits - g_top), axis=-1, keepdims=True)

    first = N_GROUPS + EXPERTS_PER_GROUP * g_idx
    in_group = (lane_f >= first) & (lane_f < first + EXPERTS_PER_GROUP)
    expert_logits = jnp.where(in_group, lg, -jnp.inf)
    v1, i1 = first_argmax(expert_logits)
    v2, i2 = first_argmax(jnp.where(lane_f == i1, -jnp.inf, expert_logits))
    t = jnp.exp(v2 - v1)
    w1 = group_w / (1.0 + t)
    w2 = group_w * t / (1.0 + t)

    oh1 = (lane_f == i1).astype(F32)
    oh2 = (lane_f == i2).astype(F32)
    both = oh1 + oh2
    rows = lg.shape[0]
    earlier = (lax.broadcasted_iota(jnp.int32, (rows, rows), 1)
               < lax.broadcasted_iota(jnp.int32, (rows, rows), 0)).astype(BF16)
    before = _dot(earlier, both.astype(BF16)) + cnt_ref[...]
    rank1 = jnp.sum(oh1 * before, axis=-1, keepdims=True)
    rank2 = jnp.sum(oh2 * before, axis=-1, keepdims=True)
    cnt_ref[...] += jnp.sum(both, axis=0, keepdims=True)
    count_ref[...] = jnp.broadcast_to(cnt_ref[...], count_ref.shape)

    out = jnp.zeros_like(lg)
    for slot, val in ((ROUTE_E1, i1 - N_GROUPS), (ROUTE_E2, i2 - N_GROUPS), (ROUTE_RANK1, rank1),
                      (ROUTE_RANK2, rank2), (ROUTE_W1, w1), (ROUTE_W2, w2)):
        out = jnp.where(lane == slot, val, out)
    route_ref[...] = out


def _route(logits):
    n = logits.shape[0]
    return pl.pallas_call(
        _route_kernel,
        grid=(n // ROUTE_TM,),
        in_specs=[pl.BlockSpec((ROUTE_TM, LANES), lambda i: (i, 0))],
        out_specs=[pl.BlockSpec((ROUTE_TM, LANES), lambda i: (i, 0)), pl.BlockSpec((8, LANES), lambda i: (0, 0))],
        out_shape=[jax.ShapeDtypeStruct((n, LANES), F32), jax.ShapeDtypeStruct((8, LANES), F32)],
        scratch_shapes=[pltpu.VMEM((1, LANES), F32)],
        compiler_params=_params(("arbitrary",)),
        name="route",
    )(logits)


GROUP = 8


INVERT_TILE = 1024


def _invert_kernel(valid_ref, dest_ref, assign_ref):
    i = pl.program_id(0)

    @pl.when(i == 0)
    def _():
        def mark_block(b, carry):
            def mark(g, inner):
                for j in range(GROUP):
                    assign_ref[b * EXPERT_BLOCK + g * GROUP + j] = -1
                return inner
            lax.fori_loop(valid_ref[b] // GROUP, EXPERT_BLOCK // GROUP, mark, 0)
            return carry
        lax.fori_loop(0, valid_ref.shape[0], mark_block, 0)

    base = i * INVERT_TILE

    def scatter(g, carry):
        for j in range(GROUP):
            k = g * GROUP + j
            assign_ref[dest_ref[0, 0, k]] = base + k
        return carry

    lax.fori_loop(0, INVERT_TILE // GROUP, scatter, 0)


def _invert(dest, valid, n_rows):
    n_assign = dest.shape[0]
    grid_spec = pltpu.PrefetchScalarGridSpec(
        num_scalar_prefetch=1,
        grid=(n_assign // INVERT_TILE,),
        in_specs=[pl.BlockSpec((1, 1, INVERT_TILE), lambda i, valid: (i, 0, 0), memory_space=pltpu.SMEM)],
        out_specs=pl.BlockSpec(memory_space=pltpu.SMEM),
    )
    return pl.pallas_call(
        _invert_kernel,
        grid_spec=grid_spec,
        out_shape=jax.ShapeDtypeStruct((n_rows,), jnp.int32),
        compiler_params=_params(("arbitrary",)),
        name="invert_assignment",
    )(valid, dest.reshape(n_assign // INVERT_TILE, 1, INVERT_TILE))


def _dispatch_plan(route, counts):
    n = route.shape[0]
    counts = counts[0, N_GROUPS:N_GROUPS + N_EXPERTS].astype(jnp.int32)
    padded = (counts + EXPERT_BLOCK - 1) // EXPERT_BLOCK * EXPERT_BLOCK
    pad_end = jnp.cumsum(padded)
    pad_start = pad_end - padded
    e = route[:, ROUTE_E1:ROUTE_E2 + 1].astype(jnp.int32)
    rank = route[:, ROUTE_RANK1:ROUTE_RANK2 + 1].astype(jnp.int32)
    start_of = jnp.sum(jnp.where(e[..., None] == jnp.arange(N_EXPERTS), pad_start, 0), axis=-1)
    dest = (start_of + rank).reshape(-1)
    n_blocks = (2 * n + N_EXPERTS * (EXPERT_BLOCK - 1) + EXPERT_BLOCK - 1) // EXPERT_BLOCK
    n_rows = n_blocks * EXPERT_BLOCK
    block = jnp.arange(n_blocks, dtype=jnp.int32)
    block_expert = jnp.minimum(jnp.sum(block[:, None] * EXPERT_BLOCK >= pad_end[None, :], axis=1), N_EXPERTS - 1)
    valid = jnp.clip(jnp.sum(jnp.where(block_expert[:, None] == jnp.arange(N_EXPERTS), pad_start + counts, 0), axis=1)
                     - block * EXPERT_BLOCK, 0, EXPERT_BLOCK)
    copies = (valid + GROUP - 1) // GROUP * GROUP
    assign = _invert(dest, valid.astype(jnp.int32), n_rows)
    used_blocks = (pad_end[-1] // EXPERT_BLOCK).reshape(1)
    row = jnp.arange(n_rows, dtype=jnp.int32)
    spare = 2 * n + ((row // EXPERT_BLOCK) % 2) * GROUP + row % GROUP
    src_token = jnp.where(assign < 0, 0, assign // 2).reshape(n_blocks, 1, EXPERT_BLOCK)
    dst_slot = jnp.where(assign < 0, spare, assign).reshape(n_blocks, 1, EXPERT_BLOCK)
    return (src_token, dst_slot, block_expert.astype(jnp.int32), copies.astype(jnp.int32),
            used_blocks.astype(jnp.int32))


GATHER_AHEAD = 2
GATHER_SLOTS = GATHER_AHEAD + 1


def _tile_rows(i):
    return pl.ds(pl.multiple_of(i * ROW_TILE, ROW_TILE), ROW_TILE)


def _expert_kernel(be_ref, copies_ref, used_ref, src_ref, src_next_ref, src_ahead_ref, dst_ref, dst_prev_ref, h_ref,
                   wg_ref, wu_ref, wd_ref, slots_ref, x_buf, y_buf, wg_s, wu_s, wd_s, gather_sem, scatter_sem):
    i = pl.program_id(0)
    used = used_ref[0]
    cur = i % 2
    slot = i % GATHER_SLOTS

    def gather_copy(idx_ref, r, buf):
        return pltpu.make_async_copy(h_ref.at[_tile_rows(idx_ref[0, 0, r])], x_buf.at[buf, _tile_rows(r)],
                                     gather_sem.at[buf])

    def scatter_copy(idx_ref, r, buf):
        return pltpu.make_async_copy(y_buf.at[buf, _tile_rows(r)], slots_ref.at[_tile_rows(idx_ref[0, 0, r])],
                                     scatter_sem.at[buf])

    def for_each_group(block, fn):
        def body(g, carry):
            for j in range(GROUP):
                fn(g * GROUP + j, j % 2)
            return carry
        lax.fori_loop(0, copies_ref[block] // GROUP, body, 0)

    @pl.when(i == 0)
    def _():
        x_buf[...] = jnp.zeros_like(x_buf)
        spare_rows = 2 * GROUP * ROW_TILE
        zero_spare = pltpu.make_async_copy(x_buf.at[0, pl.ds(0, spare_rows)],
                                           slots_ref.at[pl.ds(slots_ref.shape[0] - spare_rows, spare_rows)],
                                           scatter_sem.at[0])
        zero_spare.start()
        zero_spare.wait()
        for_each_group(0, lambda r, p: gather_copy(src_ref, r, 0).start(priority=p))

        @pl.when(1 < used)
        def _():
            for_each_group(1, lambda r, p: gather_copy(src_next_ref, r, 1).start(priority=p))

    @pl.when(i + GATHER_AHEAD < used)
    def _():
        for_each_group(i + GATHER_AHEAD,
                       lambda r, p: gather_copy(src_ahead_ref, r, (i + GATHER_AHEAD) % GATHER_SLOTS).start(priority=p))

    has_prev = jnp.logical_and(i > 0, i < used)

    @pl.when(has_prev)
    def _():
        for_each_group(i - 1, lambda r, p: scatter_copy(dst_prev_ref, r, 1 - cur).start(priority=p))

    @pl.when(jnp.logical_or(i == 0, be_ref[i] != be_ref[jnp.maximum(i - 1, 0)]))
    def _():
        wg_s[...] = wg_ref[0, 0].astype(BF16)
        wu_s[...] = wu_ref[0, 0].astype(BF16)
        wd_s[...] = wd_ref[0, 0].astype(BF16)

    @pl.when(i < used)
    def _():
        for_each_group(i, lambda r, p: gather_copy(src_ref, 0, slot).wait())
        x = jnp.concatenate([_load_token_slab(x_buf.at[slot], EXPERT_BLOCK, ROW_TILE, c) for c in range(ROW_TILE)],
                            axis=1).astype(BF16)
        hidden = _silu(_dot(x, wg_s[...])) * _dot(x, wu_s[...])
        _store_token_tiles(y_buf.at[cur], _dot(hidden.astype(BF16), wd_s[...]), ROW_TILE)

    @pl.when(has_prev)
    def _():
        for_each_group(i - 1, lambda r, p: scatter_copy(dst_prev_ref, 0, 1 - cur).wait())

    @pl.when(i == used - 1)
    def _():
        for_each_group(i, lambda r, p: scatter_copy(dst_ref, r, cur).start(priority=p))
        for_each_group(i, lambda r, p: scatter_copy(dst_ref, 0, cur).wait())


def _experts(h2_tiles, plan, w_gate, w_up, w_down, layer):
    src_token, dst_slot, block_expert, copies, used_blocks = plan
    n_blocks = src_token.shape[0]
    n_slots = 2 * (h2_tiles.shape[0] // ROW_TILE) + 2 * GROUP
    last = n_blocks - 1

    def smem_block(index):
        return pl.BlockSpec((1, 1, EXPERT_BLOCK), index, memory_space=pltpu.SMEM)

    def weight(shape):
        return pl.BlockSpec((1, 1) + shape, lambda i, be, copies, used: (layer, be[i], 0, 0))

    grid_spec = pltpu.PrefetchScalarGridSpec(
        num_scalar_prefetch=3,
        grid=(n_blocks,),
        in_specs=[
            smem_block(lambda i, be, copies, used: (i, 0, 0)),
            smem_block(lambda i, be, copies, used: (jnp.minimum(i + 1, last), 0, 0)),
            smem_block(lambda i, be, copies, used: (jnp.minimum(i + GATHER_AHEAD, last), 0, 0)),
            smem_block(lambda i, be, copies, used: (i, 0, 0)),
            smem_block(lambda i, be, copies, used: (jnp.maximum(i - 1, 0), 0, 0)),
            pl.BlockSpec(memory_space=pl.ANY),
            weight((D_MODEL, EXPERT_FF)), weight((D_MODEL, EXPERT_FF)), weight((EXPERT_FF, D_MODEL)),
        ],
        out_specs=pl.BlockSpec(memory_space=pl.ANY),
        scratch_shapes=[
            pltpu.VMEM((GATHER_SLOTS, EXPERT_BLOCK * ROW_TILE, LANES), F32),
            pltpu.VMEM((2, EXPERT_BLOCK * ROW_TILE, LANES), F32),
            pltpu.VMEM((D_MODEL, EXPERT_FF), BF16),
            pltpu.VMEM((D_MODEL, EXPERT_FF), BF16),
            pltpu.VMEM((EXPERT_FF, D_MODEL), BF16),
            pltpu.SemaphoreType.DMA((GATHER_SLOTS,)),
            pltpu.SemaphoreType.DMA((2,)),
        ],
    )
    return pl.pallas_call(
        _expert_kernel,
        grid_spec=grid_spec,
        out_shape=jax.ShapeDtypeStruct((n_slots * ROW_TILE, LANES), F32),
        compiler_params=_params(("arbitrary",), VMEM_LIMIT),
        name="experts",
    )(block_expert, copies, used_blocks, src_token, src_token, src_token, dst_slot, dst_slot, h2_tiles,
      w_gate, w_up, w_down)


def _final_kernel(slots_ref, x_ref, route_ref, mod_ref, fg_ref, o_ref):
    sq = _moe_residual(slots_ref, x_ref, route_ref, mod_ref, o_ref)
    o_ref[...] = o_ref[...] * lax.rsqrt(sq / D_MODEL + EPS) * fg_ref[...]


def _final_combine(slots, x_all, route, mod, final_g, n_batch):
    tiles = TOK // TM
    lat_tiles = SEQ // TM
    tok = lambda b, j: (b * tiles + CTX_LEN // TM + j, 0)
    return pl.pallas_call(
        _final_kernel,
        grid=(n_batch, lat_tiles),
        in_specs=[
            pl.BlockSpec((TM * 2 * ROW_TILE, LANES), tok),
            pl.BlockSpec((TM, D_MODEL), tok),
            pl.BlockSpec((TM, LANES), tok),
            pl.BlockSpec((1, 6, D_MODEL), lambda b, j: (b, 0, 0)),
            pl.BlockSpec((1, D_MODEL), lambda b, j: (0, 0)),
        ],
        out_specs=pl.BlockSpec((TM, D_MODEL), lambda b, j: (b * lat_tiles + j, 0)),
        out_shape=jax.ShapeDtypeStruct((n_batch * SEQ, D_MODEL), F32),
        compiler_params=_params(("parallel", "parallel")),
        name="final_combine",
    )(slots, x_all, route, mod, final_g.reshape(1, D_MODEL))


def _rope_tables():
    t = jnp.arange(SEQ)
    rows = (t // GRID_W).astype(F32)
    cols = (t % GRID_W).astype(F32)
    n_freq = HEAD_DIM // 4
    inv_freq = ROPE_BASE ** (-jnp.arange(n_freq, dtype=F32) / n_freq)
    ang_r = rows[:, None] * inv_freq
    ang_c = cols[:, None] * inv_freq
    cos = jnp.concatenate([jnp.cos(ang_r)] * 2 + [jnp.cos(ang_c)] * 2, axis=1)
    sin = jnp.concatenate([-jnp.sin(ang_r), jnp.sin(ang_r), -jnp.sin(ang_c), jnp.sin(ang_c)], axis=1)
    return cos, sin


def kernel(x, c, ctx, c_ctx, w_mod, b_mod, norm1_g, norm2_g, w_in, ret_decay, ret_norm_g, conv_w, conv_b,
           mlstm_gate_b, mlstm_norm_g, na_rpb, w_branch, w_out, w_group, w_router, w_expert_gate,
           w_expert_up, w_expert_down, final_norm_g):
    n_batch, seq, d = x.shape
    depth = w_mod.shape[0]
    assert (seq, d, ctx.shape[1]) == (SEQ, D_MODEL, CTX_LEN)
    n = n_batch * TOK

    cond_rows = -(-(n_batch + 1) // 8) * 8
    cond = jnp.zeros((cond_rows, d), F32).at[:n_batch].set(c).at[n_batch].set(c_ctx)
    mod_all = _modulation(cond, w_mod, b_mod).reshape(depth, cond_rows, 6, d)

    cos, sin = _rope_tables()
    x_all = jnp.concatenate([ctx, x], axis=1).reshape(n, d)

    w_packed = _pack_in_weights(w_in)
    w_branch_b = w_branch.astype(BF16)
    w_out_b = w_out.astype(BF16)
    w_route = jnp.concatenate([w_group, w_router], axis=-1)
    w_route = jnp.pad(w_route, ((0, 0), (0, 0), (0, LANES - w_route.shape[-1])))
    wr_hi = w_route.astype(BF16)
    wr_lo = (w_route - wr_hi.astype(F32)).astype(BF16)
    w_route_split = jnp.concatenate([wr_hi, wr_lo], axis=-1)
    gate_b = jnp.pad(mlstm_gate_b.reshape(depth, 1, 4 * N_HEADS), ((0, 0), (0, 0), (0, LANES - 4 * N_HEADS)))
    na_bias = _na_bias_tables(na_rpb)

    moe = None
    for layer in range(depth):
        mod = mod_all[layer]
        if moe is None:
            ret, ml, gates, na, bg = _in_projection(x_all, mod, norm1_g, w_packed, layer, n_batch)
        else:
            x_all, ret, ml, gates, na, bg = _in_projection(x_all, mod, norm1_g, w_packed, layer, n_batch, moe)
        r_out = _retention(ret, ret_decay[layer], ret_norm_g[layer], cos, sin, n_batch)
        m_out = _mlstm(ml, gates, gate_b[layer], conv_w[layer], conv_b[layer].reshape(1, 2 * WIDTH),
                       mlstm_norm_g[layer], n_batch)
        a_out = _neighbourhood_attention(na, na_bias, layer, n_batch)
        x_all, h2, logits = _merge(r_out, m_out, a_out, bg, x_all, mod, w_branch_b, w_out_b, norm2_g,
                                   w_route_split, layer, n_batch)
        route, counts = _route(logits)
        plan = _dispatch_plan(route, counts)
        slots = _experts(h2, plan, w_expert_gate, w_expert_up, w_expert_down, layer)
        moe = (slots, route, mod)

    return _final_combine(slots, x_all, route, mod, final_norm_g, n_batch).reshape(n_batch, SEQ, d)
```

```python
import functools

import numpy as np
import jax
import jax.numpy as jnp
from jax import lax
from jax.experimental import pallas as pl
from jax.experimental.pallas import tpu as pltpu

F32 = jnp.float32
BF16 = jnp.bfloat16

D_MODEL = 1024
SEQ = 2048
CTX_LEN = 256
TOK = CTX_LEN + SEQ
GRID_W = 64
GRID_ROWS = SEQ // GRID_W
HEAD_DIM = 128
N_HEADS = 4
WIDTH = N_HEADS * HEAD_DIM
CHUNK = 256
N_CHUNKS = TOK // CHUNK
CTX_CHUNKS = CTX_LEN // CHUNK
CONV_WIDTH = 5
NA_WIN_ROWS = 8
NA_WIN_COLS = 16
NA_Q_ROWS = 4
NA_Q = NA_Q_ROWS * GRID_W
NA_K_ROWS = NA_Q_ROWS + NA_WIN_ROWS
NA_K = NA_K_ROWS * GRID_W
NA_STEPS = SEQ // NA_Q
ROPE_BASE = 10000.0
N_GROUPS = 4
EXPERTS_PER_GROUP = 8
N_EXPERTS = N_GROUPS * EXPERTS_PER_GROUP
EXPERT_FF = 512
EXPERT_BLOCK = 256
EPS = 1e-6
NEG_INF = -1e30
QK_SCALE = HEAD_DIM ** -0.5

TM = 256
ROUTE_TM = 512
LANES = 128
VMEM_LIMIT = 56 * 1024 * 1024


def _dot(a, b):
    return jnp.dot(a, b, preferred_element_type=F32)


def _dot_nt(a, b):
    return lax.dot_general(a, b, (((1,), (1,)), ((), ())), preferred_element_type=F32)


def _bdot(a, b):
    return _dot(a.astype(BF16), b.astype(BF16))


def _bdot_nt(a, b):
    return _dot_nt(a.astype(BF16), b.astype(BF16))


def _bdot_tn(a, b):
    return _dot(a.T.astype(BF16), b.astype(BF16))


def _log_sigmoid(x):
    return jnp.minimum(x, 0.0) - jnp.log1p(jnp.exp(-jnp.abs(x)))


def _silu(x):
    return x * jax.nn.sigmoid(x)


def _params(sem, vmem=None):
    return pltpu.CompilerParams(dimension_semantics=sem, vmem_limit_bytes=vmem)


def _mod_row(n_batch):
    tiles = TOK // TM

    def index(i):
        return (jnp.where(i % tiles == 0, n_batch, i // tiles), 0, 0)

    return index


def _mod_kernel(c_ref, w_ref, b_ref, o_ref):
    cond = _silu(c_ref[...])
    o_ref[0] = _bdot(cond, w_ref[0]) + b_ref[0]


def _modulation(cond, w_mod, b_mod):
    depth, d, cols = w_mod.shape
    rows = cond.shape[0]
    tn = 1536
    return pl.pallas_call(
        _mod_kernel,
        grid=(depth, cols // tn),
        in_specs=[
            pl.BlockSpec((rows, d), lambda l, j: (0, 0)),
            pl.BlockSpec((1, d, tn), lambda l, j: (l, 0, j)),
            pl.BlockSpec((1, 1, tn), lambda l, j: (l, 0, j)),
        ],
        out_specs=pl.BlockSpec((1, rows, tn), lambda l, j: (l, 0, j)),
        out_shape=jax.ShapeDtypeStruct((depth, rows, cols), F32),
        compiler_params=_params(("parallel", "parallel")),
        name="modulation",
    )(cond, w_mod, b_mod.reshape(depth, 1, cols))


def _rms_modulate(x, g, shift, scale):
    y = x * lax.rsqrt(jnp.mean(x * x, axis=-1, keepdims=True) + EPS) * g
    return y * (1.0 + scale) + shift


IN_WIDTHS = (4 * WIDTH, 4 * WIDTH, LANES, 3 * WIDTH, 3 * D_MODEL)
IN_OFFSETS = tuple(int(v) for v in np.cumsum((0,) + IN_WIDTHS))
IN_DTYPES = (F32, F32, F32, F32, BF16)


def _pack_kernel(w_ref, o_ref):
    gate0 = IN_OFFSETS[2]
    n_gate = 4 * N_HEADS
    o_ref[0, :, :gate0] = w_ref[0, :, :gate0].astype(BF16)
    window = w_ref[0, :, gate0:gate0 + LANES]
    lane = lax.broadcasted_iota(jnp.int32, window.shape, 1)
    o_ref[0, :, gate0:gate0 + LANES] = jnp.where(lane < n_gate, window, 0.0).astype(BF16)
    o_ref[0, :, gate0 + LANES:] = w_ref[0, :, gate0 + n_gate:].astype(BF16)


def _pack_in_weights(w_in):
    depth, d, cols = w_in.shape
    rows = 256
    return pl.pallas_call(
        _pack_kernel,
        grid=(depth, d // rows),
        in_specs=[pl.BlockSpec((1, rows, cols), lambda l, r: (l, r, 0))],
        out_specs=pl.BlockSpec((1, rows, IN_OFFSETS[-1]), lambda l, r: (l, r, 0)),
        out_shape=jax.ShapeDtypeStruct((depth, d, IN_OFFSETS[-1]), BF16),
        compiler_params=_params(("parallel", "parallel"), VMEM_LIMIT),
        name="pack_in_weights",
    )(w_in)


def _project(x, mod_ref, g_ref, w_ref, out_refs):
    h = _rms_modulate(x, g_ref[0], mod_ref[0, 0:1, :], mod_ref[0, 1:2, :]).astype(BF16)
    for o_ref, lo, hi in zip(out_refs, IN_OFFSETS[:-1], IN_OFFSETS[1:]):
        o_ref[...] = _dot(h, w_ref[0, :, lo:hi]).astype(o_ref.dtype)


def _inproj_kernel(x_ref, mod_ref, g_ref, w_ref, *out_refs):
    _project(x_ref[...], mod_ref, g_ref, w_ref, out_refs)


def _combine_inproj_kernel(slots_ref, x_ref, route_ref, prev_mod_ref, mod_ref, g_ref, w_ref, x_out_ref, *out_refs):
    _moe_residual(slots_ref, x_ref, route_ref, prev_mod_ref, x_out_ref)
    _project(x_out_ref[...], mod_ref, g_ref, w_ref, out_refs)


def _in_projection(x_all, mod, norm_g, w_packed, layer, n_batch, moe=None):
    n = x_all.shape[0]
    row = lambda i: (i, 0)
    in_specs = [
        pl.BlockSpec((TM, D_MODEL), row),
        pl.BlockSpec((1, 6, D_MODEL), _mod_row(n_batch)),
        pl.BlockSpec((1, 1, D_MODEL), lambda i: (layer, 0, 0)),
        pl.BlockSpec((1,) + w_packed.shape[1:], lambda i: (layer, 0, 0), pipeline_mode=pl.Buffered(1)),
    ]
    out_specs = [pl.BlockSpec((TM, w), row) for w in IN_WIDTHS]
    out_shape = [jax.ShapeDtypeStruct((n, w), dt) for w, dt in zip(IN_WIDTHS, IN_DTYPES)]
    args = (x_all, mod, norm_g.reshape(-1, 1, D_MODEL), w_packed)
    if moe is None:
        body = _inproj_kernel
    else:
        slots, route, prev_mod = moe
        body = _combine_inproj_kernel
        in_specs = ([pl.BlockSpec((TM * 2 * ROW_TILE, LANES), row), in_specs[0], pl.BlockSpec((TM, LANES), row),
                     pl.BlockSpec((1, 6, D_MODEL), _mod_row(n_batch))] + in_specs[1:])
        out_specs = [pl.BlockSpec((TM, D_MODEL), row)] + out_specs
        out_shape = [jax.ShapeDtypeStruct((n, D_MODEL), F32)] + out_shape
        args = (slots, x_all, route, prev_mod) + args[1:]
    return pl.pallas_call(
        body,
        grid=(n // TM,),
        in_specs=in_specs,
        out_specs=out_specs,
        out_shape=out_shape,
        compiler_params=_params(("parallel",), VMEM_LIMIT),
        name="in_projection",
    )(*args)


def _chunk_order(t):
    fwd = t
    bwd = jnp.where(t < CTX_CHUNKS, CTX_CHUNKS - 1 - t, N_CHUNKS + CTX_CHUNKS - 1 - t)
    return fwd, bwd


def _chunk_slice(c):
    return pl.ds(pl.multiple_of(c * CHUNK, CHUNK), CHUNK)


def _head_norm(y, gain):
    mu = jnp.mean(y, axis=-1, keepdims=True)
    yc = y - mu
    var = jnp.mean(yc * yc, axis=-1, keepdims=True)
    return yc * lax.rsqrt(var + EPS) * gain


ROW_TILE = D_MODEL // LANES


def _store_token_tiles(ref, x, pitch, offset=0):
    rows = x.shape[0]
    for c in range(ROW_TILE):
        ref[pl.ds(offset + c, rows, stride=pitch), :] = x[:, c * LANES:(c + 1) * LANES]


def _load_token_slab(ref, rows, pitch, c, offset=0):
    return ref[pl.ds(offset + c, rows, stride=pitch), :]


def _moe_residual(slots_ref, x_ref, route_ref, mod_ref, o_ref):
    route = route_ref[...]
    w1 = route[:, ROUTE_W1:ROUTE_W1 + 1]
    w2 = route[:, ROUTE_W2:ROUTE_W2 + 1]
    rows = x_ref.shape[0]
    sq = jnp.zeros((rows, 1), F32)
    for c in range(ROW_TILE):
        lanes = slice(c * LANES, (c + 1) * LANES)
        y1 = _load_token_slab(slots_ref, rows, 2 * ROW_TILE, c)
        y2 = _load_token_slab(slots_ref, rows, 2 * ROW_TILE, c, offset=ROW_TILE)
        x_new = x_ref[:, lanes] + mod_ref[0, 5:6, lanes] * (y1 * w1 + y2 * w2)
        o_ref[:, lanes] = x_new
        sq = sq + jnp.sum(x_new * x_new, axis=-1, keepdims=True)
    return sq


def _select_lane(x, lane, idx):
    return jnp.sum(jnp.where(lane == idx, x, 0.0), axis=-1, keepdims=True)


HEADS_PER_STEP = 2
HEAD_STEPS = N_HEADS // HEADS_PER_STEP
STEP_WIDTH = HEADS_PER_STEP * HEAD_DIM
SCAN_VMEM = 48 * 1024 * 1024


def _head_lanes(hh):
    return slice(hh * HEAD_DIM, (hh + 1) * HEAD_DIM)


def _retention_kernel(dec_ref, q_ref, k_ref, v_ref, g_ref, cos_ref, sin_ref, gn_ref, o_ref,
                      qs_ref, ks_ref, of_ref, ob_ref, intra_ref, st_ref):
    first_head = pl.program_id(1) * HEADS_PER_STEP

    lane = lax.broadcasted_iota(jnp.int32, (CHUNK, HEAD_DIM), 1)
    first_half = (lane % (HEAD_DIM // 2)) < (HEAD_DIM // 4)

    def rope(x, cos, sin):
        rot = jnp.where(first_half, pltpu.roll(x, HEAD_DIM - HEAD_DIM // 4, 1), pltpu.roll(x, HEAD_DIM // 4, 1))
        return x * cos + rot * sin

    qs_ref[0:CTX_LEN, :] = q_ref[0:CTX_LEN, :]
    ks_ref[0:CTX_LEN, :] = k_ref[0:CTX_LEN, :] * QK_SCALE

    def rotate_chunk(c, carry):
        rows = _chunk_slice(c)
        pos = _chunk_slice(c - CTX_CHUNKS)
        cos = cos_ref[pos, :]
        sin = sin_ref[pos, :]
        for hh in range(HEADS_PER_STEP):
            qs_ref[rows, _head_lanes(hh)] = rope(q_ref[rows, _head_lanes(hh)], cos, sin)
            ks_ref[rows, _head_lanes(hh)] = rope(k_ref[rows, _head_lanes(hh)], cos, sin) * QK_SCALE
        return carry

    lax.fori_loop(CTX_CHUNKS, N_CHUNKS, rotate_chunk, 0)

    ii = lax.broadcasted_iota(jnp.int32, (CHUNK, CHUNK), 0).astype(F32)
    jj = lax.broadcasted_iota(jnp.int32, (CHUNK, CHUNK), 1).astype(F32)
    col = lax.broadcasted_iota(jnp.int32, (CHUNK, 1), 0).astype(F32)

    def decay_mat(dist, lg_dir):
        ok = dist >= 0
        return jnp.where(ok, jnp.exp(jnp.where(ok, dist, 0.0) * lg_dir), 0.0)

    dec = _log_sigmoid(dec_ref[...])
    hl = lax.broadcasted_iota(jnp.int32, dec.shape, 1)
    consts = []
    for hh in range(HEADS_PER_STEP):
        lg = jnp.sum(jnp.where(hl == first_head + hh, dec, 0.0), axis=-1, keepdims=True)
        lg_f, lg_b = lg[0:1, :], lg[1:2, :]
        intra_ref[2 * hh] = decay_mat(ii - jj, lg_f)
        intra_ref[2 * hh + 1] = decay_mat(jj - ii, lg_b)
        consts.append(dict(
            q_decay=(jnp.exp((col + 1.0) * lg_f), jnp.exp((CHUNK - col) * lg_b)),
            k_decay=(jnp.exp((CHUNK - 1.0 - col) * lg_f), jnp.exp(col * lg_b)),
            chunk_decay=(jnp.exp(CHUNK * lg_f), jnp.exp(CHUNK * lg_b))))

    st_ref[...] = jnp.zeros_like(st_ref)
    out_refs = (of_ref, ob_ref)

    def step(t, carry):
        for hh in range(HEADS_PER_STEP):
            cst = consts[hh]
            for d, c in enumerate(_chunk_order(t)):
                rows = _chunk_slice(c)
                q = qs_ref[rows, _head_lanes(hh)]
                k = ks_ref[rows, _head_lanes(hh)]
                v = v_ref[rows, _head_lanes(hh)]
                s_prev = st_ref[2 * hh + d]
                scores = _bdot_nt(q, k) * intra_ref[2 * hh + d]
                out_refs[d][rows, _head_lanes(hh)] = _bdot(scores, v) + _bdot(q * cst["q_decay"][d], s_prev)
                st_ref[2 * hh + d] = s_prev * cst["chunk_decay"][d] + _bdot_tn(k * cst["k_decay"][d], v)
        return carry

    lax.fori_loop(0, N_CHUNKS, step, 0)

    def finish_chunk(c, carry):
        rows = _chunk_slice(c)
        for hh in range(HEADS_PER_STEP):
            lanes = _head_lanes(hh)
            y = of_ref[rows, lanes] + ob_ref[rows, lanes]
            o_ref[rows, lanes] = (_head_norm(y, gn_ref[:, lanes]) * _silu(g_ref[rows, lanes])).astype(o_ref.dtype)
        return carry

    lax.fori_loop(0, N_CHUNKS, finish_chunk, 0)


def _retention(ret, ret_decay, norm_g, cos, sin, n_batch):
    n = ret.shape[0]

    def head_block(offset):
        return pl.BlockSpec((TOK, STEP_WIDTH), lambda b, h: (b, offset * HEAD_STEPS + h))

    return pl.pallas_call(
        _retention_kernel,
        grid=(n_batch, HEAD_STEPS),
        in_specs=[
            pl.BlockSpec(ret_decay.shape, lambda b, h: (0, 0)),
            head_block(0), head_block(1), head_block(2), head_block(3),
            pl.BlockSpec((SEQ, HEAD_DIM), lambda b, h: (0, 0)),
            pl.BlockSpec((SEQ, HEAD_DIM), lambda b, h: (0, 0)),
            pl.BlockSpec((1, STEP_WIDTH), lambda b, h: (0, h)),
        ],
        out_specs=pl.BlockSpec((TOK, STEP_WIDTH), lambda b, h: (b, h)),
        out_shape=jax.ShapeDtypeStruct((n, WIDTH), BF16),
        scratch_shapes=[
            pltpu.VMEM((TOK, STEP_WIDTH), F32),
            pltpu.VMEM((TOK, STEP_WIDTH), F32),
            pltpu.VMEM((TOK, STEP_WIDTH), F32),
            pltpu.VMEM((TOK, STEP_WIDTH), F32),
            pltpu.VMEM((2 * HEADS_PER_STEP, CHUNK, CHUNK), F32),
            pltpu.VMEM((2 * HEADS_PER_STEP, HEAD_DIM, HEAD_DIM), F32),
        ],
        compiler_params=_params(("parallel", "parallel"), SCAN_VMEM),
        name="retention",
    )(ret_decay, ret, ret, ret, ret, cos, sin, norm_g.reshape(1, WIDTH))


CONV_PAD = 8
CONV_ROWS = 128


def _mlstm_kernel(q_ref, k_ref, v_ref, og_ref, gate_ref, gb_ref, cwq_ref, cwk_ref, cbq_ref, cbk_ref,
                  gn_ref, o_ref, qs_ref, ks_ref, of_ref, ob_ref, pad_ref, gx_ref, gxt_ref, cst_ref):
    first_head = pl.program_id(1) * HEADS_PER_STEP

    pad_ref[0:CONV_PAD, :] = jnp.zeros((CONV_PAD, STEP_WIDTH), F32)
    pad_ref[CONV_PAD + TOK:, :] = jnp.zeros((CONV_PAD, STEP_WIDTH), F32)
    crow = lax.broadcasted_iota(jnp.int32, (CONV_ROWS, 1), 0)

    def conv(u_ref, w_ref, b_ref, dst_ref, scale):
        def fill(c, carry):
            rows = _chunk_slice(c)
            pad_ref[pl.ds(pl.multiple_of(c * CHUNK + CONV_PAD, CONV_PAD), CHUNK), :] = u_ref[rows, :]
            return carry

        lax.fori_loop(0, N_CHUNKS, fill, 0)

        for blk in range(TOK // CONV_ROWS):
            first = blk * CONV_ROWS
            acc = jnp.zeros((CONV_ROWS, STEP_WIDTH), F32)
            for j in range(CONV_WIDTH):
                shift = j - CONV_WIDTH // 2
                tap = pad_ref[first + CONV_PAD + shift:first + CONV_PAD + shift + CONV_ROWS, :]
                if (first + CONV_ROWS == CTX_LEN and shift > 0) or (first == CTX_LEN and shift < 0):
                    trow = crow + first
                    tap = jnp.where((trow < CTX_LEN) == (trow + shift < CTX_LEN), tap, 0.0)
                acc = acc + tap * w_ref[j:j + 1, :]
            dst_ref[first:first + CONV_ROWS, :] = _silu(acc + b_ref[...]) * scale

    conv(q_ref, cwq_ref, cbq_ref, qs_ref, 1.0)
    conv(k_ref, cwk_ref, cbk_ref, ks_ref, QK_SCALE)

    clane = lax.broadcasted_iota(jnp.int32, (CHUNK, LANES), 1)
    is_forget = ((clane // N_HEADS) % 2) == 1

    def gate_chunk(c, carry):
        rows = _chunk_slice(c)
        g = gate_ref[rows, :] + gb_ref[...]
        gx = jnp.where(is_forget, _log_sigmoid(g), g)
        gx_ref[rows, :] = gx
        gxt_ref[c] = gx.T
        return carry

    lax.fori_loop(0, N_CHUNKS, gate_chunk, 0)

    ii = lax.broadcasted_iota(jnp.int32, (CHUNK, CHUNK), 0)
    jj = lax.broadcasted_iota(jnp.int32, (CHUNK, CHUNK), 1)
    causal = (jj <= ii, jj >= ii)

    cst_ref[...] = jnp.zeros_like(cst_ref)
    out_refs = (of_ref, ob_ref)

    def step(t, carry):
        new_carry = []
        for hh in range(HEADS_PER_STEP):
            head = first_head + hh
            lanes = _head_lanes(hh)
            for d, c in enumerate(_chunk_order(t)):
                slot = 2 * hh + d
                n_prev, m_prev = carry[2 * slot], carry[2 * slot + 1]
                rows = _chunk_slice(c)
                q = qs_ref[rows, lanes]
                k = ks_ref[rows, lanes]
                v = v_ref[rows, lanes]
                gc = gx_ref[rows, :]
                i_col = _select_lane(gc, clane, 2 * d * N_HEADS + head)
                f_col = _select_lane(gc, clane, (2 * d + 1) * N_HEADS + head)
                i_row = gxt_ref[c, pl.ds(2 * d * N_HEADS + head, 1), :]
                f_row = gxt_ref[c, pl.ds((2 * d + 1) * N_HEADS + head, 1), :]
                vis = causal[d]
                cum_col = jnp.sum(jnp.where(vis, f_row, 0.0), axis=1, keepdims=True)
                cum_row = jnp.sum(jnp.where(causal[1 - d], f_col, 0.0), axis=0, keepdims=True)
                total = jnp.sum(f_row, axis=1, keepdims=True)
                c_prev = cst_ref[slot]

                log_kw = total - cum_col + i_col
                m_new = jnp.maximum(total + m_prev, jnp.max(log_kw, axis=0, keepdims=True))
                kw = jnp.exp(log_kw - m_new)
                pw = jnp.exp(total + m_prev - m_new)
                cst_ref[slot] = pw * c_prev + _bdot_tn(k * kw, v)
                n_new = pw * n_prev + jnp.sum(kw * k, axis=0, keepdims=True)

                log_w = jnp.where(vis, cum_col - cum_row + i_row, -jnp.inf)
                log_p = cum_col + m_prev
                m_t = jnp.maximum(log_p, jnp.max(log_w, axis=1, keepdims=True))
                w = jnp.exp(log_w - m_t)
                p = jnp.exp(log_p - m_t)
                qk = _bdot_nt(q, k) * w
                num = _bdot(qk, v) + p * _bdot(q, c_prev)
                den = jnp.sum(qk, axis=1, keepdims=True) + p * jnp.sum(q * n_prev, axis=1, keepdims=True)
                out_refs[d][rows, lanes] = num / jnp.maximum(jnp.abs(den), jnp.exp(-m_t))
                new_carry += [n_new, m_new]
        return tuple(new_carry)

    zero_n = jnp.zeros((1, HEAD_DIM), F32)
    zero_m = jnp.zeros((1, 1), F32)
    lax.fori_loop(0, N_CHUNKS, step, (zero_n, zero_m) * (2 * HEADS_PER_STEP))

    def finish_chunk(c, carry):
        rows = _chunk_slice(c)
        for hh in range(HEADS_PER_STEP):
            lanes = _head_lanes(hh)
            gated = (of_ref[rows, lanes] + ob_ref[rows, lanes]) * jax.nn.sigmoid(og_ref[rows, lanes])
            o_ref[rows, lanes] = _head_norm(gated, gn_ref[:, lanes]).astype(o_ref.dtype)
        return carry

    lax.fori_loop(0, N_CHUNKS, finish_chunk, 0)


def _mlstm(ml, gates, gate_b, conv_w, conv_b, norm_g, n_batch):
    n = ml.shape[0]

    def head_block(offset):
        return pl.BlockSpec((TOK, STEP_WIDTH), lambda b, h: (b, offset * HEAD_STEPS + h))

    def head_cols(rows, offset):
        return pl.BlockSpec((rows, STEP_WIDTH), lambda b, h: (0, offset * HEAD_STEPS + h))

    return pl.pallas_call(
        _mlstm_kernel,
        grid=(n_batch, HEAD_STEPS),
        in_specs=[
            head_block(0), head_block(1), head_block(2), head_block(3),
            pl.BlockSpec((TOK, LANES), lambda b, h: (b, 0)),
            pl.BlockSpec((1, LANES), lambda b, h: (0, 0)),
            head_cols(CONV_WIDTH, 0), head_cols(CONV_WIDTH, 1),
            head_cols(1, 0), head_cols(1, 1),
            head_cols(1, 0),
        ],
        out_specs=pl.BlockSpec((TOK, STEP_WIDTH), lambda b, h: (b, h)),
        out_shape=jax.ShapeDtypeStruct((n, WIDTH), BF16),
        scratch_shapes=[
            pltpu.VMEM((TOK, STEP_WIDTH), F32),
            pltpu.VMEM((TOK, STEP_WIDTH), F32),
            pltpu.VMEM((TOK, STEP_WIDTH), F32),
            pltpu.VMEM((TOK, STEP_WIDTH), F32),
            pltpu.VMEM((TOK + 2 * CONV_PAD, STEP_WIDTH), F32),
            pltpu.VMEM((TOK, LANES), F32),
            pltpu.VMEM((N_CHUNKS, LANES, CHUNK), F32),
            pltpu.VMEM((2 * HEADS_PER_STEP, HEAD_DIM, HEAD_DIM), F32),
        ],
        compiler_params=_params(("parallel", "parallel"), SCAN_VMEM),
        name="mlstm",
    )(ml, ml, ml, ml, gates, gate_b, conv_w, conv_w, conv_b, conv_b, norm_g.reshape(1, WIDTH))


def _na_kernel(q_ref, k_ref, v_ref, bias_ref, o_ref):
    step = pl.program_id(2)

    def context_scores(hh):
        lanes = _head_lanes(hh)
        q = q_ref[:, lanes].astype(BF16)
        v_ctx = v_ref[0:CTX_LEN, lanes].astype(BF16)
        return q, v_ctx, _dot_nt(q, k_ref[0:CTX_LEN, lanes].astype(BF16)) * QK_SCALE

    @pl.when(step == 0)
    def _():
        for hh in range(HEADS_PER_STEP):
            _, v_ctx, s_ctx = context_scores(hh)
            m = jnp.max(s_ctx, axis=-1, keepdims=True)
            p = jnp.exp(s_ctx - m)
            o = _dot(p.astype(BF16), v_ctx) / jnp.sum(p, axis=-1, keepdims=True)
            o_ref[:, _head_lanes(hh)] = o.astype(o_ref.dtype)

    @pl.when(step > 0)
    def _():
        start = CTX_LEN + NA_Q * jnp.clip(step - 2, 0, NA_STEPS - NA_K_ROWS // NA_Q_ROWS)
        rows = pl.ds(pl.multiple_of(start, NA_Q), NA_K)
        for hh in range(HEADS_PER_STEP):
            lanes = _head_lanes(hh)
            q, v_ctx, s_ctx = context_scores(hh)
            s_loc = _dot_nt(q, k_ref[rows, lanes].astype(BF16)) * QK_SCALE + bias_ref[0, hh, 0]
            m = jnp.maximum(jnp.max(s_loc, axis=-1, keepdims=True), jnp.max(s_ctx, axis=-1, keepdims=True))
            p_loc = jnp.exp(s_loc - m)
            p_ctx = jnp.exp(s_ctx - m)
            denom = jnp.sum(p_loc, axis=-1, keepdims=True) + jnp.sum(p_ctx, axis=-1, keepdims=True)
            o = _dot(p_loc.astype(BF16), v_ref[rows, lanes].astype(BF16)) + _dot(p_ctx.astype(BF16), v_ctx)
            o_ref[:, lanes] = (o / denom).astype(o_ref.dtype)


def _na_row_case(case, a):
    t = np.arange(NA_K_ROWS)
    last_start = NA_K_ROWS - NA_WIN_ROWS
    return [
        (t < NA_WIN_ROWS, NA_WIN_ROWS - 1 - a),
        ((t >= a) & (t < a + NA_WIN_ROWS), NA_WIN_ROWS // 2 - 1 - a),
        (t >= last_start, NA_Q_ROWS - NA_K_ROWS + NA_WIN_ROWS - 1 - a),
    ][case]


def _na_expand_kernel(slab_ref, o_ref):
    masked = jnp.full((GRID_W, GRID_W), NEG_INF, F32)
    for case in range(3):
        for a in range(NA_Q_ROWS):
            row_ok, first = _na_row_case(case, a)
            for pair in range(NA_K_ROWS // 2):
                tiles = [slab_ref[0, first + t] if row_ok[t] else masked for t in (2 * pair, 2 * pair + 1)]
                o_ref[0, case, a * GRID_W:(a + 1) * GRID_W, pair * LANES:(pair + 1) * LANES] = (
                    jnp.concatenate(tiles, axis=1))


def _na_bias_tables(rpb):
    lead = rpb.shape[:-2]
    n_row_off, n_col_off = rpb.shape[-2:]
    qc = np.arange(GRID_W)[:, None]
    kc = np.arange(GRID_W)[None, :]
    col_start = np.clip(qc - NA_WIN_COLS // 2, 0, GRID_W - NA_WIN_COLS)
    col_ok = (kc >= col_start) & (kc < col_start + NA_WIN_COLS)
    col_idx = np.clip(kc - qc + NA_WIN_COLS - 1, 0, n_col_off - 1)
    onehot = ((col_idx[None] == np.arange(n_col_off)[:, None, None]) & col_ok[None]).astype(np.float32)
    slabs = jnp.einsum('...rc,cqk->...rqk', rpb.astype(F32), onehot, precision=lax.Precision.HIGHEST)
    slabs = jnp.where(col_ok, slabs, NEG_INF).reshape(-1, n_row_off, GRID_W, GRID_W)
    n_tables = slabs.shape[0]
    tables = pl.pallas_call(
        _na_expand_kernel,
        grid=(n_tables,),
        in_specs=[pl.BlockSpec((1, n_row_off, GRID_W, GRID_W), lambda i: (i, 0, 0, 0))],
        out_specs=pl.BlockSpec((1, 3, NA_Q, NA_K), lambda i: (i, 0, 0, 0)),
        out_shape=jax.ShapeDtypeStruct((n_tables, 3, NA_Q, NA_K), F32),
        compiler_params=_params(("parallel",)),
        name="na_bias_tables",
    )(slabs)
    return tables.reshape(*lead, 3, NA_Q, NA_K)


def _neighbourhood_attention(na, bias, layer, n_batch):
    n = na.shape[0]
    steps = 1 + NA_STEPS
    tiles = TOK // NA_Q

    def table(b, h, j):
        return (layer, h, jnp.where(j <= 1, 0, jnp.where(j == NA_STEPS, 2, 1)), 0, 0)

    return pl.pallas_call(
        _na_kernel,
        grid=(n_batch, HEAD_STEPS, steps),
        in_specs=[
            pl.BlockSpec((NA_Q, STEP_WIDTH), lambda b, h, j: (b * tiles + j, h)),
            pl.BlockSpec((TOK, STEP_WIDTH), lambda b, h, j: (b, HEAD_STEPS + h)),
            pl.BlockSpec((TOK, STEP_WIDTH), lambda b, h, j: (b, 2 * HEAD_STEPS + h)),
            pl.BlockSpec((1, HEADS_PER_STEP, 1, NA_Q, NA_K), table),
        ],
        out_specs=pl.BlockSpec((NA_Q, STEP_WIDTH), lambda b, h, j: (b * tiles + j, h)),
        out_shape=jax.ShapeDtypeStruct((n, WIDTH), BF16),
        compiler_params=_params(("parallel", "parallel", "arbitrary")),
        name="neighbourhood_attention",
    )(na, na, na, bias)


def _merge_kernel(r_ref, m_ref, a_ref, bg_ref, x_ref, mod_ref, wb_ref, wo_ref, g2_ref, wr_ref,
                  x_out_ref, h2_ref, logit_ref):
    gate = jax.nn.sigmoid(bg_ref[...].astype(F32))
    mix = (gate[:, 0:D_MODEL] * _dot(r_ref[...], wb_ref[0, 0])
           + gate[:, D_MODEL:2 * D_MODEL] * _dot(m_ref[...], wb_ref[0, 1])
           + gate[:, 2 * D_MODEL:] * _dot(a_ref[...], wb_ref[0, 2]))
    y = _dot(mix.astype(BF16), wo_ref[0])
    x_new = x_ref[...] + mod_ref[0, 2:3, :] * y
    x_out_ref[...] = x_new
    h2 = _rms_modulate(x_new, g2_ref[0], mod_ref[0, 3:4, :], mod_ref[0, 4:5, :])
    _store_token_tiles(h2_ref, h2, ROW_TILE)
    hi = h2.astype(BF16)
    lo = (h2 - hi.astype(F32)).astype(BF16)
    hi_terms = _dot(hi, wr_ref[0])
    logit_ref[...] = hi_terms[:, :LANES] + (_dot(lo, wr_ref[0, :, :LANES]) + hi_terms[:, LANES:])


def _merge(r, m, a, bg, x_all, mod, w_branch, w_out, norm2_g, w_route, layer, n_batch):
    n = x_all.shape[0]
    row = lambda i: (i, 0)

    def layer_block(w):
        return pl.BlockSpec((1,) + w.shape[1:], lambda i: (layer,) + (0,) * (w.ndim - 1))

    return pl.pallas_call(
        _merge_kernel,
        grid=(n // TM,),
        in_specs=[
            pl.BlockSpec((TM, WIDTH), row), pl.BlockSpec((TM, WIDTH), row), pl.BlockSpec((TM, WIDTH), row),
            pl.BlockSpec((TM, 3 * D_MODEL), row),
            pl.BlockSpec((TM, D_MODEL), row),
            pl.BlockSpec((1, 6, D_MODEL), _mod_row(n_batch)),
            layer_block(w_branch), layer_block(w_out),
            pl.BlockSpec((1, 1, D_MODEL), lambda i: (layer, 0, 0)),
            layer_block(w_route),
        ],
        out_specs=[pl.BlockSpec((TM, D_MODEL), row), pl.BlockSpec((TM * ROW_TILE, LANES), row),
                   pl.BlockSpec((TM, LANES), row)],
        out_shape=[jax.ShapeDtypeStruct((n, D_MODEL), F32), jax.ShapeDtypeStruct((n * ROW_TILE, LANES), F32),
                   jax.ShapeDtypeStruct((n, LANES), F32)],
        compiler_params=_params(("parallel",), VMEM_LIMIT),
        name="merge",
    )(r, m, a, bg, x_all, mod, w_branch, w_out, norm2_g.reshape(-1, 1, D_MODEL), w_route)


ROUTE_E1, ROUTE_E2, ROUTE_RANK1, ROUTE_RANK2, ROUTE_W1, ROUTE_W2 = range(6)


def _route_kernel(logit_ref, route_ref, count_ref, cnt_ref):
    @pl.when(pl.program_id(0) == 0)
    def _():
        cnt_ref[...] = jnp.zeros_like(cnt_ref)

    lg = logit_ref[...]
    lane = lax.broadcasted_iota(jnp.int32, lg.shape, 1)
    lane_f = lane.astype(F32)

    def first_argmax(vals):
        top = jnp.max(vals, axis=-1, keepdims=True)
        idx = jnp.min(jnp.where(vals == top, lane_f, float(LANES)), axis=-1, keepdims=True)
        return top, idx

    group_logits = jnp.where(lane < N_GROUPS, lg, -jnp.inf)
    g_top, g_idx = first_argmax(group_logits)
    group_w = 1.0 / jnp.sum(jnp.exp(group_logits - g_top), axis=-1, keepdims=True)

    first = N_GROUPS + EXPERTS_PER_GROUP * g_idx
    in_group = (lane_f >= first) & (lane_f < first + EXPERTS_PER_GROUP)
    expert_logits = jnp.where(in_group, lg, -jnp.inf)
    v1, i1 = first_argmax(expert_logits)
    v2, i2 = first_argmax(jnp.where(lane_f == i1, -jnp.inf, expert_logits))
    t = jnp.exp(v2 - v1)
    w1 = group_w / (1.0 + t)
    w2 = group_w * t / (1.0 + t)

    oh1 = (lane_f == i1).astype(F32)
    oh2 = (lane_f == i2).astype(F32)
    both = oh1 + oh2
    rows = lg.shape[0]
    earlier = (lax.broadcasted_iota(jnp.int32, (rows, rows), 1)
               < lax.broadcasted_iota(jnp.int32, (rows, rows), 0)).astype(BF16)
    before = _dot(earlier, both.astype(BF16)) + cnt_ref[...]
    rank1 = jnp.sum(oh1 * before, axis=-1, keepdims=True)
    rank2 = jnp.sum(oh2 * before, axis=-1, keepdims=True)
    cnt_ref[...] += jnp.sum(both, axis=0, keepdims=True)
    count_ref[...] = jnp.broadcast_to(cnt_ref[...], count_ref.shape)

    out = jnp.zeros_like(lg)
    for slot, val in ((ROUTE_E1, i1 - N_GROUPS), (ROUTE_E2, i2 - N_GROUPS), (ROUTE_RANK1, rank1),
                      (ROUTE_RANK2, rank2), (ROUTE_W1, w1), (ROUTE_W2, w2)):
        out = jnp.where(lane == slot, val, out)
    route_ref[...] = out


def _route(logits):
    n = logits.shape[0]
    return pl.pallas_call(
        _route_kernel,
        grid=(n // ROUTE_TM,),
        in_specs=[pl.BlockSpec((ROUTE_TM, LANES), lambda i: (i, 0))],
        out_specs=[pl.BlockSpec((ROUTE_TM, LANES), lambda i: (i, 0)), pl.BlockSpec((8, LANES), lambda i: (0, 0))],
        out_shape=[jax.ShapeDtypeStruct((n, LANES), F32), jax.ShapeDtypeStruct((8, LANES), F32)],
        scratch_shapes=[pltpu.VMEM((1, LANES), F32)],
        compiler_params=_params(("arbitrary",)),
        name="route",
    )(logits)


GROUP = 8


INVERT_TILE = 1024


def _invert_kernel(valid_ref, dest_ref, assign_ref):
    i = pl.program_id(0)

    @pl.when(i == 0)
    def _():
        def mark_block(b, carry):
            def mark(g, inner):
                for j in range(GROUP):
                    assign_ref[b * EXPERT_BLOCK + g * GROUP + j] = -1
                return inner
            lax.fori_loop(valid_ref[b] // GROUP, EXPERT_BLOCK // GROUP, mark, 0)
            return carry
        lax.fori_loop(0, valid_ref.shape[0], mark_block, 0)

    base = i * INVERT_TILE

    def scatter(g, carry):
        for j in range(GROUP):
            k = g * GROUP + j
            assign_ref[dest_ref[0, 0, k]] = base + k
        return carry

    lax.fori_loop(0, INVERT_TILE // GROUP, scatter, 0)


def _invert(dest, valid, n_rows):
    n_assign = dest.shape[0]
    grid_spec = pltpu.PrefetchScalarGridSpec(
        num_scalar_prefetch=1,
        grid=(n_assign // INVERT_TILE,),
        in_specs=[pl.BlockSpec((1, 1, INVERT_TILE), lambda i, valid: (i, 0, 0), memory_space=pltpu.SMEM)],
        out_specs=pl.BlockSpec(memory_space=pltpu.SMEM),
    )
    return pl.pallas_call(
        _invert_kernel,
        grid_spec=grid_spec,
        out_shape=jax.ShapeDtypeStruct((n_rows,), jnp.int32),
        compiler_params=_params(("arbitrary",)),
        name="invert_assignment",
    )(valid, dest.reshape(n_assign // INVERT_TILE, 1, INVERT_TILE))


def _dispatch_plan(route, counts):
    n = route.shape[0]
    counts = counts[0, N_GROUPS:N_GROUPS + N_EXPERTS].astype(jnp.int32)
    padded = (counts + EXPERT_BLOCK - 1) // EXPERT_BLOCK * EXPERT_BLOCK
    pad_end = jnp.cumsum(padded)
    pad_start = pad_end - padded
    e = route[:, ROUTE_E1:ROUTE_E2 + 1].astype(jnp.int32)
    rank = route[:, ROUTE_RANK1:ROUTE_RANK2 + 1].astype(jnp.int32)
    start_of = jnp.sum(jnp.where(e[..., None] == jnp.arange(N_EXPERTS), pad_start, 0), axis=-1)
    dest = (start_of + rank).reshape(-1)
    n_blocks = (2 * n + N_EXPERTS * (EXPERT_BLOCK - 1) + EXPERT_BLOCK - 1) // EXPERT_BLOCK
    n_rows = n_blocks * EXPERT_BLOCK
    block = jnp.arange(n_blocks, dtype=jnp.int32)
    block_expert = jnp.minimum(jnp.sum(block[:, None] * EXPERT_BLOCK >= pad_end[None, :], axis=1), N_EXPERTS - 1)
    valid = jnp.clip(jnp.sum(jnp.where(block_expert[:, None] == jnp.arange(N_EXPERTS), pad_start + counts, 0), axis=1)
                     - block * EXPERT_BLOCK, 0, EXPERT_BLOCK)
    copies = (valid + GROUP - 1) // GROUP * GROUP
    assign = _invert(dest, valid.astype(jnp.int32), n_rows)
    used_blocks = (pad_end[-1] // EXPERT_BLOCK).reshape(1)
    row = jnp.arange(n_rows, dtype=jnp.int32)
    spare = 2 * n + ((row // EXPERT_BLOCK) % 2) * GROUP + row % GROUP
    src_token = jnp.where(assign < 0, 0, assign // 2).reshape(n_blocks, 1, EXPERT_BLOCK)
    dst_slot = jnp.where(assign < 0, spare, assign).reshape(n_blocks, 1, EXPERT_BLOCK)
    return (src_token, dst_slot, block_expert.astype(jnp.int32), copies.astype(jnp.int32),
            used_blocks.astype(jnp.int32))


def _tile_rows(i):
    return pl.ds(pl.multiple_of(i * ROW_TILE, ROW_TILE), ROW_TILE)


def _expert_kernel(be_ref, copies_ref, used_ref, src_ref, src_next_ref, dst_ref, dst_prev_ref, h_ref,
                   wg_ref, wu_ref, wd_ref, slots_ref, x_buf, y_buf, wg_s, wu_s, wd_s, gather_sem, scatter_sem):
    i = pl.program_id(0)
    used = used_ref[0]
    cur = i % 2

    def gather_copy(idx_ref, r, buf):
        return pltpu.make_async_copy(h_ref.at[_tile_rows(idx_ref[0, 0, r])], x_buf.at[buf, _tile_rows(r)],
                                     gather_sem.at[buf])

    def scatter_copy(idx_ref, r, buf):
        return pltpu.make_async_copy(y_buf.at[buf, _tile_rows(r)], slots_ref.at[_tile_rows(idx_ref[0, 0, r])],
                                     scatter_sem.at[buf])

    def for_each_group(block, fn):
        def body(g, carry):
            for j in range(GROUP):
                fn(g * GROUP + j, j % 2)
            return carry
        lax.fori_loop(0, copies_ref[block] // GROUP, body, 0)

    @pl.when(i == 0)
    def _():
        x_buf[...] = jnp.zeros_like(x_buf)
        spare_rows = 2 * GROUP * ROW_TILE
        zero_spare = pltpu.make_async_copy(x_buf.at[0, pl.ds(0, spare_rows)],
                                           slots_ref.at[pl.ds(slots_ref.shape[0] - spare_rows, spare_rows)],
                                           scatter_sem.at[0])
        zero_spare.start()
        zero_spare.wait()
        for_each_group(0, lambda r, p: gather_copy(src_ref, r, 0).start(priority=1))

    @pl.when(i + 1 < used)
    def _():
        for_each_group(i + 1, lambda r, p: gather_copy(src_next_ref, r, 1 - cur).start(priority=1))

    has_prev = jnp.logical_and(i > 0, i < used)

    @pl.when(has_prev)
    def _():
        for_each_group(i - 1, lambda r, p: scatter_copy(dst_prev_ref, r, 1 - cur).start(priority=p))

    @pl.when(jnp.logical_or(i == 0, be_ref[i] != be_ref[jnp.maximum(i - 1, 0)]))
    def _():
        wg_s[...] = wg_ref[0, 0].astype(BF16)
        wu_s[...] = wu_ref[0, 0].astype(BF16)
        wd_s[...] = wd_ref[0, 0].astype(BF16)

    @pl.when(i < used)
    def _():
        for_each_group(i, lambda r, p: gather_copy(src_ref, 0, cur).wait())
        x = jnp.concatenate([_load_token_slab(x_buf.at[cur], EXPERT_BLOCK, ROW_TILE, c) for c in range(ROW_TILE)],
                            axis=1).astype(BF16)
        hidden = _silu(_dot(x, wg_s[...])) * _dot(x, wu_s[...])
        _store_token_tiles(y_buf.at[cur], _dot(hidden.astype(BF16), wd_s[...]), ROW_TILE)

    @pl.when(has_prev)
    def _():
        for_each_group(i - 1, lambda r, p: scatter_copy(dst_prev_ref, 0, 1 - cur).wait())

    @pl.when(i == used - 1)
    def _():
        for_each_group(i, lambda r, p: scatter_copy(dst_ref, r, cur).start(priority=p))
        for_each_group(i, lambda r, p: scatter_copy(dst_ref, 0, cur).wait())


def _experts(h2_tiles, plan, w_gate, w_up, w_down, layer):
    src_token, dst_slot, block_expert, copies, used_blocks = plan
    n_blocks = src_token.shape[0]
    n_slots = 2 * (h2_tiles.shape[0] // ROW_TILE) + 2 * GROUP
    last = n_blocks - 1

    def smem_block(index):
        return pl.BlockSpec((1, 1, EXPERT_BLOCK), index, memory_space=pltpu.SMEM)

    def weight(shape):
        return pl.BlockSpec((1, 1) + shape, lambda i, be, copies, used: (layer, be[i], 0, 0))

    grid_spec = pltpu.PrefetchScalarGridSpec(
        num_scalar_prefetch=3,
        grid=(n_blocks,),
        in_specs=[
            smem_block(lambda i, be, copies, used: (i, 0, 0)),
            smem_block(lambda i, be, copies, used: (jnp.minimum(i + 1, last), 0, 0)),
            smem_block(lambda i, be, copies, used: (i, 0, 0)),
            smem_block(lambda i, be, copies, used: (jnp.maximum(i - 1, 0), 0, 0)),
            pl.BlockSpec(memory_space=pl.ANY),
            weight((D_MODEL, EXPERT_FF)), weight((D_MODEL, EXPERT_FF)), weight((EXPERT_FF, D_MODEL)),
        ],
        out_specs=pl.BlockSpec(memory_space=pl.ANY),
        scratch_shapes=[
            pltpu.VMEM((2, EXPERT_BLOCK * ROW_TILE, LANES), F32),
            pltpu.VMEM((2, EXPERT_BLOCK * ROW_TILE, LANES), F32),
            pltpu.VMEM((D_MODEL, EXPERT_FF), BF16),
            pltpu.VMEM((D_MODEL, EXPERT_FF), BF16),
            pltpu.VMEM((EXPERT_FF, D_MODEL), BF16),
            pltpu.SemaphoreType.DMA((2,)),
            pltpu.SemaphoreType.DMA((2,)),
        ],
    )
    return pl.pallas_call(
        _expert_kernel,
        grid_spec=grid_spec,
        out_shape=jax.ShapeDtypeStruct((n_slots * ROW_TILE, LANES), F32),
        compiler_params=_params(("arbitrary",), VMEM_LIMIT),
        name="experts",
    )(block_expert, copies, used_blocks, src_token, src_token, dst_slot, dst_slot, h2_tiles, w_gate, w_up, w_down)


def _final_kernel(slots_ref, x_ref, route_ref, mod_ref, fg_ref, o_ref):
    sq = _moe_residual(slots_ref, x_ref, route_ref, mod_ref, o_ref)
    o_ref[...] = o_ref[...] * lax.rsqrt(sq / D_MODEL + EPS) * fg_ref[...]


def _final_combine(slots, x_all, route, mod, final_g, n_batch):
    tiles = TOK // TM
    lat_tiles = SEQ // TM
    tok = lambda b, j: (b * tiles + CTX_LEN // TM + j, 0)
    return pl.pallas_call(
        _final_kernel,
        grid=(n_batch, lat_tiles),
        in_specs=[
            pl.BlockSpec((TM * 2 * ROW_TILE, LANES), tok),
            pl.BlockSpec((TM, D_MODEL), tok),
            pl.BlockSpec((TM, LANES), tok),
            pl.BlockSpec((1, 6, D_MODEL), lambda b, j: (b, 0, 0)),
            pl.BlockSpec((1, D_MODEL), lambda b, j: (0, 0)),
        ],
        out_specs=pl.BlockSpec((TM, D_MODEL), lambda b, j: (b * lat_tiles + j, 0)),
        out_shape=jax.ShapeDtypeStruct((n_batch * SEQ, D_MODEL), F32),
        compiler_params=_params(("parallel", "parallel")),
        name="final_combine",
    )(slots, x_all, route, mod, final_g.reshape(1, D_MODEL))


def _rope_tables():
    t = jnp.arange(SEQ)
    rows = (t // GRID_W).astype(F32)
    cols = (t % GRID_W).astype(F32)
    n_freq = HEAD_DIM // 4
    inv_freq = ROPE_BASE ** (-jnp.arange(n_freq, dtype=F32) / n_freq)
    ang_r = rows[:, None] * inv_freq
    ang_c = cols[:, None] * inv_freq
    cos = jnp.concatenate([jnp.cos(ang_r)] * 2 + [jnp.cos(ang_c)] * 2, axis=1)
    sin = jnp.concatenate([-jnp.sin(ang_r), jnp.sin(ang_r), -jnp.sin(ang_c), jnp.sin(ang_c)], axis=1)
    return cos, sin


def kernel(x, c, ctx, c_ctx, w_mod, b_mod, norm1_g, norm2_g, w_in, ret_decay, ret_norm_g, conv_w, conv_b,
           mlstm_gate_b, mlstm_norm_g, na_rpb, w_branch, w_out, w_group, w_router, w_expert_gate,
           w_expert_up, w_expert_down, final_norm_g):
    n_batch, seq, d = x.shape
    depth = w_mod.shape[0]
    assert (seq, d, ctx.shape[1]) == (SEQ, D_MODEL, CTX_LEN)
    n = n_batch * TOK

    cond_rows = -(-(n_batch + 1) // 8) * 8
    cond = jnp.zeros((cond_rows, d), F32).at[:n_batch].set(c).at[n_batch].set(c_ctx)
    mod_all = _modulation(cond, w_mod, b_mod).reshape(depth, cond_rows, 6, d)

    cos, sin = _rope_tables()
    x_all = jnp.concatenate([ctx, x], axis=1).reshape(n, d)

    w_packed = _pack_in_weights(w_in)
    w_branch_b = w_branch.astype(BF16)
    w_out_b = w_out.astype(BF16)
    w_route = jnp.concatenate([w_group, w_router], axis=-1)
    w_route = jnp.pad(w_route, ((0, 0), (0, 0), (0, LANES - w_route.shape[-1])))
    wr_hi = w_route.astype(BF16)
    wr_lo = (w_route - wr_hi.astype(F32)).astype(BF16)
    w_route_split = jnp.concatenate([wr_hi, wr_lo], axis=-1)
    gate_b = jnp.pad(mlstm_gate_b.reshape(depth, 1, 4 * N_HEADS), ((0, 0), (0, 0), (0, LANES - 4 * N_HEADS)))
    na_bias = _na_bias_tables(na_rpb)

    moe = None
    for layer in range(depth):
        mod = mod_all[layer]
        if moe is None:
            ret, ml, gates, na, bg = _in_projection(x_all, mod, norm1_g, w_packed, layer, n_batch)
        else:
            x_all, ret, ml, gates, na, bg = _in_projection(x_all, mod, norm1_g, w_packed, layer, n_batch, moe)
        r_out = _retention(ret, ret_decay[layer], ret_norm_g[layer], cos, sin, n_batch)
        m_out = _mlstm(ml, gates, gate_b[layer], conv_w[layer], conv_b[layer].reshape(1, 2 * WIDTH),
                       mlstm_norm_g[layer], n_batch)
        a_out = _neighbourhood_attention(na, na_bias, layer, n_batch)
        x_all, h2, logits = _merge(r_out, m_out, a_out, bg, x_all, mod, w_branch_b, w_out_b, norm2_g,
                                   w_route_split, layer, n_batch)
        route, counts = _route(logits)
        plan = _dispatch_plan(route, counts)
        slots = _experts(h2, plan, w_expert_gate, w_expert_up, w_expert_down, layer)
        moe = (slots, route, mod)

    return _final_combine(slots, x_all, route, mod, final_norm_g, n_batch).reshape(n_batch, SEQ, d)
```

```python
import functools

import numpy as np
import jax
import jax.numpy as jnp
from jax import lax
from jax.experimental import pallas as pl
from jax.experimental.pallas import tpu as pltpu

F32 = jnp.float32
BF16 = jnp.bfloat16

D_MODEL = 1024
SEQ = 2048
CTX_LEN = 256
TOK = CTX_LEN + SEQ
GRID_W = 64
GRID_ROWS = SEQ // GRID_W
HEAD_DIM = 128
N_HEADS = 4
WIDTH = N_HEADS * HEAD_DIM
CHUNK = 256
N_CHUNKS = TOK // CHUNK
CTX_CHUNKS = CTX_LEN // CHUNK
CONV_WIDTH = 5
NA_WIN_ROWS = 8
NA_WIN_COLS = 16
NA_Q_ROWS = 4
NA_Q = NA_Q_ROWS * GRID_W
NA_K_ROWS = NA_Q_ROWS + NA_WIN_ROWS
NA_K = NA_K_ROWS * GRID_W
NA_STEPS = SEQ // NA_Q
ROPE_BASE = 10000.0
N_GROUPS = 4
EXPERTS_PER_GROUP = 8
N_EXPERTS = N_GROUPS * EXPERTS_PER_GROUP
EXPERT_FF = 512
EXPERT_BLOCK = 256
EPS = 1e-6
NEG_INF = -1e30
QK_SCALE = HEAD_DIM ** -0.5

TM = 256
ROUTE_TM = 512
LANES = 128
VMEM_LIMIT = 56 * 1024 * 1024


def _dot(a, b):
    return jnp.dot(a, b, preferred_element_type=F32)


def _dot_nt(a, b):
    return lax.dot_general(a, b, (((1,), (1,)), ((), ())), preferred_element_type=F32)


def _bdot(a, b):
    return _dot(a.astype(BF16), b.astype(BF16))


def _bdot_nt(a, b):
    return _dot_nt(a.astype(BF16), b.astype(BF16))


def _bdot_tn(a, b):
    return _dot(a.T.astype(BF16), b.astype(BF16))


def _log_sigmoid(x):
    return jnp.minimum(x, 0.0) - jnp.log1p(jnp.exp(-jnp.abs(x)))


def _silu(x):
    return x * jax.nn.sigmoid(x)


def _params(sem, vmem=None):
    return pltpu.CompilerParams(dimension_semantics=sem, vmem_limit_bytes=vmem)


def _mod_row(n_batch):
    tiles = TOK // TM

    def index(i):
        return (jnp.where(i % tiles == 0, n_batch, i // tiles), 0, 0)

    return index


def _mod_kernel(c_ref, w_ref, b_ref, o_ref):
    cond = _silu(c_ref[...])
    o_ref[0] = _bdot(cond, w_ref[0]) + b_ref[0]


def _modulation(cond, w_mod, b_mod):
    depth, d, cols = w_mod.shape
    rows = cond.shape[0]
    tn = 1536
    return pl.pallas_call(
        _mod_kernel,
        grid=(depth, cols // tn),
        in_specs=[
            pl.BlockSpec((rows, d), lambda l, j: (0, 0)),
            pl.BlockSpec((1, d, tn), lambda l, j: (l, 0, j)),
            pl.BlockSpec((1, 1, tn), lambda l, j: (l, 0, j)),
        ],
        out_specs=pl.BlockSpec((1, rows, tn), lambda l, j: (l, 0, j)),
        out_shape=jax.ShapeDtypeStruct((depth, rows, cols), F32),
        compiler_params=_params(("parallel", "parallel")),
        name="modulation",
    )(cond, w_mod, b_mod.reshape(depth, 1, cols))


def _rms_modulate(x, g, shift, scale):
    y = x * lax.rsqrt(jnp.mean(x * x, axis=-1, keepdims=True) + EPS) * g
    return y * (1.0 + scale) + shift


IN_WIDTHS = (4 * WIDTH, 4 * WIDTH, LANES, 3 * WIDTH, 3 * D_MODEL)
IN_OFFSETS = tuple(int(v) for v in np.cumsum((0,) + IN_WIDTHS))
IN_DTYPES = (F32, F32, F32, F32, BF16)


def _pack_kernel(w_ref, o_ref):
    gate0 = IN_OFFSETS[2]
    n_gate = 4 * N_HEADS
    o_ref[0, :, :gate0] = w_ref[0, :, :gate0].astype(BF16)
    window = w_ref[0, :, gate0:gate0 + LANES]
    lane = lax.broadcasted_iota(jnp.int32, window.shape, 1)
    o_ref[0, :, gate0:gate0 + LANES] = jnp.where(lane < n_gate, window, 0.0).astype(BF16)
    o_ref[0, :, gate0 + LANES:] = w_ref[0, :, gate0 + n_gate:].astype(BF16)


def _pack_in_weights(w_in):
    depth, d, cols = w_in.shape
    rows = 256
    return pl.pallas_call(
        _pack_kernel,
        grid=(depth, d // rows),
        in_specs=[pl.BlockSpec((1, rows, cols), lambda l, r: (l, r, 0))],
        out_specs=pl.BlockSpec((1, rows, IN_OFFSETS[-1]), lambda l, r: (l, r, 0)),
        out_shape=jax.ShapeDtypeStruct((depth, d, IN_OFFSETS[-1]), BF16),
        compiler_params=_params(("parallel", "parallel"), VMEM_LIMIT),
        name="pack_in_weights",
    )(w_in)


def _project(x, mod_ref, g_ref, w_ref, out_refs):
    h = _rms_modulate(x, g_ref[0], mod_ref[0, 0:1, :], mod_ref[0, 1:2, :]).astype(BF16)
    for o_ref, lo, hi in zip(out_refs, IN_OFFSETS[:-1], IN_OFFSETS[1:]):
        o_ref[...] = _dot(h, w_ref[0, :, lo:hi]).astype(o_ref.dtype)


def _inproj_kernel(x_ref, mod_ref, g_ref, w_ref, *out_refs):
    _project(x_ref[...], mod_ref, g_ref, w_ref, out_refs)


def _combine_inproj_kernel(slots_ref, x_ref, route_ref, prev_mod_ref, mod_ref, g_ref, w_ref, x_out_ref, *out_refs):
    _moe_residual(slots_ref, x_ref, route_ref, prev_mod_ref, x_out_ref)
    _project(x_out_ref[...], mod_ref, g_ref, w_ref, out_refs)


def _in_projection(x_all, mod, norm_g, w_packed, layer, n_batch, moe=None):
    n = x_all.shape[0]
    row = lambda i: (i, 0)
    in_specs = [
        pl.BlockSpec((TM, D_MODEL), row),
        pl.BlockSpec((1, 6, D_MODEL), _mod_row(n_batch)),
        pl.BlockSpec((1, 1, D_MODEL), lambda i: (layer, 0, 0)),
        pl.BlockSpec((1,) + w_packed.shape[1:], lambda i: (layer, 0, 0), pipeline_mode=pl.Buffered(1)),
    ]
    out_specs = [pl.BlockSpec((TM, w), row) for w in IN_WIDTHS]
    out_shape = [jax.ShapeDtypeStruct((n, w), dt) for w, dt in zip(IN_WIDTHS, IN_DTYPES)]
    args = (x_all, mod, norm_g.reshape(-1, 1, D_MODEL), w_packed)
    if moe is None:
        body = _inproj_kernel
    else:
        slots, route, prev_mod = moe
        body = _combine_inproj_kernel
        in_specs = ([pl.BlockSpec((TM * 2 * ROW_TILE, LANES), row), in_specs[0], pl.BlockSpec((TM, LANES), row),
                     pl.BlockSpec((1, 6, D_MODEL), _mod_row(n_batch))] + in_specs[1:])
        out_specs = [pl.BlockSpec((TM, D_MODEL), row)] + out_specs
        out_shape = [jax.ShapeDtypeStruct((n, D_MODEL), F32)] + out_shape
        args = (slots, x_all, route, prev_mod) + args[1:]
    return pl.pallas_call(
        body,
        grid=(n // TM,),
        in_specs=in_specs,
        out_specs=out_specs,
        out_shape=out_shape,
        compiler_params=_params(("parallel",), VMEM_LIMIT),
        name="in_projection",
    )(*args)


def _chunk_order(t):
    fwd = t
    bwd = jnp.where(t < CTX_CHUNKS, CTX_CHUNKS - 1 - t, N_CHUNKS + CTX_CHUNKS - 1 - t)
    return fwd, bwd


def _chunk_slice(c):
    return pl.ds(pl.multiple_of(c * CHUNK, CHUNK), CHUNK)


def _head_norm(y, gain):
    mu = jnp.mean(y, axis=-1, keepdims=True)
    yc = y - mu
    var = jnp.mean(yc * yc, axis=-1, keepdims=True)
    return yc * lax.rsqrt(var + EPS) * gain


ROW_TILE = D_MODEL // LANES


def _store_token_tiles(ref, x, pitch, offset=0):
    rows = x.shape[0]
    for c in range(ROW_TILE):
        ref[pl.ds(offset + c, rows, stride=pitch), :] = x[:, c * LANES:(c + 1) * LANES]


def _load_token_slab(ref, rows, pitch, c, offset=0):
    return ref[pl.ds(offset + c, rows, stride=pitch), :]


def _moe_residual(slots_ref, x_ref, route_ref, mod_ref, o_ref):
    route = route_ref[...]
    w1 = route[:, ROUTE_W1:ROUTE_W1 + 1]
    w2 = route[:, ROUTE_W2:ROUTE_W2 + 1]
    rows = x_ref.shape[0]
    sq = jnp.zeros((rows, 1), F32)
    for c in range(ROW_TILE):
        lanes = slice(c * LANES, (c + 1) * LANES)
        y1 = _load_token_slab(slots_ref, rows, 2 * ROW_TILE, c)
        y2 = _load_token_slab(slots_ref, rows, 2 * ROW_TILE, c, offset=ROW_TILE)
        x_new = x_ref[:, lanes] + mod_ref[0, 5:6, lanes] * (y1 * w1 + y2 * w2)
        o_ref[:, lanes] = x_new
        sq = sq + jnp.sum(x_new * x_new, axis=-1, keepdims=True)
    return sq


def _select_lane(x, lane, idx):
    return jnp.sum(jnp.where(lane == idx, x, 0.0), axis=-1, keepdims=True)


HEADS_PER_STEP = 2
HEAD_STEPS = N_HEADS // HEADS_PER_STEP
STEP_WIDTH = HEADS_PER_STEP * HEAD_DIM
SCAN_VMEM = 48 * 1024 * 1024


def _head_lanes(hh):
    return slice(hh * HEAD_DIM, (hh + 1) * HEAD_DIM)


def _retention_kernel(dec_ref, q_ref, k_ref, v_ref, g_ref, cos_ref, sin_ref, gn_ref, o_ref,
                      qs_ref, ks_ref, of_ref, ob_ref, intra_ref, st_ref):
    first_head = pl.program_id(1) * HEADS_PER_STEP

    lane = lax.broadcasted_iota(jnp.int32, (CHUNK, HEAD_DIM), 1)
    first_half = (lane % (HEAD_DIM // 2)) < (HEAD_DIM // 4)

    def rope(x, cos, sin):
        rot = jnp.where(first_half, pltpu.roll(x, HEAD_DIM - HEAD_DIM // 4, 1), pltpu.roll(x, HEAD_DIM // 4, 1))
        return x * cos + rot * sin

    qs_ref[0:CTX_LEN, :] = q_ref[0:CTX_LEN, :]
    ks_ref[0:CTX_LEN, :] = k_ref[0:CTX_LEN, :] * QK_SCALE

    def rotate_chunk(c, carry):
        rows = _chunk_slice(c)
        pos = _chunk_slice(c - CTX_CHUNKS)
        cos = cos_ref[pos, :]
        sin = sin_ref[pos, :]
        for hh in range(HEADS_PER_STEP):
            qs_ref[rows, _head_lanes(hh)] = rope(q_ref[rows, _head_lanes(hh)], cos, sin)
            ks_ref[rows, _head_lanes(hh)] = rope(k_ref[rows, _head_lanes(hh)], cos, sin) * QK_SCALE
        return carry

    lax.fori_loop(CTX_CHUNKS, N_CHUNKS, rotate_chunk, 0)

    ii = lax.broadcasted_iota(jnp.int32, (CHUNK, CHUNK), 0).astype(F32)
    jj = lax.broadcasted_iota(jnp.int32, (CHUNK, CHUNK), 1).astype(F32)
    col = lax.broadcasted_iota(jnp.int32, (CHUNK, 1), 0).astype(F32)

    def decay_mat(dist, lg_dir):
        ok = dist >= 0
        return jnp.where(ok, jnp.exp(jnp.where(ok, dist, 0.0) * lg_dir), 0.0)

    dec = _log_sigmoid(dec_ref[...])
    hl = lax.broadcasted_iota(jnp.int32, dec.shape, 1)
    consts = []
    for hh in range(HEADS_PER_STEP):
        lg = jnp.sum(jnp.where(hl == first_head + hh, dec, 0.0), axis=-1, keepdims=True)
        lg_f, lg_b = lg[0:1, :], lg[1:2, :]
        intra_ref[2 * hh] = decay_mat(ii - jj, lg_f)
        intra_ref[2 * hh + 1] = decay_mat(jj - ii, lg_b)
        consts.append(dict(
            q_decay=(jnp.exp((col + 1.0) * lg_f), jnp.exp((CHUNK - col) * lg_b)),
            k_decay=(jnp.exp((CHUNK - 1.0 - col) * lg_f), jnp.exp(col * lg_b)),
            chunk_decay=(jnp.exp(CHUNK * lg_f), jnp.exp(CHUNK * lg_b))))

    st_ref[...] = jnp.zeros_like(st_ref)
    out_refs = (of_ref, ob_ref)

    def step(t, carry):
        for hh in range(HEADS_PER_STEP):
            cst = consts[hh]
            for d, c in enumerate(_chunk_order(t)):
                rows = _chunk_slice(c)
                q = qs_ref[rows, _head_lanes(hh)]
                k = ks_ref[rows, _head_lanes(hh)]
                v = v_ref[rows, _head_lanes(hh)]
                s_prev = st_ref[2 * hh + d]
                scores = _bdot_nt(q, k) * intra_ref[2 * hh + d]
                out_refs[d][rows, _head_lanes(hh)] = _bdot(scores, v) + _bdot(q * cst["q_decay"][d], s_prev)
                st_ref[2 * hh + d] = s_prev * cst["chunk_decay"][d] + _bdot_tn(k * cst["k_decay"][d], v)
        return carry

    lax.fori_loop(0, N_CHUNKS, step, 0)

    def finish_chunk(c, carry):
        rows = _chunk_slice(c)
        for hh in range(HEADS_PER_STEP):
            lanes = _head_lanes(hh)
            y = of_ref[rows, lanes] + ob_ref[rows, lanes]
            o_ref[rows, lanes] = (_head_norm(y, gn_ref[:, lanes]) * _silu(g_ref[rows, lanes])).astype(o_ref.dtype)
        return carry

    lax.fori_loop(0, N_CHUNKS, finish_chunk, 0)


def _retention(ret, ret_decay, norm_g, cos, sin, n_batch):
    n = ret.shape[0]

    def head_block(offset):
        return pl.BlockSpec((TOK, STEP_WIDTH), lambda b, h: (b, offset * HEAD_STEPS + h))

    return pl.pallas_call(
        _retention_kernel,
        grid=(n_batch, HEAD_STEPS),
        in_specs=[
            pl.BlockSpec(ret_decay.shape, lambda b, h: (0, 0)),
            head_block(0), head_block(1), head_block(2), head_block(3),
            pl.BlockSpec((SEQ, HEAD_DIM), lambda b, h: (0, 0)),
            pl.BlockSpec((SEQ, HEAD_DIM), lambda b, h: (0, 0)),
            pl.BlockSpec((1, STEP_WIDTH), lambda b, h: (0, h)),
        ],
        out_specs=pl.BlockSpec((TOK, STEP_WIDTH), lambda b, h: (b, h)),
        out_shape=jax.ShapeDtypeStruct((n, WIDTH), BF16),
        scratch_shapes=[
            pltpu.VMEM((TOK, STEP_WIDTH), F32),
            pltpu.VMEM((TOK, STEP_WIDTH), F32),
            pltpu.VMEM((TOK, STEP_WIDTH), F32),
            pltpu.VMEM((TOK, STEP_WIDTH), F32),
            pltpu.VMEM((2 * HEADS_PER_STEP, CHUNK, CHUNK), F32),
            pltpu.VMEM((2 * HEADS_PER_STEP, HEAD_DIM, HEAD_DIM), F32),
        ],
        compiler_params=_params(("parallel", "parallel"), SCAN_VMEM),
        name="retention",
    )(ret_decay, ret, ret, ret, ret, cos, sin, norm_g.reshape(1, WIDTH))


CONV_PAD = 8
CONV_ROWS = 128


def _mlstm_kernel(q_ref, k_ref, v_ref, og_ref, gate_ref, gb_ref, cwq_ref, cwk_ref, cbq_ref, cbk_ref,
                  gn_ref, o_ref, qs_ref, ks_ref, of_ref, ob_ref, pad_ref, gx_ref, gxt_ref, cst_ref):
    first_head = pl.program_id(1) * HEADS_PER_STEP

    pad_ref[0:CONV_PAD, :] = jnp.zeros((CONV_PAD, STEP_WIDTH), F32)
    pad_ref[CONV_PAD + TOK:, :] = jnp.zeros((CONV_PAD, STEP_WIDTH), F32)
    crow = lax.broadcasted_iota(jnp.int32, (CONV_ROWS, 1), 0)

    def conv(u_ref, w_ref, b_ref, dst_ref, scale):
        def fill(c, carry):
            rows = _chunk_slice(c)
            pad_ref[pl.ds(pl.multiple_of(c * CHUNK + CONV_PAD, CONV_PAD), CHUNK), :] = u_ref[rows, :]
            return carry

        lax.fori_loop(0, N_CHUNKS, fill, 0)

        for blk in range(TOK // CONV_ROWS):
            first = blk * CONV_ROWS
            acc = jnp.zeros((CONV_ROWS, STEP_WIDTH), F32)
            for j in range(CONV_WIDTH):
                shift = j - CONV_WIDTH // 2
                tap = pad_ref[first + CONV_PAD + shift:first + CONV_PAD + shift + CONV_ROWS, :]
                if (first + CONV_ROWS == CTX_LEN and shift > 0) or (first == CTX_LEN and shift < 0):
                    trow = crow + first
                    tap = jnp.where((trow < CTX_LEN) == (trow + shift < CTX_LEN), tap, 0.0)
                acc = acc + tap * w_ref[j:j + 1, :]
            dst_ref[first:first + CONV_ROWS, :] = _silu(acc + b_ref[...]) * scale

    conv(q_ref, cwq_ref, cbq_ref, qs_ref, 1.0)
    conv(k_ref, cwk_ref, cbk_ref, ks_ref, QK_SCALE)

    clane = lax.broadcasted_iota(jnp.int32, (CHUNK, LANES), 1)
    is_forget = ((clane // N_HEADS) % 2) == 1

    def gate_chunk(c, carry):
        rows = _chunk_slice(c)
        g = gate_ref[rows, :] + gb_ref[...]
        gx = jnp.where(is_forget, _log_sigmoid(g), g)
        gx_ref[rows, :] = gx
        gxt_ref[c] = gx.T
        return carry

    lax.fori_loop(0, N_CHUNKS, gate_chunk, 0)

    ii = lax.broadcasted_iota(jnp.int32, (CHUNK, CHUNK), 0)
    jj = lax.broadcasted_iota(jnp.int32, (CHUNK, CHUNK), 1)
    causal = (jj <= ii, jj >= ii)

    cst_ref[...] = jnp.zeros_like(cst_ref)
    out_refs = (of_ref, ob_ref)

    def step(t, carry):
        new_carry = []
        for hh in range(HEADS_PER_STEP):
            head = first_head + hh
            lanes = _head_lanes(hh)
            for d, c in enumerate(_chunk_order(t)):
                slot = 2 * hh + d
                n_prev, m_prev = carry[2 * slot], carry[2 * slot + 1]
                rows = _chunk_slice(c)
                q = qs_ref[rows, lanes]
                k = ks_ref[rows, lanes]
                v = v_ref[rows, lanes]
                gc = gx_ref[rows, :]
                i_col = _select_lane(gc, clane, 2 * d * N_HEADS + head)
                f_col = _select_lane(gc, clane, (2 * d + 1) * N_HEADS + head)
                i_row = gxt_ref[c, pl.ds(2 * d * N_HEADS + head, 1), :]
                f_row = gxt_ref[c, pl.ds((2 * d + 1) * N_HEADS + head, 1), :]
                vis = causal[d]
                cum_col = jnp.sum(jnp.where(vis, f_row, 0.0), axis=1, keepdims=True)
                cum_row = jnp.sum(jnp.where(causal[1 - d], f_col, 0.0), axis=0, keepdims=True)
                total = jnp.sum(f_row, axis=1, keepdims=True)
                c_prev = cst_ref[slot]

                log_kw = total - cum_col + i_col
                m_new = jnp.maximum(total + m_prev, jnp.max(log_kw, axis=0, keepdims=True))
                kw = jnp.exp(log_kw - m_new)
                pw = jnp.exp(total + m_prev - m_new)
                cst_ref[slot] = pw * c_prev + _bdot_tn(k * kw, v)
                n_new = pw * n_prev + jnp.sum(kw * k, axis=0, keepdims=True)

                log_w = jnp.where(vis, cum_col - cum_row + i_row, -jnp.inf)
                log_p = cum_col + m_prev
                m_t = jnp.maximum(log_p, jnp.max(log_w, axis=1, keepdims=True))
                w = jnp.exp(log_w - m_t)
                p = jnp.exp(log_p - m_t)
                qk = _bdot_nt(q, k) * w
                num = _bdot(qk, v) + p * _bdot(q, c_prev)
                den = jnp.sum(qk, axis=1, keepdims=True) + p * jnp.sum(q * n_prev, axis=1, keepdims=True)
                out_refs[d][rows, lanes] = num / jnp.maximum(jnp.abs(den), jnp.exp(-m_t))
                new_carry += [n_new, m_new]
        return tuple(new_carry)

    zero_n = jnp.zeros((1, HEAD_DIM), F32)
    zero_m = jnp.zeros((1, 1), F32)
    lax.fori_loop(0, N_CHUNKS, step, (zero_n, zero_m) * (2 * HEADS_PER_STEP))

    def finish_chunk(c, carry):
        rows = _chunk_slice(c)
        for hh in range(HEADS_PER_STEP):
            lanes = _head_lanes(hh)
            gated = (of_ref[rows, lanes] + ob_ref[rows, lanes]) * jax.nn.sigmoid(og_ref[rows, lanes])
            o_ref[rows, lanes] = _head_norm(gated, gn_ref[:, lanes]).astype(o_ref.dtype)
        return carry

    lax.fori_loop(0, N_CHUNKS, finish_chunk, 0)


def _mlstm(ml, gates, gate_b, conv_w, conv_b, norm_g, n_batch):
    n = ml.shape[0]

    def head_block(offset):
        return pl.BlockSpec((TOK, STEP_WIDTH), lambda b, h: (b, offset * HEAD_STEPS + h))

    def head_cols(rows, offset):
        return pl.BlockSpec((rows, STEP_WIDTH), lambda b, h: (0, offset * HEAD_STEPS + h))

    return pl.pallas_call(
        _mlstm_kernel,
        grid=(n_batch, HEAD_STEPS),
        in_specs=[
            head_block(0), head_block(1), head_block(2), head_block(3),
            pl.BlockSpec((TOK, LANES), lambda b, h: (b, 0)),
            pl.BlockSpec((1, LANES), lambda b, h: (0, 0)),
            head_cols(CONV_WIDTH, 0), head_cols(CONV_WIDTH, 1),
            head_cols(1, 0), head_cols(1, 1),
            head_cols(1, 0),
        ],
        out_specs=pl.BlockSpec((TOK, STEP_WIDTH), lambda b, h: (b, h)),
        out_shape=jax.ShapeDtypeStruct((n, WIDTH), BF16),
        scratch_shapes=[
            pltpu.VMEM((TOK, STEP_WIDTH), F32),
            pltpu.VMEM((TOK, STEP_WIDTH), F32),
            pltpu.VMEM((TOK, STEP_WIDTH), F32),
            pltpu.VMEM((TOK, STEP_WIDTH), F32),
            pltpu.VMEM((TOK + 2 * CONV_PAD, STEP_WIDTH), F32),
            pltpu.VMEM((TOK, LANES), F32),
            pltpu.VMEM((N_CHUNKS, LANES, CHUNK), F32),
            pltpu.VMEM((2 * HEADS_PER_STEP, HEAD_DIM, HEAD_DIM), F32),
        ],
        compiler_params=_params(("parallel", "parallel"), SCAN_VMEM),
        name="mlstm",
    )(ml, ml, ml, ml, gates, gate_b, conv_w, conv_w, conv_b, conv_b, norm_g.reshape(1, WIDTH))


def _na_kernel(q_ref, k_ref, v_ref, bias_ref, o_ref):
    step = pl.program_id(2)

    def context_scores(hh):
        lanes = _head_lanes(hh)
        q = q_ref[:, lanes].astype(BF16)
        v_ctx = v_ref[0:CTX_LEN, lanes].astype(BF16)
        return q, v_ctx, _dot_nt(q, k_ref[0:CTX_LEN, lanes].astype(BF16)) * QK_SCALE

    @pl.when(step == 0)
    def _():
        for hh in range(HEADS_PER_STEP):
            _, v_ctx, s_ctx = context_scores(hh)
            m = jnp.max(s_ctx, axis=-1, keepdims=True)
            p = jnp.exp(s_ctx - m)
            o = _dot(p.astype(BF16), v_ctx) / jnp.sum(p, axis=-1, keepdims=True)
            o_ref[:, _head_lanes(hh)] = o.astype(o_ref.dtype)

    @pl.when(step > 0)
    def _():
        start = CTX_LEN + NA_Q * jnp.clip(step - 2, 0, NA_STEPS - NA_K_ROWS // NA_Q_ROWS)
        rows = pl.ds(pl.multiple_of(start, NA_Q), NA_K)
        for hh in range(HEADS_PER_STEP):
            lanes = _head_lanes(hh)
            q, v_ctx, s_ctx = context_scores(hh)
            s_loc = _dot_nt(q, k_ref[rows, lanes].astype(BF16)) * QK_SCALE + bias_ref[0, hh, 0]
            m = jnp.maximum(jnp.max(s_loc, axis=-1, keepdims=True), jnp.max(s_ctx, axis=-1, keepdims=True))
            p_loc = jnp.exp(s_loc - m)
            p_ctx = jnp.exp(s_ctx - m)
            denom = jnp.sum(p_loc, axis=-1, keepdims=True) + jnp.sum(p_ctx, axis=-1, keepdims=True)
            o = _dot(p_loc.astype(BF16), v_ref[rows, lanes].astype(BF16)) + _dot(p_ctx.astype(BF16), v_ctx)
            o_ref[:, lanes] = (o / denom).astype(o_ref.dtype)


def _na_row_case(case, a):
    t = np.arange(NA_K_ROWS)
    last_start = NA_K_ROWS - NA_WIN_ROWS
    return [
        (t < NA_WIN_ROWS, NA_WIN_ROWS - 1 - a),
        ((t >= a) & (t < a + NA_WIN_ROWS), NA_WIN_ROWS // 2 - 1 - a),
        (t >= last_start, NA_Q_ROWS - NA_K_ROWS + NA_WIN_ROWS - 1 - a),
    ][case]


def _na_expand_kernel(slab_ref, o_ref):
    masked = jnp.full((GRID_W, GRID_W), NEG_INF, F32)
    for case in range(3):
        for a in range(NA_Q_ROWS):
            row_ok, first = _na_row_case(case, a)
            for pair in range(NA_K_ROWS // 2):
                tiles = [slab_ref[0, first + t] if row_ok[t] else masked for t in (2 * pair, 2 * pair + 1)]
                o_ref[0, case, a * GRID_W:(a + 1) * GRID_W, pair * LANES:(pair + 1) * LANES] = (
                    jnp.concatenate(tiles, axis=1))


def _na_bias_tables(rpb):
    lead = rpb.shape[:-2]
    n_row_off, n_col_off = rpb.shape[-2:]
    qc = np.arange(GRID_W)[:, None]
    kc = np.arange(GRID_W)[None, :]
    col_start = np.clip(qc - NA_WIN_COLS // 2, 0, GRID_W - NA_WIN_COLS)
    col_ok = (kc >= col_start) & (kc < col_start + NA_WIN_COLS)
    col_idx = np.clip(kc - qc + NA_WIN_COLS - 1, 0, n_col_off - 1)
    onehot = ((col_idx[None] == np.arange(n_col_off)[:, None, None]) & col_ok[None]).astype(np.float32)
    slabs = jnp.einsum('...rc,cqk->...rqk', rpb.astype(F32), onehot, precision=lax.Precision.HIGHEST)
    slabs = jnp.where(col_ok, slabs, NEG_INF).reshape(-1, n_row_off, GRID_W, GRID_W)
    n_tables = slabs.shape[0]
    tables = pl.pallas_call(
        _na_expand_kernel,
        grid=(n_tables,),
        in_specs=[pl.BlockSpec((1, n_row_off, GRID_W, GRID_W), lambda i: (i, 0, 0, 0))],
        out_specs=pl.BlockSpec((1, 3, NA_Q, NA_K), lambda i: (i, 0, 0, 0)),
        out_shape=jax.ShapeDtypeStruct((n_tables, 3, NA_Q, NA_K), F32),
        compiler_params=_params(("parallel",)),
        name="na_bias_tables",
    )(slabs)
    return tables.reshape(*lead, 3, NA_Q, NA_K)


def _neighbourhood_attention(na, bias, layer, n_batch):
    n = na.shape[0]
    steps = 1 + NA_STEPS
    tiles = TOK // NA_Q

    def table(b, h, j):
        return (layer, h, jnp.where(j <= 1, 0, jnp.where(j == NA_STEPS, 2, 1)), 0, 0)

    return pl.pallas_call(
        _na_kernel,
        grid=(n_batch, HEAD_STEPS, steps),
        in_specs=[
            pl.BlockSpec((NA_Q, STEP_WIDTH), lambda b, h, j: (b * tiles + j, h)),
            pl.BlockSpec((TOK, STEP_WIDTH), lambda b, h, j: (b, HEAD_STEPS + h)),
            pl.BlockSpec((TOK, STEP_WIDTH), lambda b, h, j: (b, 2 * HEAD_STEPS + h)),
            pl.BlockSpec((1, HEADS_PER_STEP, 1, NA_Q, NA_K), table),
        ],
        out_specs=pl.BlockSpec((NA_Q, STEP_WIDTH), lambda b, h, j: (b * tiles + j, h)),
        out_shape=jax.ShapeDtypeStruct((n, WIDTH), BF16),
        compiler_params=_params(("parallel", "parallel", "arbitrary")),
        name="neighbourhood_attention",
    )(na, na, na, bias)


def _merge_kernel(r_ref, m_ref, a_ref, bg_ref, x_ref, mod_ref, wb_ref, wo_ref, g2_ref, wr_ref,
                  x_out_ref, h2_ref, logit_ref):
    gate = jax.nn.sigmoid(bg_ref[...].astype(F32))
    mix = (gate[:, 0:D_MODEL] * _dot(r_ref[...], wb_ref[0, 0])
           + gate[:, D_MODEL:2 * D_MODEL] * _dot(m_ref[...], wb_ref[0, 1])
           + gate[:, 2 * D_MODEL:] * _dot(a_ref[...], wb_ref[0, 2]))
    y = _dot(mix.astype(BF16), wo_ref[0])
    x_new = x_ref[...] + mod_ref[0, 2:3, :] * y
    x_out_ref[...] = x_new
    h2 = _rms_modulate(x_new, g2_ref[0], mod_ref[0, 3:4, :], mod_ref[0, 4:5, :])
    _store_token_tiles(h2_ref, h2, ROW_TILE)
    hi = h2.astype(BF16)
    lo = (h2 - hi.astype(F32)).astype(BF16)
    hi_terms = _dot(hi, wr_ref[0])
    logit_ref[...] = hi_terms[:, :LANES] + (_dot(lo, wr_ref[0, :, :LANES]) + hi_terms[:, LANES:])


def _merge(r, m, a, bg, x_all, mod, w_branch, w_out, norm2_g, w_route, layer, n_batch, latent_only=False):
    tiles = TOK // TM
    if latent_only:
        lat_tiles = SEQ // TM
        n = n_batch * SEQ
        row = lambda i: ((i // lat_tiles) * tiles + CTX_LEN // TM + i % lat_tiles, 0)
        mod_row = lambda i: (i // lat_tiles, 0, 0)
    else:
        n = x_all.shape[0]
        row = lambda i: (i, 0)
        mod_row = _mod_row(n_batch)
    out_row = lambda i: (i, 0)

    def layer_block(w):
        return pl.BlockSpec((1,) + w.shape[1:], lambda i: (layer,) + (0,) * (w.ndim - 1))

    return pl.pallas_call(
        _merge_kernel,
        grid=(n // TM,),
        in_specs=[
            pl.BlockSpec((TM, WIDTH), row), pl.BlockSpec((TM, WIDTH), row), pl.BlockSpec((TM, WIDTH), row),
            pl.BlockSpec((TM, 3 * D_MODEL), row),
            pl.BlockSpec((TM, D_MODEL), row),
            pl.BlockSpec((1, 6, D_MODEL), mod_row),
            layer_block(w_branch), layer_block(w_out),
            pl.BlockSpec((1, 1, D_MODEL), lambda i: (layer, 0, 0)),
            layer_block(w_route),
        ],
        out_specs=[pl.BlockSpec((TM, D_MODEL), out_row), pl.BlockSpec((TM * ROW_TILE, LANES), out_row),
                   pl.BlockSpec((TM, LANES), out_row)],
        out_shape=[jax.ShapeDtypeStruct((n, D_MODEL), F32), jax.ShapeDtypeStruct((n * ROW_TILE, LANES), F32),
                   jax.ShapeDtypeStruct((n, LANES), F32)],
        compiler_params=_params(("parallel",), VMEM_LIMIT),
        name="merge",
    )(r, m, a, bg, x_all, mod, w_branch, w_out, norm2_g.reshape(-1, 1, D_MODEL), w_route)


ROUTE_E1, ROUTE_E2, ROUTE_RANK1, ROUTE_RANK2, ROUTE_W1, ROUTE_W2 = range(6)


def _route_kernel(logit_ref, route_ref, count_ref, cnt_ref):
    @pl.when(pl.program_id(0) == 0)
    def _():
        cnt_ref[...] = jnp.zeros_like(cnt_ref)

    lg = logit_ref[...]
    lane = lax.broadcasted_iota(jnp.int32, lg.shape, 1)
    lane_f = lane.astype(F32)

    def first_argmax(vals):
        top = jnp.max(vals, axis=-1, keepdims=True)
        idx = jnp.min(jnp.where(vals == top, lane_f, float(LANES)), axis=-1, keepdims=True)
        return top, idx

    group_logits = jnp.where(lane < N_GROUPS, lg, -jnp.inf)
    g_top, g_idx = first_argmax(group_logits)
    group_w = 1.0 / jnp.sum(jnp.exp(group_logits - g_top), axis=-1, keepdims=True)

    first = N_GROUPS + EXPERTS_PER_GROUP * g_idx
    in_group = (lane_f >= first) & (lane_f < first + EXPERTS_PER_GROUP)
    expert_logits = jnp.where(in_group, lg, -jnp.inf)
    v1, i1 = first_argmax(expert_logits)
    v2, i2 = first_argmax(jnp.where(lane_f == i1, -jnp.inf, expert_logits))
    t = jnp.exp(v2 - v1)
    w1 = group_w / (1.0 + t)
    w2 = group_w * t / (1.0 + t)

    oh1 = (lane_f == i1).astype(F32)
    oh2 = (lane_f == i2).astype(F32)
    both = oh1 + oh2
    rows = lg.shape[0]
    earlier = (lax.broadcasted_iota(jnp.int32, (rows, rows), 1)
               < lax.broadcasted_iota(jnp.int32, (rows, rows), 0)).astype(BF16)
    before = _dot(earlier, both.astype(BF16)) + cnt_ref[...]
    rank1 = jnp.sum(oh1 * before, axis=-1, keepdims=True)
    rank2 = jnp.sum(oh2 * before, axis=-1, keepdims=True)
    cnt_ref[...] += jnp.sum(both, axis=0, keepdims=True)
    count_ref[...] = jnp.broadcast_to(cnt_ref[...], count_ref.shape)

    out = jnp.zeros_like(lg)
    for slot, val in ((ROUTE_E1, i1 - N_GROUPS), (ROUTE_E2, i2 - N_GROUPS), (ROUTE_RANK1, rank1),
                      (ROUTE_RANK2, rank2), (ROUTE_W1, w1), (ROUTE_W2, w2)):
        out = jnp.where(lane == slot, val, out)
    route_ref[...] = out


def _route(logits):
    n = logits.shape[0]
    return pl.pallas_call(
        _route_kernel,
        grid=(n // ROUTE_TM,),
        in_specs=[pl.BlockSpec((ROUTE_TM, LANES), lambda i: (i, 0))],
        out_specs=[pl.BlockSpec((ROUTE_TM, LANES), lambda i: (i, 0)), pl.BlockSpec((8, LANES), lambda i: (0, 0))],
        out_shape=[jax.ShapeDtypeStruct((n, LANES), F32), jax.ShapeDtypeStruct((8, LANES), F32)],
        scratch_shapes=[pltpu.VMEM((1, LANES), F32)],
        compiler_params=_params(("arbitrary",)),
        name="route",
    )(logits)


GROUP = 8


INVERT_TILE = 1024


def _invert_kernel(valid_ref, dest_ref, assign_ref):
    i = pl.program_id(0)

    @pl.when(i == 0)
    def _():
        def mark_block(b, carry):
            def mark(g, inner):
                for j in range(GROUP):
                    assign_ref[b * EXPERT_BLOCK + g * GROUP + j] = -1
                return inner
            lax.fori_loop(valid_ref[b] // GROUP, EXPERT_BLOCK // GROUP, mark, 0)
            return carry
        lax.fori_loop(0, valid_ref.shape[0], mark_block, 0)

    base = i * INVERT_TILE

    def scatter(g, carry):
        for j in range(GROUP):
            k = g * GROUP + j
            assign_ref[dest_ref[0, 0, k]] = base + k
        return carry

    lax.fori_loop(0, INVERT_TILE // GROUP, scatter, 0)


def _invert(dest, valid, n_rows):
    n_assign = dest.shape[0]
    grid_spec = pltpu.PrefetchScalarGridSpec(
        num_scalar_prefetch=1,
        grid=(n_assign // INVERT_TILE,),
        in_specs=[pl.BlockSpec((1, 1, INVERT_TILE), lambda i, valid: (i, 0, 0), memory_space=pltpu.SMEM)],
        out_specs=pl.BlockSpec(memory_space=pltpu.SMEM),
    )
    return pl.pallas_call(
        _invert_kernel,
        grid_spec=grid_spec,
        out_shape=jax.ShapeDtypeStruct((n_rows,), jnp.int32),
        compiler_params=_params(("arbitrary",)),
        name="invert_assignment",
    )(valid, dest.reshape(n_assign // INVERT_TILE, 1, INVERT_TILE))


def _dispatch_plan(route, counts):
    n = route.shape[0]
    counts = counts[0, N_GROUPS:N_GROUPS + N_EXPERTS].astype(jnp.int32)
    padded = (counts + EXPERT_BLOCK - 1) // EXPERT_BLOCK * EXPERT_BLOCK
    pad_end = jnp.cumsum(padded)
    pad_start = pad_end - padded
    e = route[:, ROUTE_E1:ROUTE_E2 + 1].astype(jnp.int32)
    rank = route[:, ROUTE_RANK1:ROUTE_RANK2 + 1].astype(jnp.int32)
    start_of = jnp.sum(jnp.where(e[..., None] == jnp.arange(N_EXPERTS), pad_start, 0), axis=-1)
    dest = (start_of + rank).reshape(-1)
    n_blocks = (2 * n + N_EXPERTS * (EXPERT_BLOCK - 1) + EXPERT_BLOCK - 1) // EXPERT_BLOCK
    n_rows = n_blocks * EXPERT_BLOCK
    block = jnp.arange(n_blocks, dtype=jnp.int32)
    block_expert = jnp.minimum(jnp.sum(block[:, None] * EXPERT_BLOCK >= pad_end[None, :], axis=1), N_EXPERTS - 1)
    valid = jnp.clip(jnp.sum(jnp.where(block_expert[:, None] == jnp.arange(N_EXPERTS), pad_start + counts, 0), axis=1)
                     - block * EXPERT_BLOCK, 0, EXPERT_BLOCK)
    copies = (valid + GROUP - 1) // GROUP * GROUP
    assign = _invert(dest, valid.astype(jnp.int32), n_rows)
    used_blocks = (pad_end[-1] // EXPERT_BLOCK).reshape(1)
    row = jnp.arange(n_rows, dtype=jnp.int32)
    spare = 2 * n + ((row // EXPERT_BLOCK) % 2) * GROUP + row % GROUP
    src_token = jnp.where(assign < 0, 0, assign // 2).reshape(n_blocks, 1, EXPERT_BLOCK)
    dst_slot = jnp.where(assign < 0, spare, assign).reshape(n_blocks, 1, EXPERT_BLOCK)
    return (src_token, dst_slot, block_expert.astype(jnp.int32), copies.astype(jnp.int32),
            used_blocks.astype(jnp.int32))


def _tile_rows(i):
    return pl.ds(pl.multiple_of(i * ROW_TILE, ROW_TILE), ROW_TILE)


def _expert_kernel(be_ref, copies_ref, used_ref, src_ref, src_next_ref, dst_ref, dst_prev_ref, h_ref,
                   wg_ref, wu_ref, wd_ref, slots_ref, x_buf, y_buf, wg_s, wu_s, wd_s, gather_sem, scatter_sem):
    i = pl.program_id(0)
    used = used_ref[0]
    cur = i % 2

    def gather_copy(idx_ref, r, buf):
        return pltpu.make_async_copy(h_ref.at[_tile_rows(idx_ref[0, 0, r])], x_buf.at[buf, _tile_rows(r)],
                                     gather_sem.at[buf])

    def scatter_copy(idx_ref, r, buf):
        return pltpu.make_async_copy(y_buf.at[buf, _tile_rows(r)], slots_ref.at[_tile_rows(idx_ref[0, 0, r])],
                                     scatter_sem.at[buf])

    def for_each_group(block, fn):
        def body(g, carry):
            for j in range(GROUP):
                fn(g * GROUP + j, j % 2)
            return carry
        lax.fori_loop(0, copies_ref[block] // GROUP, body, 0)

    @pl.when(i == 0)
    def _():
        x_buf[...] = jnp.zeros_like(x_buf)
        spare_rows = 2 * GROUP * ROW_TILE
        zero_spare = pltpu.make_async_copy(x_buf.at[0, pl.ds(0, spare_rows)],
                                           slots_ref.at[pl.ds(slots_ref.shape[0] - spare_rows, spare_rows)],
                                           scatter_sem.at[0])
        zero_spare.start()
        zero_spare.wait()
        for_each_group(0, lambda r, p: gather_copy(src_ref, r, 0).start(priority=1))

    @pl.when(i + 1 < used)
    def _():
        for_each_group(i + 1, lambda r, p: gather_copy(src_next_ref, r, 1 - cur).start(priority=1))

    has_prev = jnp.logical_and(i > 0, i < used)

    @pl.when(has_prev)
    def _():
        for_each_group(i - 1, lambda r, p: scatter_copy(dst_prev_ref, r, 1 - cur).start(priority=p))

    @pl.when(jnp.logical_or(i == 0, be_ref[i] != be_ref[jnp.maximum(i - 1, 0)]))
    def _():
        wg_s[...] = wg_ref[0, 0].astype(BF16)
        wu_s[...] = wu_ref[0, 0].astype(BF16)
        wd_s[...] = wd_ref[0, 0].astype(BF16)

    @pl.when(i < used)
    def _():
        for_each_group(i, lambda r, p: gather_copy(src_ref, 0, cur).wait())
        x = jnp.concatenate([_load_token_slab(x_buf.at[cur], EXPERT_BLOCK, ROW_TILE, c) for c in range(ROW_TILE)],
                            axis=1).astype(BF16)
        hidden = _silu(_dot(x, wg_s[...])) * _dot(x, wu_s[...])
        _store_token_tiles(y_buf.at[cur], _dot(hidden.astype(BF16), wd_s[...]), ROW_TILE)

    @pl.when(has_prev)
    def _():
        for_each_group(i - 1, lambda r, p: scatter_copy(dst_prev_ref, 0, 1 - cur).wait())

    @pl.when(i == used - 1)
    def _():
        for_each_group(i, lambda r, p: scatter_copy(dst_ref, r, cur).start(priority=p))
        for_each_group(i, lambda r, p: scatter_copy(dst_ref, 0, cur).wait())


def _experts(h2_tiles, plan, w_gate, w_up, w_down, layer):
    src_token, dst_slot, block_expert, copies, used_blocks = plan
    n_blocks = src_token.shape[0]
    n_slots = 2 * (h2_tiles.shape[0] // ROW_TILE) + 2 * GROUP
    last = n_blocks - 1

    def smem_block(index):
        return pl.BlockSpec((1, 1, EXPERT_BLOCK), index, memory_space=pltpu.SMEM)

    def weight(shape):
        return pl.BlockSpec((1, 1) + shape, lambda i, be, copies, used: (layer, be[i], 0, 0))

    grid_spec = pltpu.PrefetchScalarGridSpec(
        num_scalar_prefetch=3,
        grid=(n_blocks,),
        in_specs=[
            smem_block(lambda i, be, copies, used: (i, 0, 0)),
            smem_block(lambda i, be, copies, used: (jnp.minimum(i + 1, last), 0, 0)),
            smem_block(lambda i, be, copies, used: (i, 0, 0)),
            smem_block(lambda i, be, copies, used: (jnp.maximum(i - 1, 0), 0, 0)),
            pl.BlockSpec(memory_space=pl.ANY),
            weight((D_MODEL, EXPERT_FF)), weight((D_MODEL, EXPERT_FF)), weight((EXPERT_FF, D_MODEL)),
        ],
        out_specs=pl.BlockSpec(memory_space=pl.ANY),
        scratch_shapes=[
            pltpu.VMEM((2, EXPERT_BLOCK * ROW_TILE, LANES), F32),
            pltpu.VMEM((2, EXPERT_BLOCK * ROW_TILE, LANES), F32),
            pltpu.VMEM((D_MODEL, EXPERT_FF), BF16),
            pltpu.VMEM((D_MODEL, EXPERT_FF), BF16),
            pltpu.VMEM((EXPERT_FF, D_MODEL), BF16),
            pltpu.SemaphoreType.DMA((2,)),
            pltpu.SemaphoreType.DMA((2,)),
        ],
    )
    return pl.pallas_call(
        _expert_kernel,
        grid_spec=grid_spec,
        out_shape=jax.ShapeDtypeStruct((n_slots * ROW_TILE, LANES), F32),
        compiler_params=_params(("arbitrary",), VMEM_LIMIT),
        name="experts",
    )(block_expert, copies, used_blocks, src_token, src_token, dst_slot, dst_slot, h2_tiles, w_gate, w_up, w_down)


def _final_kernel(slots_ref, x_ref, route_ref, mod_ref, fg_ref, o_ref):
    sq = _moe_residual(slots_ref, x_ref, route_ref, mod_ref, o_ref)
    o_ref[...] = o_ref[...] * lax.rsqrt(sq / D_MODEL + EPS) * fg_ref[...]


def _final_combine(slots, x_lat, route, mod, final_g, n_batch):
    lat_tiles = SEQ // TM
    row = lambda i: (i, 0)
    return pl.pallas_call(
        _final_kernel,
        grid=(n_batch * lat_tiles,),
        in_specs=[
            pl.BlockSpec((TM * 2 * ROW_TILE, LANES), row),
            pl.BlockSpec((TM, D_MODEL), row),
            pl.BlockSpec((TM, LANES), row),
            pl.BlockSpec((1, 6, D_MODEL), lambda i: (i // lat_tiles, 0, 0)),
            pl.BlockSpec((1, D_MODEL), lambda i: (0, 0)),
        ],
        out_specs=pl.BlockSpec((TM, D_MODEL), row),
        out_shape=jax.ShapeDtypeStruct((n_batch * SEQ, D_MODEL), F32),
        compiler_params=_params(("parallel",)),
        name="final_combine",
    )(slots, x_lat, route, mod, final_g.reshape(1, D_MODEL))


def _rope_tables():
    t = jnp.arange(SEQ)
    rows = (t // GRID_W).astype(F32)
    cols = (t % GRID_W).astype(F32)
    n_freq = HEAD_DIM // 4
    inv_freq = ROPE_BASE ** (-jnp.arange(n_freq, dtype=F32) / n_freq)
    ang_r = rows[:, None] * inv_freq
    ang_c = cols[:, None] * inv_freq
    cos = jnp.concatenate([jnp.cos(ang_r)] * 2 + [jnp.cos(ang_c)] * 2, axis=1)
    sin = jnp.concatenate([-jnp.sin(ang_r), jnp.sin(ang_r), -jnp.sin(ang_c), jnp.sin(ang_c)], axis=1)
    return cos, sin


def kernel(x, c, ctx, c_ctx, w_mod, b_mod, norm1_g, norm2_g, w_in, ret_decay, ret_norm_g, conv_w, conv_b,
           mlstm_gate_b, mlstm_norm_g, na_rpb, w_branch, w_out, w_group, w_router, w_expert_gate,
           w_expert_up, w_expert_down, final_norm_g):
    n_batch, seq, d = x.shape
    depth = w_mod.shape[0]
    assert (seq, d, ctx.shape[1]) == (SEQ, D_MODEL, CTX_LEN)
    n = n_batch * TOK

    cond_rows = -(-(n_batch + 1) // 8) * 8
    cond = jnp.zeros((cond_rows, d), F32).at[:n_batch].set(c).at[n_batch].set(c_ctx)
    mod_all = _modulation(cond, w_mod, b_mod).reshape(depth, cond_rows, 6, d)

    cos, sin = _rope_tables()
    x_all = jnp.concatenate([ctx, x], axis=1).reshape(n, d)

    w_packed = _pack_in_weights(w_in)
    w_branch_b = w_branch.astype(BF16)
    w_out_b = w_out.astype(BF16)
    w_route = jnp.concatenate([w_group, w_router], axis=-1)
    w_route = jnp.pad(w_route, ((0, 0), (0, 0), (0, LANES - w_route.shape[-1])))
    wr_hi = w_route.astype(BF16)
    wr_lo = (w_route - wr_hi.astype(F32)).astype(BF16)
    w_route_split = jnp.concatenate([wr_hi, wr_lo], axis=-1)
    gate_b = jnp.pad(mlstm_gate_b.reshape(depth, 1, 4 * N_HEADS), ((0, 0), (0, 0), (0, LANES - 4 * N_HEADS)))
    na_bias = _na_bias_tables(na_rpb)

    moe = None
    for layer in range(depth):
        mod = mod_all[layer]
        if moe is None:
            ret, ml, gates, na, bg = _in_projection(x_all, mod, norm1_g, w_packed, layer, n_batch)
        else:
            x_all, ret, ml, gates, na, bg = _in_projection(x_all, mod, norm1_g, w_packed, layer, n_batch, moe)
        r_out = _retention(ret, ret_decay[layer], ret_norm_g[layer], cos, sin, n_batch)
        m_out = _mlstm(ml, gates, gate_b[layer], conv_w[layer], conv_b[layer].reshape(1, 2 * WIDTH),
                       mlstm_norm_g[layer], n_batch)
        a_out = _neighbourhood_attention(na, na_bias, layer, n_batch)
        x_all, h2, logits = _merge(r_out, m_out, a_out, bg, x_all, mod, w_branch_b, w_out_b, norm2_g,
                                   w_route_split, layer, n_batch, latent_only=layer == depth - 1)
        route, counts = _route(logits)
        plan = _dispatch_plan(route, counts)
        slots = _experts(h2, plan, w_expert_gate, w_expert_up, w_expert_down, layer)
        moe = (slots, route, mod)

    return _final_combine(slots, x_all, route, mod, final_norm_g, n_batch).reshape(n_batch, SEQ, d)
```
